```python
import math
import jax, jax.numpy as jnp
from jax import lax
import numpy as np

D_MODEL = 4096
BATCH = 1
SEQ = 16384
DEPTH = 2

N_A_LAYERS = DEPTH - DEPTH // 2
N_B_LAYERS = DEPTH // 2
CONV_WIDTH = 3
CONV_DIM = D_MODEL
HEAD_DIM = 64
N_HEADS = D_MODEL // HEAD_DIM
N_KV_HEADS = N_HEADS // 8
GROUP = N_HEADS // N_KV_HEADS
ATTN_DIM = N_HEADS * HEAD_DIM
KV_DIM = N_KV_HEADS * HEAD_DIM
WINDOW = 128
BLOCK = 128
N_BUCKETS = 32
MAX_DISTANCE = 128
EPS = 1e-6

kernel_name = "yoco_shortconv_swa_sink_hybrid"


def rmsnorm(x, g):
    x32 = x.astype(jnp.float32)
    y = x32 * lax.rsqrt(jnp.mean(x32 * x32, axis=-1, keepdims=True) + EPS)
    return (y * g.astype(jnp.float32)).astype(x.dtype)


def modulated_rmsnorm(x, g, shift, scale):
    return rmsnorm(x, g) * (1 + scale[:, None, :]) + shift[:, None, :]


def adaln_params(c_act, w_mod, b_mod, n_chunks):
    mod = c_act @ w_mod + b_mod
    return jnp.split(mod, n_chunks, axis=-1)


def causal_short_conv(u, w):
    s_ = u.shape[1]
    up = jnp.pad(u, ((0, 0), (CONV_WIDTH - 1, 0), (0, 0)))
    y = w[0] * up[:, 0:s_]
    for k in range(1, CONV_WIDTH):
        y = y + w[k] * up[:, k:k + s_]
    return y


def short_conv_mixer(h, w_in, conv_w, w_out):
    proj = h @ w_in
    b_gate, c_gate, u, z = jnp.split(proj, 4, axis=-1)
    y = b_gate * causal_short_conv(c_gate * u, conv_w)
    return (y * jax.nn.silu(z)) @ w_out


def t5_bucket(dist):
    max_exact = N_BUCKETS // 2
    d = jnp.maximum(dist, 0)
    d_f = jnp.maximum(d, 1).astype(jnp.float32)
    large = max_exact + (jnp.log(d_f / max_exact) / math.log(MAX_DISTANCE / max_exact)
                         * (N_BUCKETS - max_exact)).astype(jnp.int32)
    large = jnp.minimum(large, N_BUCKETS - 1)
    return jnp.where(d < max_exact, d, large)


def band_geometry(rel_bias):
    a = jnp.arange(BLOCK)[:, None]
    b = jnp.arange(2 * BLOCK)[None, :]
    dist = a + BLOCK - b
    band = (dist >= 0) & (dist < WINDOW)
    bias = rel_bias[t5_bucket(dist)]
    bias = bias.transpose(2, 0, 1).reshape(N_KV_HEADS, GROUP, BLOCK, 2 * BLOCK)
    return band, bias.astype(jnp.float32)


def sliding_window_attention(q, k, v, band, bias, sink):
    b_, s_ = q.shape[:2]
    nb = s_ // BLOCK
    qb = q.reshape(b_, nb, BLOCK, N_KV_HEADS, GROUP, HEAD_DIM).transpose(1, 0, 2, 3, 4, 5)

    def windows(t):
        tb = t.reshape(b_, nb, BLOCK, N_KV_HEADS, HEAD_DIM)
        prev = jnp.concatenate([jnp.zeros_like(tb[:, :1]), tb[:, :-1]], axis=1)
        return jnp.concatenate([prev, tb], axis=2).transpose(1, 0, 2, 3, 4)

    kw, vw = windows(k), windows(v)
    key_off = jnp.arange(2 * BLOCK) - BLOCK
    sink_l = sink.astype(jnp.float32).reshape(N_KV_HEADS, GROUP, 1, 1)

    def one_block(args):
        qi, ki, vi, blk = args
        logits = jnp.einsum('bqhgd,bkhd->bhgqk', qi, ki).astype(jnp.float32) + bias
        valid = band & ((blk * BLOCK + key_off) >= 0)[None, :]
        logits = jnp.where(valid, logits, -jnp.inf)
        m = jnp.maximum(jnp.max(logits, axis=-1, keepdims=True), sink_l)
        p = jnp.exp(logits - m)
        denom = jnp.sum(p, axis=-1, keepdims=True) + jnp.exp(sink_l - m)
        out = jnp.einsum('bhgqk,bkhd->bqhgd', p / denom, vi.astype(jnp.float32))
        return out.astype(qi.dtype)

    out = lax.map(one_block, (qb, kw, vw, jnp.arange(nb)))
    return out.transpose(1, 0, 2, 3, 4, 5).reshape(b_, s_, ATTN_DIM)


def setup_inputs(seed: int = 0) -> dict:
    key = jax.random.key(seed)
    ks = jax.random.split(key, 24)
    D = D_MODEL
    f = jnp.float32
    nrm = lambda k, shape, std: (jax.random.normal(k, shape, f) * std).astype(f)
    mod_std = 0.2 * D ** -0.5
    return {
        "x": nrm(ks[0], (BATCH, SEQ, D), 1.0),
        "c": nrm(ks[1], (BATCH, D), 1.0),
        "g_a": 1.0 + nrm(ks[2], (N_A_LAYERS, D), 0.05),
        "w_mod_a": nrm(ks[3], (N_A_LAYERS, D, 3 * D), mod_std),
        "b_mod_a": nrm(ks[4], (N_A_LAYERS, 3 * D), 0.05),
        "w_a_in": nrm(ks[5], (N_A_LAYERS, D, 4 * CONV_DIM), D ** -0.5),
        "conv_a": nrm(ks[6], (N_A_LAYERS, CONV_WIDTH, CONV_DIM), CONV_WIDTH ** -0.5),
        "w_a_out": nrm(ks[7], (N_A_LAYERS, CONV_DIM, D), CONV_DIM ** -0.5),
        "g_kv": 1.0 + nrm(ks[8], (D,), 0.05),
        "w_mod_kv": nrm(ks[9], (D, 2 * D), mod_std),
        "b_mod_kv": nrm(ks[10], (2 * D,), 0.05),
        "w_kv": nrm(ks[11], (D, 2 * KV_DIM), D ** -0.5),
        "rel_bias": nrm(ks[12], (N_BUCKETS, N_HEADS), 0.5),
        "g_b": 1.0 + nrm(ks[13], (N_B_LAYERS, D), 0.05),
        "w_mod_b": nrm(ks[14], (N_B_LAYERS, D, 3 * D), mod_std),
        "b_mod_b": nrm(ks[15], (N_B_LAYERS, 3 * D), 0.05),
        "w_b_in": nrm(ks[16], (N_B_LAYERS, D, 2 * ATTN_DIM), D ** -0.5),
        "sinks_b": nrm(ks[17], (N_B_LAYERS, N_HEADS), 1.0),
        "w_b_out": nrm(ks[18], (N_B_LAYERS, ATTN_DIM, D), ATTN_DIM ** -0.5),
        "g_final": 1.0 + nrm(ks[19], (D,), 0.05),
    }


def reference(x, c, g_a, w_mod_a, b_mod_a, w_a_in, conv_a, w_a_out, g_kv, w_mod_kv, b_mod_kv,
              w_kv, rel_bias, g_b, w_mod_b, b_mod_b, w_b_in, sinks_b, w_b_out, g_final):
    b_, s_ = x.shape[:2]
    c_act = jax.nn.silu(c)
    band, bias = band_geometry(rel_bias)
    k = None
    v = None
    for layer in range(DEPTH):
        if layer < N_A_LAYERS:
            i = layer
            shift, scale, gate = adaln_params(c_act, w_mod_a[i], b_mod_a[i], 3)
            h = modulated_rmsnorm(x, g_a[i], shift, scale)
            x = x + gate[:, None, :] * short_conv_mixer(h, w_a_in[i], conv_a[i], w_a_out[i])
        else:
            i = layer - N_A_LAYERS
            if i == 0:
                shift_kv, scale_kv = adaln_params(c_act, w_mod_kv, b_mod_kv, 2)
                h_kv = modulated_rmsnorm(x, g_kv, shift_kv, scale_kv)
                k, v = jnp.split(h_kv @ w_kv, 2, axis=-1)
                k = k.reshape(b_, s_, N_KV_HEADS, HEAD_DIM)
                v = v.reshape(b_, s_, N_KV_HEADS, HEAD_DIM)
            shift, scale, gate = adaln_params(c_act, w_mod_b[i], b_mod_b[i], 3)
            h = modulated_rmsnorm(x, g_b[i], shift, scale)
            q, z = jnp.split(h @ w_b_in[i], 2, axis=-1)
            q = q.reshape(b_, s_, N_HEADS, HEAD_DIM) * (HEAD_DIM ** -0.5)
            attn = sliding_window_attention(q, k, v, band, bias, sinks_b[i])
            x = x + gate[:, None, :] * ((attn * jax.nn.silu(z)) @ w_b_out[i])
    return rmsnorm(x, g_final)
```

```python
import functools
import math

import jax
import jax.numpy as jnp
from jax import lax
from jax.experimental import pallas as pl
from jax.experimental.pallas import tpu as pltpu

HEAD_DIM = 64
GROUP = 8
BLOCK = 128
N_BUCKETS = 32
MAX_DISTANCE = 128
CONV_WIDTH = 3
EPS = 1e-6

LANES = 128
SUBLANES = 8
MIB = 1024 * 1024
VMEM_LIMIT = 56 * MIB

F32 = jnp.float32
BF16 = jnp.bfloat16
NORM_ROWS = 128


def _params(*semantics):
    return pltpu.CompilerParams(dimension_semantics=semantics, vmem_limit_bytes=VMEM_LIMIT)


def _silu(v):
    return v * jax.nn.sigmoid(v)


def _adaln_kernel(cb_ref, w_ref, b_ref, o_ref):
    ca = _silu(cb_ref[...])
    for g in range(o_ref.shape[1] // LANES):
        cols = slice(g * LANES, (g + 1) * LANES)
        o_ref[:, cols] = jnp.sum(w_ref[:, cols] * ca, axis=0, keepdims=True) + b_ref[:, cols]


def _adaln(cb, w, b, tn=1024):
    k, n = w.shape
    return pl.pallas_call(
        _adaln_kernel,
        grid=(n // tn,),
        in_specs=[
            pl.BlockSpec((k, LANES), lambda j: (0, 0)),
            pl.BlockSpec((k, tn), lambda j: (0, j)),
            pl.BlockSpec((1, tn), lambda j: (0, j)),
        ],
        out_specs=pl.BlockSpec((1, tn), lambda j: (0, j)),
        out_shape=jax.ShapeDtypeStruct((1, n), F32),
        compiler_params=_params("arbitrary"),
        name="adaln_matvec",
    )(cb, w, b.reshape(1, n))


def _modnorm_to_scratch(x_ref, g_ref, sh_ref, sc_ref, h_scr):
    gain = g_ref[...]
    one_plus = 1.0 + sc_ref[...]
    shift = sh_ref[...]

    def body(r, carry):
        rows = pl.ds(pl.multiple_of(r * NORM_ROWS, NORM_ROWS), NORM_ROWS)
        xr = x_ref[rows, :]
        ms = jnp.mean(xr * xr, axis=-1, keepdims=True)
        y = xr * lax.rsqrt(ms + EPS)
        h_scr[rows, :] = ((y * gain) * one_plus + shift).astype(h_scr.dtype)
        return carry

    lax.fori_loop(0, x_ref.shape[0] // NORM_ROWS, body, 0)


def _conv_in_kernel(x_ref, g_ref, sh_ref, sc_ref, wb_ref, wc_ref, wu_ref, wz_ref, cw_ref,
                    y_ref, h_scr, halo_scr):
    i = pl.program_id(0)
    j = pl.program_id(1)
    tm = y_ref.shape[0]

    @pl.when(j == 0)
    def _():
        _modnorm_to_scratch(x_ref, g_ref, sh_ref, sc_ref, h_scr)

    @pl.when(i == 0)
    def _():
        halo_scr[j] = jnp.zeros(halo_scr.shape[1:], F32)

    h = h_scr[...]
    dot = functools.partial(jnp.dot, preferred_element_type=F32)
    cu = dot(h, wc_ref[...]) * dot(h, wu_ref[...])
    w0 = cw_ref[0:1, :]
    w1 = cw_ref[1:2, :]
    w2 = cw_ref[2:3, :]
    conv = w0 * pltpu.roll(cu, 2, axis=0) + w1 * pltpu.roll(cu, 1, axis=0) + w2 * cu
    gate = dot(h, wb_ref[...]) * _silu(dot(h, wz_ref[...]))
    y_ref[...] = (gate * conv).astype(y_ref.dtype)

    prev = halo_scr[j]
    top = cu[0:SUBLANES]
    row = lax.broadcasted_iota(jnp.int32, top.shape, 0)
    back1 = jnp.where(row < 1, pltpu.roll(prev, 1, axis=0), pltpu.roll(top, 1, axis=0))
    back2 = jnp.where(row < 2, pltpu.roll(prev, 2, axis=0), pltpu.roll(top, 2, axis=0))
    conv_top = w0 * back2 + w1 * back1 + w2 * top
    y_ref[0:SUBLANES, :] = (gate[0:SUBLANES] * conv_top).astype(y_ref.dtype)
    halo_scr[j] = cu[tm - SUBLANES:tm]


def _conv_in(x, g, shift, scale, w_in, conv_w, tm=512, tn=256):
    s, d = x.shape
    c = conv_w.shape[1]
    nj = c // tn
    vec = pl.BlockSpec((1, d), lambda i, j: (0, 0))
    w_specs = [pl.BlockSpec((d, tn), lambda i, j, k=k: (0, k * nj + j)) for k in range(4)]
    return pl.pallas_call(
        _conv_in_kernel,
        grid=(s // tm, nj),
        in_specs=[pl.BlockSpec((tm, d), lambda i, j: (i, 0)), vec, vec, vec, *w_specs,
                  pl.BlockSpec((CONV_WIDTH, tn), lambda i, j: (0, j))],
        out_specs=pl.BlockSpec((tm, tn), lambda i, j: (i, j)),
        out_shape=jax.ShapeDtypeStruct((s, c), BF16),
        scratch_shapes=[pltpu.VMEM((tm, d), BF16), pltpu.VMEM((nj, SUBLANES, tn), F32)],
        compiler_params=_params("arbitrary", "arbitrary"),
        name="conv_in_proj",
    )(x, g, shift, scale, w_in, w_in, w_in, w_in, conv_w)


def _out_proj_kernel(y_ref, w_ref, x_ref, gate_ref, o_ref):
    acc = jnp.dot(y_ref[...], w_ref[...], preferred_element_type=F32)
    o_ref[...] = x_ref[...] + gate_ref[...] * acc


def _out_proj(y, w, x, gate, tm=1024, tn=512):
    s, k = y.shape
    d = w.shape[1]
    return pl.pallas_call(
        _out_proj_kernel,
        grid=(s // tm, d // tn),
        in_specs=[pl.BlockSpec((tm, k), lambda i, j: (i, 0)),
                  pl.BlockSpec((k, tn), lambda i, j: (0, j)),
                  pl.BlockSpec((tm, tn), lambda i, j: (i, j)),
                  pl.BlockSpec((1, tn), lambda i, j: (0, j))],
        out_specs=pl.BlockSpec((tm, tn), lambda i, j: (i, j)),
        out_shape=jax.ShapeDtypeStruct((s, d), F32),
        compiler_params=_params("arbitrary", "arbitrary"),
        name="out_proj",
    )(y, w, x, gate)


def _out_proj_final_kernel(y_ref, w_ref, x_ref, gate_ref, gf_ref, o_ref):
    j = pl.program_id(1)
    tn = x_ref.shape[1]
    acc = jnp.dot(y_ref[...], w_ref[...], preferred_element_type=F32)
    o_ref[:, pl.ds(pl.multiple_of(j * tn, tn), tn)] = x_ref[...] + gate_ref[...] * acc

    @pl.when(j == pl.num_programs(1) - 1)
    def _():
        gain = gf_ref[...]

        def body(r, carry):
            rows = pl.ds(pl.multiple_of(r * NORM_ROWS, NORM_ROWS), NORM_ROWS)
            xr = o_ref[rows, :]
            ms = jnp.mean(xr * xr, axis=-1, keepdims=True)
            o_ref[rows, :] = (xr * lax.rsqrt(ms + EPS)) * gain
            return carry

        lax.fori_loop(0, o_ref.shape[0] // NORM_ROWS, body, 0)


def _out_proj_final(y, w, x, gate, g_final, tm=512, tn=512):
    s, k = y.shape
    d = w.shape[1]
    return pl.pallas_call(
        _out_proj_final_kernel,
        grid=(s // tm, d // tn),
        in_specs=[pl.BlockSpec((tm, k), lambda i, j: (i, 0)),
                  pl.BlockSpec((k, tn), lambda i, j: (0, j)),
                  pl.BlockSpec((tm, tn), lambda i, j: (i, j)),
                  pl.BlockSpec((1, tn), lambda i, j: (0, j)),
                  pl.BlockSpec((1, d), lambda i, j: (0, 0))],
        out_specs=pl.BlockSpec((tm, d), lambda i, j: (i, 0)),
        out_shape=jax.ShapeDtypeStruct((s, d), F32),
        compiler_params=_params("arbitrary", "arbitrary"),
        name="out_proj_final_norm",
    )(y, w, x, gate, g_final)


def _norm_proj_kernel(x_ref, g_ref, sh_ref, sc_ref, w_ref, o_ref, h_scr, *, scaled_tiles, scale):
    j = pl.program_id(1)

    @pl.when(j == 0)
    def _():
        _modnorm_to_scratch(x_ref, g_ref, sh_ref, sc_ref, h_scr)

    acc = jnp.dot(h_scr[...], w_ref[...], preferred_element_type=F32)
    if scaled_tiles:
        acc = acc * jnp.where(j < scaled_tiles, scale, 1.0)
    o_ref[...] = acc.astype(o_ref.dtype)


def _norm_proj(x, g, shift, scale, w, scaled_cols=0, col_scale=1.0, tm=512, tn=1024):
    s, d = x.shape
    n = w.shape[1]
    assert scaled_cols % tn == 0
    vec = pl.BlockSpec((1, d), lambda i, j: (0, 0))
    body = functools.partial(_norm_proj_kernel, scaled_tiles=scaled_cols // tn, scale=col_scale)
    return pl.pallas_call(
        body,
        grid=(s // tm, n // tn),
        in_specs=[pl.BlockSpec((tm, d), lambda i, j: (i, 0)), vec, vec, vec,
                  pl.BlockSpec((d, tn), lambda i, j: (0, j))],
        out_specs=pl.BlockSpec((tm, tn), lambda i, j: (i, j)),
        out_shape=jax.ShapeDtypeStruct((s, n), BF16),
        scratch_shapes=[pltpu.VMEM((tm, d), BF16)],
        compiler_params=_params("arbitrary", "arbitrary"),
        name="norm_proj",
    )(x, g, shift, scale, w)


def _t5_bucket(dist):
    max_exact = N_BUCKETS // 2
    d = jnp.maximum(dist, 0)
    d_f = jnp.maximum(d, 1).astype(F32)
    large = max_exact + (jnp.log(d_f / max_exact) / math.log(MAX_DISTANCE / max_exact)
                         * (N_BUCKETS - max_exact)).astype(jnp.int32)
    large = jnp.minimum(large, N_BUCKETS - 1)
    return jnp.where(d < max_exact, d, large)


def _band_buckets():
    a = jnp.arange(BLOCK)[:, None]
    b = jnp.arange(2 * BLOCK)[None, :]
    dist = a + BLOCK - b
    band = (dist >= 0) & (dist < BLOCK)
    return jnp.where(band, _t5_bucket(dist), -1).astype(jnp.int32)


def _bias_table_kernel(rb_ref, bucket_ref, o_ref):
    g = pl.program_id(0)
    bucket = bucket_ref[...]
    for hh in range(GROUP):
        acc = jnp.full(bucket.shape, -jnp.inf, F32)
        for k in range(N_BUCKETS):
            acc = jnp.where(bucket == k, rb_ref[k, g * GROUP + hh], acc)
        o_ref[0, hh * BLOCK:(hh + 1) * BLOCK, :] = acc


def _bias_table(rel_bias):
    n_kv = rel_bias.shape[1] // GROUP
    return pl.pallas_call(
        _bias_table_kernel,
        grid=(n_kv,),
        in_specs=[pl.BlockSpec(memory_space=pltpu.SMEM),
                  pl.BlockSpec((BLOCK, 2 * BLOCK), lambda g: (0, 0))],
        out_specs=pl.BlockSpec((1, GROUP * BLOCK, 2 * BLOCK), lambda g: (g, 0, 0)),
        out_shape=jax.ShapeDtypeStruct((n_kv, GROUP * BLOCK, 2 * BLOCK), F32),
        compiler_params=_params("arbitrary"),
        name="bias_table",
    )(rel_bias, _band_buckets())


def _attn_kernel(sink_ref, q_ref, z_ref, kp_ref, kc_ref, vp_ref, vc_ref, bias_ref, o_ref):
    i = pl.program_id(0)
    n_kv = bias_ref.shape[0]
    col = lax.broadcasted_iota(jnp.int32, (1, 2 * BLOCK), 1)
    no_prev = jnp.where((col < BLOCK) & (i == 0), -jnp.inf, 0.0).astype(F32)
    for g in range(n_kv):
        kv_cols = slice(g * HEAD_DIM, (g + 1) * HEAD_DIM)
        kg = jnp.concatenate([kp_ref[:, kv_cols], kc_ref[:, kv_cols]], axis=0)
        vg = jnp.concatenate([vp_ref[:, kv_cols], vc_ref[:, kv_cols]], axis=0)
        heads = [g * GROUP + hh for hh in range(GROUP)]
        qg = jnp.concatenate([q_ref[:, h * HEAD_DIM:(h + 1) * HEAD_DIM] for h in heads], axis=0)
        logits = lax.dot_general(qg, kg, (((1,), (1,)), ((), ())), preferred_element_type=F32)
        logits = logits + bias_ref[g] + no_prev
        probs, denoms = [], []
        for hh, h in enumerate(heads):
            l = logits[hh * BLOCK:(hh + 1) * BLOCK]
            sink = sink_ref[h]
            m = jnp.maximum(jnp.max(l, axis=-1, keepdims=True), sink)
            p = jnp.exp(l - m)
            denoms.append(jnp.sum(p, axis=-1, keepdims=True) + jnp.exp(sink - m))
            probs.append(p.astype(BF16))
        og = jnp.dot(jnp.concatenate(probs, axis=0), vg, preferred_element_type=F32)
        for hh, h in enumerate(heads):
            cols = slice(h * HEAD_DIM, (h + 1) * HEAD_DIM)
            attn = og[hh * BLOCK:(hh + 1) * BLOCK] / denoms[hh]
            o_ref[:, cols] = (attn * _silu(z_ref[:, cols].astype(F32))).astype(o_ref.dtype)


def _attention(qz, kv, bias, sinks):
    s = qz.shape[0]
    attn_dim = qz.shape[1] // 2
    kv_dim = kv.shape[1] // 2
    n_kv = kv_dim // HEAD_DIM
    prev = lambda i: jnp.maximum(i - 1, 0)
    return pl.pallas_call(
        _attn_kernel,
        grid=(s // BLOCK,),
        in_specs=[pl.BlockSpec(memory_space=pltpu.SMEM),
                  pl.BlockSpec((BLOCK, attn_dim), lambda i: (i, 0)),
                  pl.BlockSpec((BLOCK, attn_dim), lambda i: (i, 1)),
                  pl.BlockSpec((BLOCK, kv_dim), lambda i: (prev(i), 0)),
                  pl.BlockSpec((BLOCK, kv_dim), lambda i: (i, 0)),
                  pl.BlockSpec((BLOCK, kv_dim), lambda i: (prev(i), 1)),
                  pl.BlockSpec((BLOCK, kv_dim), lambda i: (i, 1)),
                  pl.BlockSpec((n_kv, GROUP * BLOCK, 2 * BLOCK), lambda i: (0, 0, 0))],
        out_specs=pl.BlockSpec((BLOCK, attn_dim), lambda i: (i, 0)),
        out_shape=jax.ShapeDtypeStruct((s, attn_dim), BF16),
        compiler_params=_params("arbitrary"),
        name="swa_attention",
    )(sinks, qz, qz, kv, kv, kv, kv, bias)


def _trunk(x, c, g_a, w_mod_a, b_mod_a, w_a_in, conv_a, w_a_out, g_kv, w_mod_kv, b_mod_kv,
           w_kv, rel_bias, g_b, w_mod_b, b_mod_b, w_b_in, sinks_b, w_b_out, g_final):
    s, d = x.shape
    n_a, n_b = w_a_in.shape[0], w_b_in.shape[0]
    assert n_b >= 1 and s % BLOCK == 0
    row = lambda v: v.reshape(1, -1)
    cb = jnp.broadcast_to(c.reshape(d, 1), (d, LANES))

    for i in range(n_a):
        shift, scale, gate = jnp.split(_adaln(cb, w_mod_a[i], b_mod_a[i]), 3, axis=-1)
        y = _conv_in(x, row(g_a[i]), shift, scale, w_a_in[i].astype(BF16), conv_a[i])
        x = _out_proj(y, w_a_out[i].astype(BF16), x, gate)

    shift_kv, scale_kv = jnp.split(_adaln(cb, w_mod_kv, b_mod_kv), 2, axis=-1)
    kv = _norm_proj(x, row(g_kv), shift_kv, scale_kv, w_kv.astype(BF16))
    bias = _bias_table(rel_bias)
    attn_dim = w_b_in.shape[2] // 2
    for i in range(n_b):
        shift, scale, gate = jnp.split(_adaln(cb, w_mod_b[i], b_mod_b[i]), 3, axis=-1)
        qz = _norm_proj(x, row(g_b[i]), shift, scale, w_b_in[i].astype(BF16),
                        scaled_cols=attn_dim, col_scale=HEAD_DIM ** -0.5)
        a = _attention(qz, kv, bias, sinks_b[i])
        w_out = w_b_out[i].astype(BF16)
        if i + 1 < n_b:
            x = _out_proj(a, w_out, x, gate)
        else:
            x = _out_proj_final(a, w_out, x, gate, row(g_final))
    return x


def kernel(x, c, g_a, w_mod_a, b_mod_a, w_a_in, conv_a, w_a_out, g_kv, w_mod_kv, b_mod_kv, w_kv,
           rel_bias, g_b, w_mod_b, b_mod_b, w_b_in, sinks_b, w_b_out, g_final):
    outs = [_trunk(x[b], c[b], g_a, w_mod_a, b_mod_a, w_a_in, conv_a, w_a_out, g_kv, w_mod_kv,
                   b_mod_kv, w_kv, rel_bias, g_b, w_mod_b, b_mod_b, w_b_in, sinks_b, w_b_out,
                   g_final) for b in range(x.shape[0])]
    return jnp.stack(outs, axis=0)
```

```python
import functools
import math

import jax
import jax.numpy as jnp
from jax import lax
from jax.experimental import pallas as pl
from jax.experimental.pallas import tpu as pltpu

HEAD_DIM = 64
GROUP = 8
BLOCK = 128
N_BUCKETS = 32
MAX_DISTANCE = 128
CONV_WIDTH = 3
EPS = 1e-6

LANES = 128
SUBLANES = 8
MIB = 1024 * 1024
VMEM_LIMIT = 56 * MIB

F32 = jnp.float32
BF16 = jnp.bfloat16
NORM_ROWS = 128


def _params(*semantics, vmem=VMEM_LIMIT):
    return pltpu.CompilerParams(dimension_semantics=semantics, vmem_limit_bytes=vmem)


def _silu(v):
    return v * jax.nn.sigmoid(v)


def _adaln_kernel(cb_ref, w_ref, b_ref, o_ref):
    ca = _silu(cb_ref[...])
    for g in range(o_ref.shape[1] // LANES):
        cols = slice(g * LANES, (g + 1) * LANES)
        o_ref[:, cols] = jnp.sum(w_ref[:, cols] * ca, axis=0, keepdims=True) + b_ref[:, cols]


def _adaln(cb, w, b, tn=1024):
    k, n = w.shape
    return pl.pallas_call(
        _adaln_kernel,
        grid=(n // tn,),
        in_specs=[
            pl.BlockSpec((k, LANES), lambda j: (0, 0)),
            pl.BlockSpec((k, tn), lambda j: (0, j)),
            pl.BlockSpec((1, tn), lambda j: (0, j)),
        ],
        out_specs=pl.BlockSpec((1, tn), lambda j: (0, j)),
        out_shape=jax.ShapeDtypeStruct((1, n), F32),
        compiler_params=_params("arbitrary"),
        name="adaln_matvec",
    )(cb, w, b.reshape(1, n))


def _modnorm_to_scratch(x_ref, g_ref, sh_ref, sc_ref, h_scr, rstd_scr):
    n_mod = h_scr.shape[0]
    tm, d = x_ref.shape
    col_groups = [slice(k * LANES, (k + 1) * LANES) for k in range(d // LANES)]
    row_chunk = lambda r: pl.ds(pl.multiple_of(r * NORM_ROWS, NORM_ROWS), NORM_ROWS)

    def stats(r, carry):
        rows = row_chunk(r)
        acc = jnp.zeros((NORM_ROWS, LANES), F32)
        for cols in col_groups:
            xk = x_ref[rows, cols]
            acc = acc + xk * xk
        ms = jnp.sum(acc, axis=-1, keepdims=True) * (1.0 / d)
        rstd_scr[rows, :] = jnp.broadcast_to(lax.rsqrt(ms + EPS), (NORM_ROWS, LANES))
        return carry

    lax.fori_loop(0, tm // NORM_ROWS, stats, 0)

    def apply(r, carry):
        rows = row_chunk(r)
        rstd = rstd_scr[rows, :]
        for cols in col_groups:
            y = x_ref[rows, cols] * rstd
            for v in range(n_mod):
                gain = g_ref[v:v + 1, cols] * (1.0 + sc_ref[v:v + 1, cols])
                h_scr[v, rows, cols] = (y * gain + sh_ref[v:v + 1, cols]).astype(h_scr.dtype)
        return carry

    lax.fori_loop(0, tm // NORM_ROWS, apply, 0)


def _conv_in_kernel(x_ref, g_ref, sh_ref, sc_ref, wb_ref, wc_ref, wu_ref, wz_ref, cw_ref,
                    y_ref, h_scr, rstd_scr, halo_scr):
    i = pl.program_id(0)
    j = pl.program_id(1)
    tm = y_ref.shape[0]

    @pl.when(j == 0)
    def _():
        _modnorm_to_scratch(x_ref, g_ref, sh_ref, sc_ref, h_scr, rstd_scr)

    @pl.when(i == 0)
    def _():
        halo_scr[j] = jnp.zeros(halo_scr.shape[1:], F32)

    h = h_scr[0]
    dot = functools.partial(jnp.dot, preferred_element_type=F32)
    cu = dot(h, wc_ref[...]) * dot(h, wu_ref[...])
    w0 = cw_ref[0:1, :]
    w1 = cw_ref[1:2, :]
    w2 = cw_ref[2:3, :]
    conv = w0 * pltpu.roll(cu, 2, axis=0) + w1 * pltpu.roll(cu, 1, axis=0) + w2 * cu
    gate = dot(h, wb_ref[...]) * _silu(dot(h, wz_ref[...]))
    y_ref[...] = (gate * conv).astype(y_ref.dtype)

    prev = halo_scr[j]
    top = cu[0:SUBLANES]
    row = lax.broadcasted_iota(jnp.int32, top.shape, 0)
    back1 = jnp.where(row < 1, pltpu.roll(prev, 1, axis=0), pltpu.roll(top, 1, axis=0))
    back2 = jnp.where(row < 2, pltpu.roll(prev, 2, axis=0), pltpu.roll(top, 2, axis=0))
    conv_top = w0 * back2 + w1 * back1 + w2 * top
    y_ref[0:SUBLANES, :] = (gate[0:SUBLANES] * conv_top).astype(y_ref.dtype)
    halo_scr[j] = cu[tm - SUBLANES:tm]


def _conv_in(x, g, shift, scale, w_in, conv_w, tm=512, tn=256):
    s, d = x.shape
    c = conv_w.shape[1]
    nj = c // tn
    vec = pl.BlockSpec((1, d), lambda i, j: (0, 0))
    w_specs = [pl.BlockSpec((d, tn), lambda i, j, k=k: (0, k * nj + j)) for k in range(4)]
    return pl.pallas_call(
        _conv_in_kernel,
        grid=(s // tm, nj),
        in_specs=[pl.BlockSpec((tm, d), lambda i, j: (i, 0)), vec, vec, vec, *w_specs,
                  pl.BlockSpec((CONV_WIDTH, tn), lambda i, j: (0, j))],
        out_specs=pl.BlockSpec((tm, tn), lambda i, j: (i, j)),
        out_shape=jax.ShapeDtypeStruct((s, c), BF16),
        scratch_shapes=[pltpu.VMEM((1, tm, d), BF16), pltpu.VMEM((tm, LANES), F32),
                        pltpu.VMEM((nj, SUBLANES, tn), F32)],
        compiler_params=_params("arbitrary", "arbitrary"),
        name="conv_in_proj",
    )(x, g, shift, scale, w_in, w_in, w_in, w_in, conv_w)


def _out_proj_kernel(y_ref, w_ref, x_ref, gate_ref, o_ref):
    acc = jnp.dot(y_ref[...], w_ref[...], preferred_element_type=F32)
    o_ref[...] = x_ref[...] + gate_ref[...] * acc


def _out_proj(y, w, x, gate, tm=1024, tn=1024):
    s, k = y.shape
    d = w.shape[1]
    return pl.pallas_call(
        _out_proj_kernel,
        grid=(s // tm, d // tn),
        in_specs=[pl.BlockSpec((tm, k), lambda i, j: (i, 0)),
                  pl.BlockSpec((k, tn), lambda i, j: (0, j)),
                  pl.BlockSpec((tm, tn), lambda i, j: (i, j)),
                  pl.BlockSpec((1, tn), lambda i, j: (0, j))],
        out_specs=pl.BlockSpec((tm, tn), lambda i, j: (i, j)),
        out_shape=jax.ShapeDtypeStruct((s, d), F32),
        compiler_params=_params("arbitrary", "arbitrary"),
        name="out_proj",
    )(y, w, x, gate)


def _out_proj_norm_kernel(y_ref, w_ref, x_ref, gate_ref, gf_ref, o_ref):
    acc = jnp.dot(y_ref[...], w_ref[...], preferred_element_type=F32)
    x2 = x_ref[...] + gate_ref[...] * acc
    ms = jnp.mean(x2 * x2, axis=-1, keepdims=True)
    o_ref[...] = (x2 * lax.rsqrt(ms + EPS)) * gf_ref[...]


def _out_proj_norm(y, w, x, gate, g_final, tm=256):
    s, k = y.shape
    d = w.shape[1]
    vec = pl.BlockSpec((1, d), lambda i: (0, 0))
    return pl.pallas_call(
        _out_proj_norm_kernel,
        grid=(s // tm,),
        in_specs=[pl.BlockSpec((tm, k), lambda i: (i, 0)),
                  pl.BlockSpec((k, d), lambda i: (0, 0), pipeline_mode=pl.Buffered(1)),
                  pl.BlockSpec((tm, d), lambda i: (i, 0)), vec, vec],
        out_specs=pl.BlockSpec((tm, d), lambda i: (i, 0)),
        out_shape=jax.ShapeDtypeStruct((s, d), F32),
        compiler_params=_params("arbitrary", vmem=60 * MIB),
        name="out_proj_final_norm",
    )(y, w, x, gate, g_final)


def _norm_proj_kernel(x_ref, g_ref, sh_ref, sc_ref, w_ref, o_ref, h_scr, rstd_scr, *,
                      mod_tile_starts, scaled_tiles, scale):
    j = pl.program_id(1)

    @pl.when(j == 0)
    def _():
        _modnorm_to_scratch(x_ref, g_ref, sh_ref, sc_ref, h_scr, rstd_scr)

    mod = sum((j >= t).astype(jnp.int32) for t in mod_tile_starts[1:]) if len(
        mod_tile_starts) > 1 else 0
    acc = jnp.dot(h_scr[mod], w_ref[...], preferred_element_type=F32)
    if scaled_tiles:
        acc = acc * jnp.where(j < scaled_tiles, scale, 1.0)
    o_ref[...] = acc.astype(o_ref.dtype)


def _norm_proj(x, g, shift, scale, w, mod_col_starts=(0,), scaled_cols=0, col_scale=1.0,
               tm=512, tn=1024):
    s, d = x.shape
    n = w.shape[1]
    n_mod = g.shape[0]
    assert scaled_cols % tn == 0 and all(c % tn == 0 for c in mod_col_starts)
    assert len(mod_col_starts) == n_mod
    vec = pl.BlockSpec((n_mod, d), lambda i, j: (0, 0))
    body = functools.partial(_norm_proj_kernel, scaled_tiles=scaled_cols // tn, scale=col_scale,
                             mod_tile_starts=tuple(c // tn for c in mod_col_starts))
    return pl.pallas_call(
        body,
        grid=(s // tm, n // tn),
        in_specs=[pl.BlockSpec((tm, d), lambda i, j: (i, 0)), vec, vec, vec,
                  pl.BlockSpec((d, tn), lambda i, j: (0, j))],
        out_specs=pl.BlockSpec((tm, tn), lambda i, j: (i, j)),
        out_shape=jax.ShapeDtypeStruct((s, n), BF16),
        scratch_shapes=[pltpu.VMEM((n_mod, tm, d), BF16), pltpu.VMEM((tm, LANES), F32)],
        compiler_params=_params("arbitrary", "arbitrary"),
        name="norm_proj",
    )(x, g, shift, scale, w)


def _t5_bucket(dist):
    max_exact = N_BUCKETS // 2
    d = jnp.maximum(dist, 0)
    d_f = jnp.maximum(d, 1).astype(F32)
    large = max_exact + (jnp.log(d_f / max_exact) / math.log(MAX_DISTANCE / max_exact)
                         * (N_BUCKETS - max_exact)).astype(jnp.int32)
    large = jnp.minimum(large, N_BUCKETS - 1)
    return jnp.where(d < max_exact, d, large)


def _band_buckets():
    b = jnp.arange(2 * BLOCK)[:, None]
    a = jnp.arange(BLOCK)[None, :]
    dist = a + BLOCK - b
    band = (dist >= 0) & (dist < BLOCK)
    later = jnp.where(band, _t5_bucket(dist), -1)
    first = jnp.where(b >= BLOCK, later, -1)
    return jnp.stack([first, later]).astype(jnp.int32)


def _bias_table_kernel(rb_ref, bucket_ref, o_ref):
    g = pl.program_id(1)
    bucket = bucket_ref[0]
    for hh in range(GROUP):
        acc = jnp.full(bucket.shape, -jnp.inf, F32)
        for k in range(N_BUCKETS):
            acc = jnp.where(bucket == k, rb_ref[k, g * GROUP + hh], acc)
        o_ref[0, 0, :, hh * BLOCK:(hh + 1) * BLOCK] = acc


def _bias_table(rel_bias):
    n_kv = rel_bias.shape[1] // GROUP
    return pl.pallas_call(
        _bias_table_kernel,
        grid=(2, n_kv),
        in_specs=[pl.BlockSpec(memory_space=pltpu.SMEM),
                  pl.BlockSpec((1, 2 * BLOCK, BLOCK), lambda v, g: (v, 0, 0))],
        out_specs=pl.BlockSpec((1, 1, 2 * BLOCK, GROUP * BLOCK), lambda v, g: (v, g, 0, 0)),
        out_shape=jax.ShapeDtypeStruct((2, n_kv, 2 * BLOCK, GROUP * BLOCK), F32),
        compiler_params=_params("arbitrary", "arbitrary"),
        name="bias_table",
    )(rel_bias, _band_buckets())


def _attn_kernel(sink_ref, q_ref, z_ref, kp_ref, kc_ref, vp_ref, vc_ref, bias_ref, o_ref):
    n_kv = bias_ref.shape[1]
    zero_half = jnp.zeros((HEAD_DIM, GROUP * BLOCK), BF16)
    for g in range(n_kv):
        heads = [g * GROUP + hh for hh in range(GROUP)]
        kv_cols = slice((g // 2) * LANES, (g // 2 + 1) * LANES)
        k2 = jnp.concatenate([kp_ref[:, kv_cols], kc_ref[:, kv_cols]], axis=0)
        v2_t = jnp.concatenate([vp_ref[:, kv_cols].T, vc_ref[:, kv_cols].T], axis=1)
        q_t = []
        for h in heads[::2]:
            both = q_ref[:, (h // 2) * LANES:(h // 2 + 1) * LANES].T
            q_t += [both[:HEAD_DIM], both[HEAD_DIM:]]
        qg_t = jnp.concatenate(q_t, axis=1)
        rhs = jnp.concatenate([qg_t, zero_half] if g % 2 == 0 else [zero_half, qg_t], axis=0)
        s_t = jnp.dot(k2, rhs, preferred_element_type=F32) + bias_ref[0, g]
        sink = jnp.concatenate([jnp.full((1, BLOCK), sink_ref[h], F32) for h in heads], axis=1)
        m = jnp.maximum(jnp.max(s_t, axis=0, keepdims=True), sink)
        p = jnp.exp(s_t - m)
        den = jnp.sum(p, axis=0, keepdims=True) + jnp.exp(sink - m)
        o2_t = jnp.dot(v2_t, p.astype(BF16), preferred_element_type=F32)
        o_t = (o2_t[:HEAD_DIM] if g % 2 == 0 else o2_t[HEAD_DIM:]) / den
        for hh in range(0, GROUP, 2):
            cols = slice((heads[hh] // 2) * LANES, (heads[hh] // 2 + 1) * LANES)
            both = jnp.concatenate([o_t[:, hh * BLOCK:(hh + 1) * BLOCK],
                                    o_t[:, (hh + 1) * BLOCK:(hh + 2) * BLOCK]], axis=0)
            o_ref[:, cols] = (both.T * _silu(z_ref[:, cols].astype(F32))).astype(o_ref.dtype)


def _attention(qz, q_blk, z_blk, kv, k_blk, v_blk, bias, sinks):
    s = qz.shape[0]
    n_kv = bias.shape[1]
    kv_dim = n_kv * HEAD_DIM
    attn_dim = kv_dim * GROUP
    assert n_kv % 2 == 0 and GROUP % 2 == 0 and 2 * HEAD_DIM == LANES and BLOCK == LANES
    prev = lambda i: jnp.maximum(i - 1, 0)
    return pl.pallas_call(
        _attn_kernel,
        grid=(s // BLOCK,),
        in_specs=[pl.BlockSpec(memory_space=pltpu.SMEM),
                  pl.BlockSpec((BLOCK, attn_dim), lambda i: (i, q_blk)),
                  pl.BlockSpec((BLOCK, attn_dim), lambda i: (i, z_blk)),
                  pl.BlockSpec((BLOCK, kv_dim), lambda i: (prev(i), k_blk)),
                  pl.BlockSpec((BLOCK, kv_dim), lambda i: (i, k_blk)),
                  pl.BlockSpec((BLOCK, kv_dim), lambda i: (prev(i), v_blk)),
                  pl.BlockSpec((BLOCK, kv_dim), lambda i: (i, v_blk)),
                  pl.BlockSpec((1, n_kv, 2 * BLOCK, GROUP * BLOCK),
                               lambda i: (jnp.minimum(i, 1), 0, 0, 0))],
        out_specs=pl.BlockSpec((BLOCK, attn_dim), lambda i: (i, 0)),
        out_shape=jax.ShapeDtypeStruct((s, attn_dim), BF16),
        compiler_params=_params("arbitrary"),
        name="swa_attention",
    )(sinks, qz, qz, kv, kv, kv, kv, bias)


def _trunk(x, c, g_a, w_mod_a, b_mod_a, w_a_in, conv_a, w_a_out, g_kv, w_mod_kv, b_mod_kv,
           w_kv, rel_bias, g_b, w_mod_b, b_mod_b, w_b_in, sinks_b, w_b_out, g_final):
    s, d = x.shape
    n_a, n_b = w_a_in.shape[0], w_b_in.shape[0]
    assert n_b >= 1 and s % BLOCK == 0
    row = lambda v: v.reshape(1, -1)
    cb = jnp.broadcast_to(c.reshape(d, 1), (d, LANES))

    for i in range(n_a):
        shift, scale, gate = jnp.split(_adaln(cb, w_mod_a[i], b_mod_a[i]), 3, axis=-1)
        y = _conv_in(x, row(g_a[i]), shift, scale, w_a_in[i].astype(BF16), conv_a[i])
        x = _out_proj(y, w_a_out[i].astype(BF16), x, gate)

    shift_kv, scale_kv = jnp.split(_adaln(cb, w_mod_kv, b_mod_kv), 2, axis=-1)
    bias = _bias_table(rel_bias)
    attn_dim = w_b_in.shape[2] // 2
    kv_dim = w_kv.shape[1] // 2
    kv = None
    for i in range(n_b):
        shift, scale, gate = jnp.split(_adaln(cb, w_mod_b[i], b_mod_b[i]), 3, axis=-1)
        q_scale = dict(scaled_cols=attn_dim, col_scale=HEAD_DIM ** -0.5)
        if i == 0:
            w_cat = jnp.concatenate([w_b_in[i], w_kv], axis=1).astype(BF16)
            qz = _norm_proj(x, jnp.stack([g_b[i], g_kv]), jnp.concatenate([shift, shift_kv]),
                            jnp.concatenate([scale, scale_kv]), w_cat,
                            mod_col_starts=(0, 2 * attn_dim), **q_scale)
            kv, k_blk = qz, 2 * attn_dim // kv_dim
        else:
            qz = _norm_proj(x, row(g_b[i]), shift, scale, w_b_in[i].astype(BF16), **q_scale)
        a = _attention(qz, 0, 1, kv, k_blk, k_blk + 1, bias, sinks_b[i])
        w_out = w_b_out[i].astype(BF16)
        if i + 1 < n_b:
            x = _out_proj(a, w_out, x, gate)
        else:
            x = _out_proj_norm(a, w_out, x, gate, row(g_final))
    return x


def kernel(x, c, g_a, w_mod_a, b_mod_a, w_a_in, conv_a, w_a_out, g_kv, w_mod_kv, b_mod_kv, w_kv,
           rel_bias, g_b, w_mod_b, b_mod_b, w_b_in, sinks_b, w_b_out, g_final):
    outs = [_trunk(x[b], c[b], g_a, w_mod_a, b_mod_a, w_a_in, conv_a, w_a_out, g_kv, w_mod_kv,
                   b_mod_kv, w_kv, rel_bias, g_b, w_mod_b, b_mod_b, w_b_in, sinks_b, w_b_out,
                   g_final) for b in range(x.shape[0])]
    return jnp.stack(outs, axis=0)
```

```python
import functools
import math

import jax
import jax.numpy as jnp
from jax import lax
from jax.experimental import pallas as pl
from jax.experimental.pallas import tpu as pltpu

HEAD_DIM = 64
GROUP = 8
BLOCK = 128
N_BUCKETS = 32
MAX_DISTANCE = 128
CONV_WIDTH = 3
EPS = 1e-6

LANES = 128
SUBLANES = 8
MIB = 1024 * 1024
VMEM_LIMIT = 56 * MIB

F32 = jnp.float32
BF16 = jnp.bfloat16
NORM_ROWS = 128


def _params(*semantics, vmem=VMEM_LIMIT):
    return pltpu.CompilerParams(dimension_semantics=semantics, vmem_limit_bytes=vmem)


def _silu(v):
    return v * jax.nn.sigmoid(v)


def _adaln_kernel(cb_ref, w_ref, b_ref, o_ref):
    ca = _silu(cb_ref[...])
    for g in range(o_ref.shape[1] // LANES):
        cols = slice(g * LANES, (g + 1) * LANES)
        o_ref[:, cols] = jnp.sum(w_ref[:, cols] * ca, axis=0, keepdims=True) + b_ref[:, cols]


def _adaln(cb, w, b, tn=1024):
    k, n = w.shape
    return pl.pallas_call(
        _adaln_kernel,
        grid=(n // tn,),
        in_specs=[
            pl.BlockSpec((k, LANES), lambda j: (0, 0)),
            pl.BlockSpec((k, tn), lambda j: (0, j)),
            pl.BlockSpec((1, tn), lambda j: (0, j)),
        ],
        out_specs=pl.BlockSpec((1, tn), lambda j: (0, j)),
        out_shape=jax.ShapeDtypeStruct((1, n), F32),
        compiler_params=_params("arbitrary"),
        name="adaln_matvec",
    )(cb, w, b.reshape(1, n))


def _row_chunk(r):
    return pl.ds(pl.multiple_of(r * NORM_ROWS, NORM_ROWS), NORM_ROWS)


def _col_groups(d):
    return [slice(k * LANES, (k + 1) * LANES) for k in range(d // LANES)]


def _rms_stats(x_ref, rstd_scr):
    tm, d = x_ref.shape

    def body(r, carry):
        rows = _row_chunk(r)
        acc = jnp.zeros((NORM_ROWS, LANES), F32)
        for cols in _col_groups(d):
            xk = x_ref[rows, cols]
            acc = acc + xk * xk
        ms = jnp.sum(acc, axis=-1, keepdims=True) * (1.0 / d)
        rstd_scr[rows, :] = jnp.broadcast_to(lax.rsqrt(ms + EPS), (NORM_ROWS, LANES))
        return carry

    lax.fori_loop(0, tm // NORM_ROWS, body, 0)


def _modnorm_apply(x_ref, rstd_scr, g_ref, sh_ref, sc_ref, mod, h_scr):
    tm, d = x_ref.shape
    vec = slice(mod, mod + 1)

    def body(r, carry):
        rows = _row_chunk(r)
        rstd = rstd_scr[rows, :]
        for cols in _col_groups(d):
            gain = g_ref[vec, cols] * (1.0 + sc_ref[vec, cols])
            h_scr[rows, cols] = ((x_ref[rows, cols] * rstd) * gain
                                 + sh_ref[vec, cols]).astype(h_scr.dtype)
        return carry

    lax.fori_loop(0, tm // NORM_ROWS, body, 0)


def _row_tile_copy(x_hbm, x_buf, sem, i):
    tm = x_buf.shape[0]
    return pltpu.make_async_copy(x_hbm.at[pl.ds(i * tm, tm), :], x_buf, sem)


def _await_row_tile(x_hbm, x_buf, sem, i):
    @pl.when(i == 0)
    def _():
        _row_tile_copy(x_hbm, x_buf, sem, 0).start()

    _row_tile_copy(x_hbm, x_buf, sem, i).wait()


def _prefetch_next_row_tile(x_hbm, x_buf, sem, i):
    @pl.when(i + 1 < pl.num_programs(0))
    def _():
        _row_tile_copy(x_hbm, x_buf, sem, i + 1).start()


def _conv_in_kernel(x_hbm, g_ref, sh_ref, sc_ref, wb_ref, wc_ref, wu_ref, wz_ref, cw_ref,
                    y_ref, x_buf, x_sem, h_scr, rstd_scr, halo_scr):
    i = pl.program_id(0)
    j = pl.program_id(1)
    tm = y_ref.shape[0]

    @pl.when(j == 0)
    def _():
        _await_row_tile(x_hbm, x_buf, x_sem, i)
        _rms_stats(x_buf, rstd_scr)
        _modnorm_apply(x_buf, rstd_scr, g_ref, sh_ref, sc_ref, 0, h_scr)
        _prefetch_next_row_tile(x_hbm, x_buf, x_sem, i)

    @pl.when(i == 0)
    def _():
        halo_scr[j] = jnp.zeros(halo_scr.shape[1:], F32)

    h = h_scr[...]
    dot = functools.partial(jnp.dot, preferred_element_type=F32)
    cu = dot(h, wc_ref[...]) * dot(h, wu_ref[...])
    w0 = cw_ref[0:1, :]
    w1 = cw_ref[1:2, :]
    w2 = cw_ref[2:3, :]
    conv = w0 * pltpu.roll(cu, 2, axis=0) + w1 * pltpu.roll(cu, 1, axis=0) + w2 * cu
    gate = dot(h, wb_ref[...]) * _silu(dot(h, wz_ref[...]))
    y_ref[...] = (gate * conv).astype(y_ref.dtype)

    prev = halo_scr[j]
    top = cu[0:SUBLANES]
    row = lax.broadcasted_iota(jnp.int32, top.shape, 0)
    back1 = jnp.where(row < 1, pltpu.roll(prev, 1, axis=0), pltpu.roll(top, 1, axis=0))
    back2 = jnp.where(row < 2, pltpu.roll(prev, 2, axis=0), pltpu.roll(top, 2, axis=0))
    conv_top = w0 * back2 + w1 * back1 + w2 * top
    y_ref[0:SUBLANES, :] = (gate[0:SUBLANES] * conv_top).astype(y_ref.dtype)
    halo_scr[j] = cu[tm - SUBLANES:tm]


def _conv_in(x, g, shift, scale, w_in, conv_w, tm=1024, tn=256):
    s, d = x.shape
    c = conv_w.shape[1]
    nj = c // tn
    vec = pl.BlockSpec((1, d), lambda i, j: (0, 0))
    w_specs = [pl.BlockSpec((d, tn), lambda i, j, k=k: (0, k * nj + j)) for k in range(4)]
    return pl.pallas_call(
        _conv_in_kernel,
        grid=(s // tm, nj),
        in_specs=[pl.BlockSpec(memory_space=pl.ANY), vec, vec, vec, *w_specs,
                  pl.BlockSpec((CONV_WIDTH, tn), lambda i, j: (0, j))],
        out_specs=pl.BlockSpec((tm, tn), lambda i, j: (i, j)),
        out_shape=jax.ShapeDtypeStruct((s, c), BF16),
        scratch_shapes=[pltpu.VMEM((tm, d), F32), pltpu.SemaphoreType.DMA(()),
                        pltpu.VMEM((tm, d), BF16), pltpu.VMEM((tm, LANES), F32),
                        pltpu.VMEM((nj, SUBLANES, tn), F32)],
        compiler_params=_params("arbitrary", "arbitrary"),
        name="conv_in_proj",
    )(x, g, shift, scale, w_in, w_in, w_in, w_in, conv_w)


def _out_proj_kernel(y_ref, w_ref, x_ref, gate_ref, o_ref):
    acc = jnp.dot(y_ref[...], w_ref[...], preferred_element_type=F32)
    o_ref[...] = x_ref[...] + gate_ref[...] * acc


def _out_proj(y, w, x, gate, tm=1024, tn=1024):
    s, k = y.shape
    d = w.shape[1]
    return pl.pallas_call(
        _out_proj_kernel,
        grid=(s // tm, d // tn),
        in_specs=[pl.BlockSpec((tm, k), lambda i, j: (i, 0)),
                  pl.BlockSpec((k, tn), lambda i, j: (0, j)),
                  pl.BlockSpec((tm, tn), lambda i, j: (i, j)),
                  pl.BlockSpec((1, tn), lambda i, j: (0, j))],
        out_specs=pl.BlockSpec((tm, tn), lambda i, j: (i, j)),
        out_shape=jax.ShapeDtypeStruct((s, d), F32),
        compiler_params=_params("arbitrary", "arbitrary"),
        name="out_proj",
    )(y, w, x, gate)


def _out_proj_norm_kernel(y_ref, w_ref, x_ref, gate_ref, gf_ref, o_ref):
    acc = jnp.dot(y_ref[...], w_ref[...], preferred_element_type=F32)
    x2 = x_ref[...] + gate_ref[...] * acc
    ms = jnp.mean(x2 * x2, axis=-1, keepdims=True)
    o_ref[...] = (x2 * lax.rsqrt(ms + EPS)) * gf_ref[...]


def _out_proj_norm(y, w, x, gate, g_final, tm=256):
    s, k = y.shape
    d = w.shape[1]
    vec = pl.BlockSpec((1, d), lambda i: (0, 0))
    return pl.pallas_call(
        _out_proj_norm_kernel,
        grid=(s // tm,),
        in_specs=[pl.BlockSpec((tm, k), lambda i: (i, 0)),
                  pl.BlockSpec((k, d), lambda i: (0, 0), pipeline_mode=pl.Buffered(1)),
                  pl.BlockSpec((tm, d), lambda i: (i, 0)), vec, vec],
        out_specs=pl.BlockSpec((tm, d), lambda i: (i, 0)),
        out_shape=jax.ShapeDtypeStruct((s, d), F32),
        compiler_params=_params("arbitrary", vmem=60 * MIB),
        name="out_proj_final_norm",
    )(y, w, x, gate, g_final)


def _norm_proj_kernel(x_hbm, g_ref, sh_ref, sc_ref, w_ref, o_ref, x_buf, x_sem, h_scr, rstd_scr,
                      *, tail_tiles, scaled_tiles, scale):
    i = pl.program_id(0)
    j = pl.program_id(1)

    @pl.when(j == 0)
    def _():
        _await_row_tile(x_hbm, x_buf, x_sem, i)
        _rms_stats(x_buf, rstd_scr)
        _modnorm_apply(x_buf, rstd_scr, g_ref, sh_ref, sc_ref, 1 if tail_tiles else 0, h_scr)
        if not tail_tiles:
            _prefetch_next_row_tile(x_hbm, x_buf, x_sem, i)

    if tail_tiles:
        @pl.when(j == tail_tiles)
        def _():
            _modnorm_apply(x_buf, rstd_scr, g_ref, sh_ref, sc_ref, 0, h_scr)
            _prefetch_next_row_tile(x_hbm, x_buf, x_sem, i)

    acc = jnp.dot(h_scr[...], w_ref[...], preferred_element_type=F32)
    if scaled_tiles:
        scaled = (j >= tail_tiles) & (j < tail_tiles + scaled_tiles)
        acc = acc * jnp.where(scaled, scale, 1.0)
    o_ref[...] = acc.astype(o_ref.dtype)


def _norm_proj(x, g, shift, scale, w, tail_cols=0, scaled_cols=0, col_scale=1.0,
               tm=1024, tn=1024):
    s, d = x.shape
    n = w.shape[1]
    n_mod = g.shape[0]
    assert scaled_cols % tn == 0 and tail_cols % tn == 0 and n_mod == (2 if tail_cols else 1)
    n_tiles, tail_tiles = n // tn, tail_cols // tn
    col_tile = lambda j: (j + n_tiles - tail_tiles) % n_tiles
    vec = pl.BlockSpec((n_mod, d), lambda i, j: (0, 0))
    body = functools.partial(_norm_proj_kernel, tail_tiles=tail_tiles,
                             scaled_tiles=scaled_cols // tn, scale=col_scale)
    return pl.pallas_call(
        body,
        grid=(s // tm, n_tiles),
        in_specs=[pl.BlockSpec(memory_space=pl.ANY), vec, vec, vec,
                  pl.BlockSpec((d, tn), lambda i, j: (0, col_tile(j)))],
        out_specs=pl.BlockSpec((tm, tn), lambda i, j: (i, col_tile(j))),
        out_shape=jax.ShapeDtypeStruct((s, n), BF16),
        scratch_shapes=[pltpu.VMEM((tm, d), F32), pltpu.SemaphoreType.DMA(()),
                        pltpu.VMEM((tm, d), BF16), pltpu.VMEM((tm, LANES), F32)],
        compiler_params=_params("arbitrary", "arbitrary"),
        name="norm_proj",
    )(x, g, shift, scale, w)


def _t5_bucket(dist):
    max_exact = N_BUCKETS // 2
    d = jnp.maximum(dist, 0)
    d_f = jnp.maximum(d, 1).astype(F32)
    large = max_exact + (jnp.log(d_f / max_exact) / math.log(MAX_DISTANCE / max_exact)
                         * (N_BUCKETS - max_exact)).astype(jnp.int32)
    large = jnp.minimum(large, N_BUCKETS - 1)
    return jnp.where(d < max_exact, d, large)


def _band_buckets():
    b = jnp.arange(2 * BLOCK)[:, None]
    a = jnp.arange(BLOCK)[None, :]
    dist = a + BLOCK - b
    band = (dist >= 0) & (dist < BLOCK)
    later = jnp.where(band, _t5_bucket(dist), -1)
    first = jnp.where(b >= BLOCK, later, -1)
    return jnp.stack([first, later]).astype(jnp.int32)


def _bias_table_kernel(rb_ref, bucket_ref, o_ref):
    g = pl.program_id(1)
    bucket = bucket_ref[0]
    for hh in range(GROUP):
        acc = jnp.full(bucket.shape, -jnp.inf, F32)
        for k in range(N_BUCKETS):
            acc = jnp.where(bucket == k, rb_ref[k, g * GROUP + hh], acc)
        o_ref[0, 0, :, hh * BLOCK:(hh + 1) * BLOCK] = acc


def _bias_table(rel_bias):
    n_kv = rel_bias.shape[1] // GROUP
    return pl.pallas_call(
        _bias_table_kernel,
        grid=(2, n_kv),
        in_specs=[pl.BlockSpec(memory_space=pltpu.SMEM),
                  pl.BlockSpec((1, 2 * BLOCK, BLOCK), lambda v, g: (v, 0, 0))],
        out_specs=pl.BlockSpec((1, 1, 2 * BLOCK, GROUP * BLOCK), lambda v, g: (v, g, 0, 0)),
        out_shape=jax.ShapeDtypeStruct((2, n_kv, 2 * BLOCK, GROUP * BLOCK), F32),
        compiler_params=_params("arbitrary", "arbitrary"),
        name="bias_table",
    )(rel_bias, _band_buckets())


def _attn_kernel(sink_ref, q_ref, z_ref, kp_ref, kc_ref, vp_ref, vc_ref, bias_ref, o_ref):
    n_kv = bias_ref.shape[1]
    zero_half = jnp.zeros((HEAD_DIM, GROUP * BLOCK), BF16)
    for g in range(n_kv):
        heads = [g * GROUP + hh for hh in range(GROUP)]
        kv_cols = slice((g // 2) * LANES, (g // 2 + 1) * LANES)
        k2 = jnp.concatenate([kp_ref[:, kv_cols], kc_ref[:, kv_cols]], axis=0)
        v2_t = jnp.concatenate([vp_ref[:, kv_cols].T, vc_ref[:, kv_cols].T], axis=1)
        q_t = []
        for h in heads[::2]:
            both = q_ref[:, (h // 2) * LANES:(h // 2 + 1) * LANES].T
            q_t += [both[:HEAD_DIM], both[HEAD_DIM:]]
        qg_t = jnp.concatenate(q_t, axis=1)
        rhs = jnp.concatenate([qg_t, zero_half] if g % 2 == 0 else [zero_half, qg_t], axis=0)
        s_t = jnp.dot(k2, rhs, preferred_element_type=F32) + bias_ref[0, g]
        sink = jnp.concatenate([jnp.full((1, BLOCK), sink_ref[h], F32) for h in heads], axis=1)
        m = jnp.maximum(jnp.max(s_t, axis=0, keepdims=True), sink)
        p = jnp.exp(s_t - m)
        den = jnp.sum(p, axis=0, keepdims=True) + jnp.exp(sink - m)
        o2_t = jnp.dot(v2_t, p.astype(BF16), preferred_element_type=F32)
        o_t = (o2_t[:HEAD_DIM] if g % 2 == 0 else o2_t[HEAD_DIM:]) / den
        for hh in range(0, GROUP, 2):
            cols = slice((heads[hh] // 2) * LANES, (heads[hh] // 2 + 1) * LANES)
            both = jnp.concatenate([o_t[:, hh * BLOCK:(hh + 1) * BLOCK],
                                    o_t[:, (hh + 1) * BLOCK:(hh + 2) * BLOCK]], axis=0)
            o_ref[:, cols] = (both.T * _silu(z_ref[:, cols].astype(F32))).astype(o_ref.dtype)


def _attention(qz, q_blk, z_blk, kv, k_blk, v_blk, bias, sinks):
    s = qz.shape[0]
    n_kv = bias.shape[1]
    kv_dim = n_kv * HEAD_DIM
    attn_dim = kv_dim * GROUP
    assert n_kv % 2 == 0 and GROUP % 2 == 0 and 2 * HEAD_DIM == LANES and BLOCK == LANES
    prev = lambda i: jnp.maximum(i - 1, 0)
    return pl.pallas_call(
        _attn_kernel,
        grid=(s // BLOCK,),
        in_specs=[pl.BlockSpec(memory_space=pltpu.SMEM),
                  pl.BlockSpec((BLOCK, attn_dim), lambda i: (i, q_blk)),
                  pl.BlockSpec((BLOCK, attn_dim), lambda i: (i, z_blk)),
                  pl.BlockSpec((BLOCK, kv_dim), lambda i: (prev(i), k_blk)),
                  pl.BlockSpec((BLOCK, kv_dim), lambda i: (i, k_blk)),
                  pl.BlockSpec((BLOCK, kv_dim), lambda i: (prev(i), v_blk)),
                  pl.BlockSpec((BLOCK, kv_dim), lambda i: (i, v_blk)),
                  pl.BlockSpec((1, n_kv, 2 * BLOCK, GROUP * BLOCK),
                               lambda i: (jnp.minimum(i, 1), 0, 0, 0))],
        out_specs=pl.BlockSpec((BLOCK, attn_dim), lambda i: (i, 0)),
        out_shape=jax.ShapeDtypeStruct((s, attn_dim), BF16),
        compiler_params=_params("arbitrary"),
        name="swa_attention",
    )(sinks, qz, qz, kv, kv, kv, kv, bias)


def _trunk(x, c, g_a, w_mod_a, b_mod_a, w_a_in, conv_a, w_a_out, g_kv, w_mod_kv, b_mod_kv,
           w_kv, rel_bias, g_b, w_mod_b, b_mod_b, w_b_in, sinks_b, w_b_out, g_final):
    s, d = x.shape
    n_a, n_b = w_a_in.shape[0], w_b_in.shape[0]
    assert n_b >= 1 and s % BLOCK == 0
    row = lambda v: v.reshape(1, -1)
    cb = jnp.broadcast_to(c.reshape(d, 1), (d, LANES))

    for i in range(n_a):
        shift, scale, gate = jnp.split(_adaln(cb, w_mod_a[i], b_mod_a[i]), 3, axis=-1)
        y = _conv_in(x, row(g_a[i]), shift, scale, w_a_in[i].astype(BF16), conv_a[i])
        x = _out_proj(y, w_a_out[i].astype(BF16), x, gate)

    shift_kv, scale_kv = jnp.split(_adaln(cb, w_mod_kv, b_mod_kv), 2, axis=-1)
    bias = _bias_table(rel_bias)
    attn_dim = w_b_in.shape[2] // 2
    kv_dim = w_kv.shape[1] // 2
    kv = None
    for i in range(n_b):
        shift, scale, gate = jnp.split(_adaln(cb, w_mod_b[i], b_mod_b[i]), 3, axis=-1)
        q_scale = dict(scaled_cols=attn_dim, col_scale=HEAD_DIM ** -0.5)
        if i == 0:
            w_cat = jnp.concatenate([w_b_in[i], w_kv], axis=1).astype(BF16)
            qz = _norm_proj(x, jnp.stack([g_b[i], g_kv]), jnp.concatenate([shift, shift_kv]),
                            jnp.concatenate([scale, scale_kv]), w_cat,
                            tail_cols=2 * kv_dim, **q_scale)
            kv, k_blk = qz, 2 * attn_dim // kv_dim
        else:
            qz = _norm_proj(x, row(g_b[i]), shift, scale, w_b_in[i].astype(BF16), **q_scale)
        a = _attention(qz, 0, 1, kv, k_blk, k_blk + 1, bias, sinks_b[i])
        w_out = w_b_out[i].astype(BF16)
        if i + 1 < n_b:
            x = _out_proj(a, w_out, x, gate)
        else:
            x = _out_proj_norm(a, w_out, x, gate, row(g_final))
    return x


def kernel(x, c, g_a, w_mod_a, b_mod_a, w_a_in, conv_a, w_a_out, g_kv, w_mod_kv, b_mod_kv, w_kv,
           rel_bias, g_b, w_mod_b, b_mod_b, w_b_in, sinks_b, w_b_out, g_final):
    outs = [_trunk(x[b], c[b], g_a, w_mod_a, b_mod_a, w_a_in, conv_a, w_a_out, g_kv, w_mod_kv,
                   b_mod_kv, w_kv, rel_bias, g_b, w_mod_b, b_mod_b, w_b_in, sinks_b, w_b_out,
                   g_final) for b in range(x.shape[0])]
    return jnp.stack(outs, axis=0)
```

```python
import functools
import math

import jax
import jax.numpy as jnp
from jax import lax
from jax.experimental import pallas as pl
from jax.experimental.pallas import tpu as pltpu

HEAD_DIM = 64
GROUP = 8
BLOCK = 128
N_BUCKETS = 32
MAX_DISTANCE = 128
CONV_WIDTH = 3
EPS = 1e-6

LANES = 128
SUBLANES = 8
MIB = 1024 * 1024
VMEM_LIMIT = 56 * MIB

F32 = jnp.float32
BF16 = jnp.bfloat16
NORM_ROWS = 128


def _params(*semantics, vmem=VMEM_LIMIT, flags=None):
    return pltpu.CompilerParams(dimension_semantics=semantics, vmem_limit_bytes=vmem, flags=flags)


def _silu(v):
    return v * jax.nn.sigmoid(v)


def _adaln_kernel(cb_ref, w_ref, b_ref, o_ref):
    ca = _silu(cb_ref[...])
    for g in range(o_ref.shape[1] // LANES):
        cols = slice(g * LANES, (g + 1) * LANES)
        o_ref[:, cols] = jnp.sum(w_ref[:, cols] * ca, axis=0, keepdims=True) + b_ref[:, cols]


def _adaln(cb, w, b, tn=1024):
    k, n = w.shape
    return pl.pallas_call(
        _adaln_kernel,
        grid=(n // tn,),
        in_specs=[
            pl.BlockSpec((k, LANES), lambda j: (0, 0)),
            pl.BlockSpec((k, tn), lambda j: (0, j)),
            pl.BlockSpec((1, tn), lambda j: (0, j)),
        ],
        out_specs=pl.BlockSpec((1, tn), lambda j: (0, j)),
        out_shape=jax.ShapeDtypeStruct((1, n), F32),
        compiler_params=_params("arbitrary"),
        name="adaln_matvec",
    )(cb, w, b.reshape(1, n))


def _row_chunk(r):
    return pl.ds(pl.multiple_of(r * NORM_ROWS, NORM_ROWS), NORM_ROWS)


def _col_groups(d):
    return [slice(k * LANES, (k + 1) * LANES) for k in range(d // LANES)]


def _rms_stats(x_ref, rstd_scr):
    tm, d = x_ref.shape

    def body(r, carry):
        rows = _row_chunk(r)
        acc = jnp.zeros((NORM_ROWS, LANES), F32)
        for cols in _col_groups(d):
            xk = x_ref[rows, cols]
            acc = acc + xk * xk
        ms = jnp.sum(acc, axis=-1, keepdims=True) * (1.0 / d)
        rstd_scr[rows, :] = jnp.broadcast_to(lax.rsqrt(ms + EPS), (NORM_ROWS, LANES))
        return carry

    lax.fori_loop(0, tm // NORM_ROWS, body, 0)


def _modnorm_apply(x_ref, rstd_scr, g_ref, sh_ref, sc_ref, mod, h_scr):
    tm, d = x_ref.shape
    vec = slice(mod, mod + 1)

    def body(r, carry):
        rows = _row_chunk(r)
        rstd = rstd_scr[rows, :]
        for cols in _col_groups(d):
            gain = g_ref[vec, cols] * (1.0 + sc_ref[vec, cols])
            h_scr[rows, cols] = ((x_ref[rows, cols] * rstd) * gain
                                 + sh_ref[vec, cols]).astype(h_scr.dtype)
        return carry

    lax.fori_loop(0, tm // NORM_ROWS, body, 0)


def _row_tile_copy(x_hbm, x_buf, sem, i):
    tm = x_buf.shape[0]
    return pltpu.make_async_copy(x_hbm.at[pl.ds(i * tm, tm), :], x_buf, sem)


def _await_row_tile(x_hbm, x_buf, sem, i):
    @pl.when(i == 0)
    def _():
        _row_tile_copy(x_hbm, x_buf, sem, 0).start()

    _row_tile_copy(x_hbm, x_buf, sem, i).wait()


def _prefetch_next_row_tile(x_hbm, x_buf, sem, i):
    @pl.when(i + 1 < pl.num_programs(0))
    def _():
        _row_tile_copy(x_hbm, x_buf, sem, i + 1).start()


SIDE_COLS = 2048
ROW_SPLIT = 2


def _cast_cols(in_ref, o_ref, c0, o0):
    n = min(SIDE_COLS, in_ref.shape[1] - c0)
    o_ref[:, o0:o0 + n] = in_ref[:, c0:c0 + n].astype(o_ref.dtype)


def _matvec_cols(cb_ref, w_ref, acc_ref, c0):
    ca = _silu(cb_ref[...])
    for k in range(c0 // LANES, min(c0 + SIDE_COLS, w_ref.shape[1]) // LANES):
        cols = slice(k * LANES, (k + 1) * LANES)
        prod = w_ref[:, cols] * ca
        part = prod[0:SUBLANES]
        for q in range(1, prod.shape[0] // SUBLANES):
            part = part + prod[q * SUBLANES:(q + 1) * SUBLANES]
        acc_ref[:, cols] += part


def _conv_in_kernel(*refs, cast_widths, n_matvec):
    refs = list(refs)
    take = lambda n: [refs.pop(0) for _ in range(n)]
    x_hbm, g_ref, sh_ref, sc_ref, wb_ref, wc_ref, wu_ref, wz_ref, cw_ref = take(9)
    cast_in = [take(len(widths)) for widths in cast_widths]
    cb_ref = take(1)[0] if n_matvec else None
    mv_in = [take(2) for _ in range(n_matvec)]
    y_ref = take(1)[0]
    cast_out = take(len(cast_widths))
    mv_out = take(n_matvec)
    x_buf, x_sem, h_scr, rstd_scr, halo_scr = take(5)
    mv_acc = take(n_matvec)

    i = pl.program_id(0)
    j = pl.program_id(1)
    tm = y_ref.shape[0]
    step = i * pl.num_programs(1) + j
    last_step = pl.num_programs(0) * pl.num_programs(1) - 1

    @pl.when(step == 0)
    def _():
        for acc_ref in mv_acc:
            acc_ref[...] = jnp.zeros(acc_ref.shape, F32)

    @pl.when(j == 0)
    def _():
        _await_row_tile(x_hbm, x_buf, x_sem, i)
        _rms_stats(x_buf, rstd_scr)
        _modnorm_apply(x_buf, rstd_scr, g_ref, sh_ref, sc_ref, 0, h_scr)
        _prefetch_next_row_tile(x_hbm, x_buf, x_sem, i)

    @pl.when(i == 0)
    def _():
        halo_scr[j] = jnp.zeros(halo_scr.shape[1:], F32)

    side_jobs = []
    for ins, out in zip(cast_in, cast_out):
        col = 0
        for r in ins:
            for c0 in range(0, r.shape[1], SIDE_COLS):
                side_jobs.append(functools.partial(_cast_cols, r, out, c0, col + c0))
            col += r.shape[1]
    for (w_ref, _), acc_ref in zip(mv_in, mv_acc):
        for c0 in range(0, w_ref.shape[1], SIDE_COLS):
            side_jobs.append(functools.partial(_matvec_cols, cb_ref, w_ref, acc_ref, c0))
    proj = [[None] * ROW_SPLIT for _ in range(4)]
    n_gaps = 4 * ROW_SPLIT - 1
    rows_per = tm // ROW_SPLIT
    for n, (half, k) in enumerate((half, k) for half in range(ROW_SPLIT) for k in range(4)):
        w_ref = (wc_ref, wu_ref, wb_ref, wz_ref)[k]
        rows = slice(half * rows_per, (half + 1) * rows_per)
        proj[k][half] = jnp.dot(h_scr[rows, :], w_ref[...], preferred_element_type=F32)
        if n < n_gaps:
            for job in side_jobs[n::n_gaps]:
                job()
    c_proj, u_proj, b_proj, z_proj = [jnp.concatenate(p, axis=0) for p in proj]

    cu = c_proj * u_proj
    w0 = cw_ref[0:1, :]
    w1 = cw_ref[1:2, :]
    w2 = cw_ref[2:3, :]
    conv = w0 * pltpu.roll(cu, 2, axis=0) + w1 * pltpu.roll(cu, 1, axis=0) + w2 * cu
    gate = b_proj * _silu(z_proj)
    y_ref[...] = (gate * conv).astype(y_ref.dtype)

    prev = halo_scr[j]
    top = cu[0:SUBLANES]
    row = lax.broadcasted_iota(jnp.int32, top.shape, 0)
    back1 = jnp.where(row < 1, pltpu.roll(prev, 1, axis=0), pltpu.roll(top, 1, axis=0))
    back2 = jnp.where(row < 2, pltpu.roll(prev, 2, axis=0), pltpu.roll(top, 2, axis=0))
    conv_top = w0 * back2 + w1 * back1 + w2 * top
    y_ref[0:SUBLANES, :] = (gate[0:SUBLANES] * conv_top).astype(y_ref.dtype)
    halo_scr[j] = cu[tm - SUBLANES:tm]

    if n_matvec:
        @pl.when(step == last_step)
        def _():
            for (_, b_ref), o_ref, acc_ref in zip(mv_in, mv_out, mv_acc):
                o_ref[...] = jnp.sum(acc_ref[...], axis=0, keepdims=True) + b_ref[...]


CONV_TM, CONV_TN = 1024, 256
BF16_ROWS = 16


def _side_rows(s, c, k):
    steps = (s // CONV_TM) * (c // CONV_TN)
    rows = k // steps
    return rows if rows * steps == k and rows % BF16_ROWS == 0 else 0


def _conv_in(x, g, shift, scale, w_in, conv_w, cast_jobs=(), matvec_jobs=(), cb=None):
    tm, tn = CONV_TM, CONV_TN
    s, d = x.shape
    c = conv_w.shape[1]
    nj = c // tn
    vec = pl.BlockSpec((1, d), lambda i, j: (0, 0))
    w_specs = [pl.BlockSpec((d, tn), lambda i, j, k=k: (0, k * nj + j)) for k in range(4)]
    row_block = lambda rows, n: pl.BlockSpec((rows, n), lambda i, j: (i * nj + j, 0))
    whole = lambda n: pl.BlockSpec((1, n), lambda i, j: (0, 0))

    side_in, side_in_specs = [], []
    out_shapes = [jax.ShapeDtypeStruct((s, c), BF16)]
    out_specs = [pl.BlockSpec((tm, tn), lambda i, j: (i, j))]
    scratch = [pltpu.VMEM((tm, d), F32), pltpu.SemaphoreType.DMA(()),
               pltpu.VMEM((tm, d), BF16), pltpu.VMEM((tm, LANES), F32),
               pltpu.VMEM((nj, SUBLANES, tn), F32)]
    for job in cast_jobs:
        k = job[0].shape[0]
        rows = _side_rows(s, c, k)
        assert rows and all(a.shape[0] == k and a.shape[1] % LANES == 0 for a in job)
        side_in += list(job)
        side_in_specs += [row_block(rows, a.shape[1]) for a in job]
        n = sum(a.shape[1] for a in job)
        out_shapes.append(jax.ShapeDtypeStruct((k, n), BF16))
        out_specs.append(row_block(rows, n))
    if matvec_jobs:
        k = cb.shape[0]
        rows = _side_rows(s, c, k)
        assert rows
        side_in.append(cb)
        side_in_specs.append(row_block(rows, LANES))
    for w, b in matvec_jobs:
        n = w.shape[1]
        assert w.shape[0] == k and n % LANES == 0
        side_in += [w, b.reshape(1, n)]
        side_in_specs += [row_block(rows, n), whole(n)]
        out_shapes.append(jax.ShapeDtypeStruct((1, n), F32))
        out_specs.append(whole(n))
        scratch.append(pltpu.VMEM((SUBLANES, n), F32))

    body = functools.partial(_conv_in_kernel, cast_widths=tuple(tuple(a.shape[1] for a in job)
                                                                for job in cast_jobs),
                             n_matvec=len(matvec_jobs))
    outs = pl.pallas_call(
        body,
        grid=(s // tm, nj),
        in_specs=[pl.BlockSpec(memory_space=pl.ANY), vec, vec, vec, *w_specs,
                  pl.BlockSpec((CONV_WIDTH, tn), lambda i, j: (0, j)), *side_in_specs],
        out_specs=out_specs,
        out_shape=out_shapes,
        scratch_shapes=scratch,
        compiler_params=_params("arbitrary", "arbitrary"),
        name="conv_in_proj",
    )(x, g, shift, scale, w_in, w_in, w_in, w_in, conv_w, *side_in)
    n_cast = len(cast_jobs)
    return outs[0], list(outs[1:1 + n_cast]), list(outs[1 + n_cast:])


def _out_proj_kernel(y_ref, w_ref, x_ref, gate_ref, o_ref):
    acc = jnp.dot(y_ref[...], w_ref[...], preferred_element_type=F32)
    o_ref[...] = x_ref[...] + gate_ref[...] * acc


def _out_proj(y, w, x, gate, tm=1024, tn=1024):
    s, k = y.shape
    d = w.shape[1]
    return pl.pallas_call(
        _out_proj_kernel,
        grid=(s // tm, d // tn),
        in_specs=[pl.BlockSpec((tm, k), lambda i, j: (i, 0)),
                  pl.BlockSpec((k, tn), lambda i, j: (0, j)),
                  pl.BlockSpec((tm, tn), lambda i, j: (i, j)),
                  pl.BlockSpec((1, tn), lambda i, j: (0, j))],
        out_specs=pl.BlockSpec((tm, tn), lambda i, j: (i, j)),
        out_shape=jax.ShapeDtypeStruct((s, d), F32),
        compiler_params=_params("arbitrary", "arbitrary"),
        name="out_proj",
    )(y, w, x, gate)


def _out_proj_norm_kernel(y_ref, w_ref, x_ref, gate_ref, gf_ref, o_ref):
    acc = jnp.dot(y_ref[...], w_ref[...], preferred_element_type=F32)
    x2 = x_ref[...] + gate_ref[...] * acc
    ms = jnp.mean(x2 * x2, axis=-1, keepdims=True)
    o_ref[...] = (x2 * lax.rsqrt(ms + EPS)) * gf_ref[...]


def _out_proj_norm(y, w, x, gate, g_final, tm=256):
    s, k = y.shape
    d = w.shape[1]
    vec = pl.BlockSpec((1, d), lambda i: (0, 0))
    return pl.pallas_call(
        _out_proj_norm_kernel,
        grid=(s // tm,),
        in_specs=[pl.BlockSpec((tm, k), lambda i: (i, 0)),
                  pl.BlockSpec((k, d), lambda i: (0, 0), pipeline_mode=pl.Buffered(1)),
                  pl.BlockSpec((tm, d), lambda i: (i, 0)), vec, vec],
        out_specs=pl.BlockSpec((tm, d), lambda i: (i, 0)),
        out_shape=jax.ShapeDtypeStruct((s, d), F32),
        compiler_params=_params("arbitrary", vmem=60 * MIB),
        name="out_proj_final_norm",
    )(y, w, x, gate, g_final)


def _norm_proj_kernel(x_hbm, g_ref, sh_ref, sc_ref, w_ref, o_ref, x_buf, x_sem, h_scr, rstd_scr,
                      *, tail_tiles, scaled_tiles, scale):
    i = pl.program_id(0)
    j = pl.program_id(1)

    @pl.when(j == 0)
    def _():
        _await_row_tile(x_hbm, x_buf, x_sem, i)
        _rms_stats(x_buf, rstd_scr)
        _modnorm_apply(x_buf, rstd_scr, g_ref, sh_ref, sc_ref, 1 if tail_tiles else 0, h_scr)
        if not tail_tiles:
            _prefetch_next_row_tile(x_hbm, x_buf, x_sem, i)

    if tail_tiles:
        @pl.when(j == tail_tiles)
        def _():
            _modnorm_apply(x_buf, rstd_scr, g_ref, sh_ref, sc_ref, 0, h_scr)
            _prefetch_next_row_tile(x_hbm, x_buf, x_sem, i)

    acc = jnp.dot(h_scr[...], w_ref[...], preferred_element_type=F32)
    if scaled_tiles:
        scaled = (j >= tail_tiles) & (j < tail_tiles + scaled_tiles)
        acc = acc * jnp.where(scaled, scale, 1.0)
    o_ref[...] = acc.astype(o_ref.dtype)


def _norm_proj(x, g, shift, scale, w, tail_cols=0, scaled_cols=0, col_scale=1.0,
               tm=1024, tn=1024):
    s, d = x.shape
    n = w.shape[1]
    n_mod = g.shape[0]
    assert scaled_cols % tn == 0 and tail_cols % tn == 0 and n_mod == (2 if tail_cols else 1)
    n_tiles, tail_tiles = n // tn, tail_cols // tn
    col_tile = lambda j: (j + n_tiles - tail_tiles) % n_tiles
    vec = pl.BlockSpec((n_mod, d), lambda i, j: (0, 0))
    body = functools.partial(_norm_proj_kernel, tail_tiles=tail_tiles,
                             scaled_tiles=scaled_cols // tn, scale=col_scale)
    return pl.pallas_call(
        body,
        grid=(s // tm, n_tiles),
        in_specs=[pl.BlockSpec(memory_space=pl.ANY), vec, vec, vec,
                  pl.BlockSpec((d, tn), lambda i, j: (0, col_tile(j)))],
        out_specs=pl.BlockSpec((tm, tn), lambda i, j: (i, col_tile(j))),
        out_shape=jax.ShapeDtypeStruct((s, n), BF16),
        scratch_shapes=[pltpu.VMEM((tm, d), F32), pltpu.SemaphoreType.DMA(()),
                        pltpu.VMEM((tm, d), BF16), pltpu.VMEM((tm, LANES), F32)],
        compiler_params=_params("arbitrary", "arbitrary"),
        name="norm_proj",
    )(x, g, shift, scale, w)


def _t5_bucket(dist):
    max_exact = N_BUCKETS // 2
    d = jnp.maximum(dist, 0)
    d_f = jnp.maximum(d, 1).astype(F32)
    large = max_exact + (jnp.log(d_f / max_exact) / math.log(MAX_DISTANCE / max_exact)
                         * (N_BUCKETS - max_exact)).astype(jnp.int32)
    large = jnp.minimum(large, N_BUCKETS - 1)
    return jnp.where(d < max_exact, d, large)


def _band_buckets():
    b = jnp.arange(2 * BLOCK)[:, None]
    a = jnp.arange(BLOCK)[None, :]
    dist = a + BLOCK - b
    band = (dist >= 0) & (dist < BLOCK)
    later = jnp.where(band, _t5_bucket(dist), -1)
    first = jnp.where(b >= BLOCK, later, -1)
    return jnp.stack([first, later]).astype(jnp.int32)


def _bias_table_kernel(rb_ref, bucket_ref, o_ref):
    g = pl.program_id(1)
    bucket = bucket_ref[0]
    for hh in range(GROUP):
        acc = jnp.full(bucket.shape, -jnp.inf, F32)
        for k in range(N_BUCKETS):
            acc = jnp.where(bucket == k, rb_ref[k, g * GROUP + hh], acc)
        o_ref[0, 0, :, hh * BLOCK:(hh + 1) * BLOCK] = acc


def _bias_table(rel_bias):
    n_kv = rel_bias.shape[1] // GROUP
    return pl.pallas_call(
        _bias_table_kernel,
        grid=(2, n_kv),
        in_specs=[pl.BlockSpec(memory_space=pltpu.SMEM),
                  pl.BlockSpec((1, 2 * BLOCK, BLOCK), lambda v, g: (v, 0, 0))],
        out_specs=pl.BlockSpec((1, 1, 2 * BLOCK, GROUP * BLOCK), lambda v, g: (v, g, 0, 0)),
        out_shape=jax.ShapeDtypeStruct((2, n_kv, 2 * BLOCK, GROUP * BLOCK), F32),
        compiler_params=_params("arbitrary", "arbitrary"),
        name="bias_table",
    )(rel_bias, _band_buckets())


def _attn_kernel(sink_ref, q_ref, z_ref, kp_ref, kc_ref, vp_ref, vc_ref, bias_ref, o_ref):
    n_kv = bias_ref.shape[1]
    zero_half = jnp.zeros((HEAD_DIM, GROUP * BLOCK), BF16)
    for g in range(n_kv):
        heads = [g * GROUP + hh for hh in range(GROUP)]
        kv_cols = slice((g // 2) * LANES, (g // 2 + 1) * LANES)
        k2 = jnp.concatenate([kp_ref[:, kv_cols], kc_ref[:, kv_cols]], axis=0)
        v2_t = jnp.concatenate([vp_ref[:, kv_cols].T, vc_ref[:, kv_cols].T], axis=1)
        q_t = []
        for h in heads[::2]:
            both = q_ref[:, (h // 2) * LANES:(h // 2 + 1) * LANES].T
            q_t += [both[:HEAD_DIM], both[HEAD_DIM:]]
        qg_t = jnp.concatenate(q_t, axis=1)
        rhs = jnp.concatenate([qg_t, zero_half] if g % 2 == 0 else [zero_half, qg_t], axis=0)
        s_t = jnp.dot(k2, rhs, preferred_element_type=F32) + bias_ref[0, g]
        sink = jnp.concatenate([jnp.full((1, BLOCK), sink_ref[h], F32) for h in heads], axis=1)
        m = jnp.maximum(jnp.max(s_t, axis=0, keepdims=True), sink)
        p = jnp.exp(s_t - m)
        den = jnp.sum(p, axis=0, keepdims=True) + jnp.exp(sink - m)
        o2_t = jnp.dot(v2_t, p.astype(BF16), preferred_element_type=F32)
        o_t = (o2_t[:HEAD_DIM] if g % 2 == 0 else o2_t[HEAD_DIM:]) / den
        for hh in range(0, GROUP, 2):
            cols = slice((heads[hh] // 2) * LANES, (heads[hh] // 2 + 1) * LANES)
            both = jnp.concatenate([o_t[:, hh * BLOCK:(hh + 1) * BLOCK],
                                    o_t[:, (hh + 1) * BLOCK:(hh + 2) * BLOCK]], axis=0)
            o_ref[:, cols] = (both.T * _silu(z_ref[:, cols].astype(F32))).astype(o_ref.dtype)


def _attention(qz, q_blk, z_blk, kv, k_blk, v_blk, bias, sinks):
    s = qz.shape[0]
    n_kv = bias.shape[1]
    kv_dim = n_kv * HEAD_DIM
    attn_dim = kv_dim * GROUP
    assert n_kv % 2 == 0 and GROUP % 2 == 0 and 2 * HEAD_DIM == LANES and BLOCK == LANES
    prev = lambda i: jnp.maximum(i - 1, 0)
    return pl.pallas_call(
        _attn_kernel,
        grid=(s // BLOCK,),
        in_specs=[pl.BlockSpec(memory_space=pltpu.SMEM),
                  pl.BlockSpec((BLOCK, attn_dim), lambda i: (i, q_blk)),
                  pl.BlockSpec((BLOCK, attn_dim), lambda i: (i, z_blk)),
                  pl.BlockSpec((BLOCK, kv_dim), lambda i: (prev(i), k_blk)),
                  pl.BlockSpec((BLOCK, kv_dim), lambda i: (i, k_blk)),
                  pl.BlockSpec((BLOCK, kv_dim), lambda i: (prev(i), v_blk)),
                  pl.BlockSpec((BLOCK, kv_dim), lambda i: (i, v_blk)),
                  pl.BlockSpec((1, n_kv, 2 * BLOCK, GROUP * BLOCK),
                               lambda i: (jnp.minimum(i, 1), 0, 0, 0))],
        out_specs=pl.BlockSpec((BLOCK, attn_dim), lambda i: (i, 0)),
        out_shape=jax.ShapeDtypeStruct((s, attn_dim), BF16),
        compiler_params=_params("arbitrary"),
        name="swa_attention",
    )(sinks, qz, qz, kv, kv, kv, kv, bias)


def _trunk(x, c, g_a, w_mod_a, b_mod_a, w_a_in, conv_a, w_a_out, g_kv, w_mod_kv, b_mod_kv,
           w_kv, rel_bias, g_b, w_mod_b, b_mod_b, w_b_in, sinks_b, w_b_out, g_final):
    s, d = x.shape
    n_a, n_b = w_a_in.shape[0], w_b_in.shape[0]
    assert n_b >= 1 and s % BLOCK == 0
    row = lambda v: v.reshape(1, -1)
    cb = jnp.broadcast_to(c.reshape(d, 1), (d, LANES))

    side_operands = (w_a_out[-1], w_b_in[0], w_kv, w_b_out[0], w_mod_kv, w_mod_b[0]) if n_a else ()
    stream = n_a > 0 and all(_side_rows(s, conv_a.shape[2], a.shape[0]) for a in side_operands)
    w_a_out_bf = w_cat = w_b_out_bf = mod_kv = mod_b = None
    for i in range(n_a):
        shift, scale, gate = jnp.split(_adaln(cb, w_mod_a[i], b_mod_a[i]), 3, axis=-1)
        args = (x, row(g_a[i]), shift, scale, w_a_in[i].astype(BF16), conv_a[i])
        if stream and i == n_a - 1:
            y, (w_a_out_bf, w_cat, w_b_out_bf), (mod_kv, mod_b) = _conv_in(
                *args, cast_jobs=((w_a_out[i],), (w_b_in[0], w_kv), (w_b_out[0],)),
                matvec_jobs=((w_mod_kv, b_mod_kv), (w_mod_b[0], b_mod_b[0])), cb=cb)
        else:
            y, _, _ = _conv_in(*args)
            w_a_out_bf = w_a_out[i].astype(BF16)
        x = _out_proj(y, w_a_out_bf, x, gate)

    if mod_kv is None:
        mod_kv = _adaln(cb, w_mod_kv, b_mod_kv)
    shift_kv, scale_kv = jnp.split(mod_kv, 2, axis=-1)
    bias = _bias_table(rel_bias)
    attn_dim = w_b_in.shape[2] // 2
    kv_dim = w_kv.shape[1] // 2
    kv = None
    for i in range(n_b):
        if i > 0 or mod_b is None:
            mod_b = _adaln(cb, w_mod_b[i], b_mod_b[i])
        shift, scale, gate = jnp.split(mod_b, 3, axis=-1)
        q_scale = dict(scaled_cols=attn_dim, col_scale=HEAD_DIM ** -0.5)
        if i == 0:
            if w_cat is None:
                w_cat = jnp.concatenate([w_b_in[i], w_kv], axis=1).astype(BF16)
            qz = _norm_proj(x, jnp.stack([g_b[i], g_kv]), jnp.concatenate([shift, shift_kv]),
                            jnp.concatenate([scale, scale_kv]), w_cat,
                            tail_cols=2 * kv_dim, **q_scale)
            kv, k_blk = qz, 2 * attn_dim // kv_dim
        else:
            qz = _norm_proj(x, row(g_b[i]), shift, scale, w_b_in[i].astype(BF16), **q_scale)
        a = _attention(qz, 0, 1, kv, k_blk, k_blk + 1, bias, sinks_b[i])
        w_out = w_b_out_bf if i == 0 and w_b_out_bf is not None else w_b_out[i].astype(BF16)
        if i + 1 < n_b:
            x = _out_proj(a, w_out, x, gate)
        else:
            x = _out_proj_norm(a, w_out, x, gate, row(g_final))
    return x


def kernel(x, c, g_a, w_mod_a, b_mod_a, w_a_in, conv_a, w_a_out, g_kv, w_mod_kv, b_mod_kv, w_kv,
           rel_bias, g_b, w_mod_b, b_mod_b, w_b_in, sinks_b, w_b_out, g_final):
    outs = [_trunk(x[b], c[b], g_a, w_mod_a, b_mod_a, w_a_in, conv_a, w_a_out, g_kv, w_mod_kv,
                   b_mod_kv, w_kv, rel_bias, g_b, w_mod_b, b_mod_b, w_b_in, sinks_b, w_b_out,
                   g_final) for b in range(x.shape[0])]
    return jnp.stack(outs, axis=0)
```

```python
import functools
import math

import jax
import jax.numpy as jnp
from jax import lax
from jax.experimental import pallas as pl
from jax.experimental.pallas import tpu as pltpu

HEAD_DIM = 64
GROUP = 8
BLOCK = 128
N_BUCKETS = 32
MAX_DISTANCE = 128
CONV_WIDTH = 3
EPS = 1e-6
LOG2E = math.log2(math.e)

LANES = 128
SUBLANES = 8
MIB = 1024 * 1024
VMEM_LIMIT = 56 * MIB

F32 = jnp.float32
BF16 = jnp.bfloat16
NORM_ROWS = 128


def _params(*semantics, vmem=VMEM_LIMIT, flags=None):
    return pltpu.CompilerParams(dimension_semantics=semantics, vmem_limit_bytes=vmem, flags=flags)


def _silu(v):
    return v * jax.nn.sigmoid(v)


def _adaln_kernel(cb_ref, w_ref, b_ref, o_ref):
    ca = _silu(cb_ref[...])
    for g in range(o_ref.shape[1] // LANES):
        cols = slice(g * LANES, (g + 1) * LANES)
        o_ref[:, cols] = jnp.sum(w_ref[:, cols] * ca, axis=0, keepdims=True) + b_ref[:, cols]


def _adaln(cb, w, b, tn=1024):
    k, n = w.shape
    return pl.pallas_call(
        _adaln_kernel,
        grid=(n // tn,),
        in_specs=[
            pl.BlockSpec((k, LANES), lambda j: (0, 0)),
            pl.BlockSpec((k, tn), lambda j: (0, j)),
            pl.BlockSpec((1, tn), lambda j: (0, j)),
        ],
        out_specs=pl.BlockSpec((1, tn), lambda j: (0, j)),
        out_shape=jax.ShapeDtypeStruct((1, n), F32),
        compiler_params=_params("arbitrary"),
        name="adaln_matvec",
    )(cb, w, b.reshape(1, n))


def _row_chunk(r):
    return pl.ds(pl.multiple_of(r * NORM_ROWS, NORM_ROWS), NORM_ROWS)


def _col_groups(d):
    return [slice(k * LANES, (k + 1) * LANES) for k in range(d // LANES)]


def _rms_stats(x_ref, rstd_scr):
    tm, d = x_ref.shape

    def body(r, carry):
        rows = _row_chunk(r)
        acc = jnp.zeros((NORM_ROWS, LANES), F32)
        for cols in _col_groups(d):
            xk = x_ref[rows, cols]
            acc = acc + xk * xk
        ms = jnp.sum(acc, axis=-1, keepdims=True) * (1.0 / d)
        rstd_scr[rows, :] = jnp.broadcast_to(lax.rsqrt(ms + EPS), (NORM_ROWS, LANES))
        return carry

    lax.fori_loop(0, tm // NORM_ROWS, body, 0)


def _modnorm_apply(x_ref, rstd_scr, g_ref, sh_ref, sc_ref, mod, h_scr):
    tm, d = x_ref.shape
    vec = slice(mod, mod + 1)

    def body(r, carry):
        rows = _row_chunk(r)
        rstd = rstd_scr[rows, :]
        for cols in _col_groups(d):
            gain = g_ref[vec, cols] * (1.0 + sc_ref[vec, cols])
            h_scr[rows, cols] = ((x_ref[rows, cols] * rstd) * gain
                                 + sh_ref[vec, cols]).astype(h_scr.dtype)
        return carry

    lax.fori_loop(0, tm // NORM_ROWS, body, 0)


def _row_tile_copy(x_hbm, x_buf, sem, i):
    tm = x_buf.shape[0]
    return pltpu.make_async_copy(x_hbm.at[pl.ds(i * tm, tm), :], x_buf, sem)


def _await_row_tile(x_hbm, x_buf, sem, i):
    @pl.when(i == 0)
    def _():
        _row_tile_copy(x_hbm, x_buf, sem, 0).start()

    _row_tile_copy(x_hbm, x_buf, sem, i).wait()


def _prefetch_next_row_tile(x_hbm, x_buf, sem, i):
    @pl.when(i + 1 < pl.num_programs(0))
    def _():
        _row_tile_copy(x_hbm, x_buf, sem, i + 1).start()


SIDE_COLS = 2048
ROW_SPLIT = 2


def _cast_cols(in_ref, o_ref, c0, o0):
    n = min(SIDE_COLS, in_ref.shape[1] - c0)
    o_ref[:, o0:o0 + n] = in_ref[:, c0:c0 + n].astype(o_ref.dtype)


def _matvec_cols(cb_ref, w_ref, acc_ref, c0):
    ca = _silu(cb_ref[...])
    for k in range(c0 // LANES, min(c0 + SIDE_COLS, w_ref.shape[1]) // LANES):
        cols = slice(k * LANES, (k + 1) * LANES)
        prod = w_ref[:, cols] * ca
        part = prod[0:SUBLANES]
        for q in range(1, prod.shape[0] // SUBLANES):
            part = part + prod[q * SUBLANES:(q + 1) * SUBLANES]
        acc_ref[:, cols] += part


def _conv_in_kernel(*refs, cast_widths, n_matvec):
    refs = list(refs)
    take = lambda n: [refs.pop(0) for _ in range(n)]
    x_hbm, g_ref, sh_ref, sc_ref, wb_ref, wc_ref, wu_ref, wz_ref, cw_ref = take(9)
    cast_in = [take(len(widths)) for widths in cast_widths]
    cb_ref = take(1)[0] if n_matvec else None
    mv_in = [take(2) for _ in range(n_matvec)]
    y_ref = take(1)[0]
    cast_out = take(len(cast_widths))
    mv_out = take(n_matvec)
    x_buf, x_sem, h_scr, rstd_scr, halo_scr = take(5)
    mv_acc = take(n_matvec)

    i = pl.program_id(0)
    j = pl.program_id(1)
    tm = y_ref.shape[0]
    step = i * pl.num_programs(1) + j
    last_step = pl.num_programs(0) * pl.num_programs(1) - 1

    @pl.when(step == 0)
    def _():
        for acc_ref in mv_acc:
            acc_ref[...] = jnp.zeros(acc_ref.shape, F32)

    @pl.when(j == 0)
    def _():
        _await_row_tile(x_hbm, x_buf, x_sem, i)
        _rms_stats(x_buf, rstd_scr)
        _modnorm_apply(x_buf, rstd_scr, g_ref, sh_ref, sc_ref, 0, h_scr)
        _prefetch_next_row_tile(x_hbm, x_buf, x_sem, i)

    @pl.when(i == 0)
    def _():
        halo_scr[j] = jnp.zeros(halo_scr.shape[1:], F32)

    side_jobs = []
    for ins, out in zip(cast_in, cast_out):
        col = 0
        for r in ins:
            for c0 in range(0, r.shape[1], SIDE_COLS):
                side_jobs.append(functools.partial(_cast_cols, r, out, c0, col + c0))
            col += r.shape[1]
    for (w_ref, _), acc_ref in zip(mv_in, mv_acc):
        for c0 in range(0, w_ref.shape[1], SIDE_COLS):
            side_jobs.append(functools.partial(_matvec_cols, cb_ref, w_ref, acc_ref, c0))
    proj = [[None] * ROW_SPLIT for _ in range(4)]
    n_gaps = 4 * ROW_SPLIT - 1
    rows_per = tm // ROW_SPLIT
    for n, (half, k) in enumerate((half, k) for half in range(ROW_SPLIT) for k in range(4)):
        w_ref = (wc_ref, wu_ref, wb_ref, wz_ref)[k]
        rows = slice(half * rows_per, (half + 1) * rows_per)
        proj[k][half] = jnp.dot(h_scr[rows, :], w_ref[...], preferred_element_type=F32)
        if n < n_gaps:
            for job in side_jobs[n::n_gaps]:
                job()
    c_proj, u_proj, b_proj, z_proj = [jnp.concatenate(p, axis=0) for p in proj]

    cu = c_proj * u_proj
    w0 = cw_ref[0:1, :]
    w1 = cw_ref[1:2, :]
    w2 = cw_ref[2:3, :]
    conv = w0 * pltpu.roll(cu, 2, axis=0) + w1 * pltpu.roll(cu, 1, axis=0) + w2 * cu
    gate = b_proj * _silu(z_proj)
    y_ref[...] = (gate * conv).astype(y_ref.dtype)

    prev = halo_scr[j]
    top = cu[0:SUBLANES]
    row = lax.broadcasted_iota(jnp.int32, top.shape, 0)
    back1 = jnp.where(row < 1, pltpu.roll(prev, 1, axis=0), pltpu.roll(top, 1, axis=0))
    back2 = jnp.where(row < 2, pltpu.roll(prev, 2, axis=0), pltpu.roll(top, 2, axis=0))
    conv_top = w0 * back2 + w1 * back1 + w2 * top
    y_ref[0:SUBLANES, :] = (gate[0:SUBLANES] * conv_top).astype(y_ref.dtype)
    halo_scr[j] = cu[tm - SUBLANES:tm]

    if n_matvec:
        @pl.when(step == last_step)
        def _():
            for (_, b_ref), o_ref, acc_ref in zip(mv_in, mv_out, mv_acc):
                o_ref[...] = jnp.sum(acc_ref[...], axis=0, keepdims=True) + b_ref[...]


CONV_TM, CONV_TN = 1024, 256
BF16_ROWS = 16


def _side_rows(s, c, k):
    steps = (s // CONV_TM) * (c // CONV_TN)
    rows = k // steps
    return rows if rows * steps == k and rows % BF16_ROWS == 0 else 0


def _conv_in(x, g, shift, scale, w_in, conv_w, cast_jobs=(), matvec_jobs=(), cb=None):
    tm, tn = CONV_TM, CONV_TN
    s, d = x.shape
    c = conv_w.shape[1]
    nj = c // tn
    vec = pl.BlockSpec((1, d), lambda i, j: (0, 0))
    w_specs = [pl.BlockSpec((d, tn), lambda i, j, k=k: (0, k * nj + j)) for k in range(4)]
    row_block = lambda rows, n: pl.BlockSpec((rows, n), lambda i, j: (i * nj + j, 0))
    whole = lambda n: pl.BlockSpec((1, n), lambda i, j: (0, 0))

    side_in, side_in_specs = [], []
    out_shapes = [jax.ShapeDtypeStruct((s, c), BF16)]
    out_specs = [pl.BlockSpec((tm, tn), lambda i, j: (i, j))]
    scratch = [pltpu.VMEM((tm, d), F32), pltpu.SemaphoreType.DMA(()),
               pltpu.VMEM((tm, d), BF16), pltpu.VMEM((tm, LANES), F32),
               pltpu.VMEM((nj, SUBLANES, tn), F32)]
    for job in cast_jobs:
        k = job[0].shape[0]
        rows = _side_rows(s, c, k)
        assert rows and all(a.shape[0] == k and a.shape[1] % LANES == 0 for a in job)
        side_in += list(job)
        side_in_specs += [row_block(rows, a.shape[1]) for a in job]
        n = sum(a.shape[1] for a in job)
        out_shapes.append(jax.ShapeDtypeStruct((k, n), BF16))
        out_specs.append(row_block(rows, n))
    if matvec_jobs:
        k = cb.shape[0]
        rows = _side_rows(s, c, k)
        assert rows
        side_in.append(cb)
        side_in_specs.append(row_block(rows, LANES))
    for w, b in matvec_jobs:
        n = w.shape[1]
        assert w.shape[0] == k and n % LANES == 0
        side_in += [w, b.reshape(1, n)]
        side_in_specs += [row_block(rows, n), whole(n)]
        out_shapes.append(jax.ShapeDtypeStruct((1, n), F32))
        out_specs.append(whole(n))
        scratch.append(pltpu.VMEM((SUBLANES, n), F32))

    body = functools.partial(_conv_in_kernel, cast_widths=tuple(tuple(a.shape[1] for a in job)
                                                                for job in cast_jobs),
                             n_matvec=len(matvec_jobs))
    outs = pl.pallas_call(
        body,
        grid=(s // tm, nj),
        in_specs=[pl.BlockSpec(memory_space=pl.ANY), vec, vec, vec, *w_specs,
                  pl.BlockSpec((CONV_WIDTH, tn), lambda i, j: (0, j)), *side_in_specs],
        out_specs=out_specs,
        out_shape=out_shapes,
        scratch_shapes=scratch,
        compiler_params=_params("arbitrary", "arbitrary"),
        name="conv_in_proj",
    )(x, g, shift, scale, w_in, w_in, w_in, w_in, conv_w, *side_in)
    n_cast = len(cast_jobs)
    return outs[0], list(outs[1:1 + n_cast]), list(outs[1 + n_cast:])


def _out_proj_kernel(y_ref, w_ref, x_ref, gate_ref, o_ref):
    acc = jnp.dot(y_ref[...], w_ref[...], preferred_element_type=F32)
    o_ref[...] = x_ref[...] + gate_ref[...] * acc


def _out_proj(y, w, x, gate, tm=1024, tn=1024):
    s, k = y.shape
    d = w.shape[1]
    return pl.pallas_call(
        _out_proj_kernel,
        grid=(s // tm, d // tn),
        in_specs=[pl.BlockSpec((tm, k), lambda i, j: (i, 0)),
                  pl.BlockSpec((k, tn), lambda i, j: (0, j)),
                  pl.BlockSpec((tm, tn), lambda i, j: (i, j)),
                  pl.BlockSpec((1, tn), lambda i, j: (0, j))],
        out_specs=pl.BlockSpec((tm, tn), lambda i, j: (i, j)),
        out_shape=jax.ShapeDtypeStruct((s, d), F32),
        compiler_params=_params("arbitrary", "arbitrary"),
        name="out_proj",
    )(y, w, x, gate)


def _out_proj_norm_kernel(y_ref, w_ref, x_ref, gate_ref, gf_ref, o_ref):
    acc = jnp.dot(y_ref[...], w_ref[...], preferred_element_type=F32)
    x2 = x_ref[...] + gate_ref[...] * acc
    ms = jnp.mean(x2 * x2, axis=-1, keepdims=True)
    o_ref[...] = (x2 * lax.rsqrt(ms + EPS)) * gf_ref[...]


def _out_proj_norm(y, w, x, gate, g_final, tm=256):
    s, k = y.shape
    d = w.shape[1]
    vec = pl.BlockSpec((1, d), lambda i: (0, 0))
    return pl.pallas_call(
        _out_proj_norm_kernel,
        grid=(s // tm,),
        in_specs=[pl.BlockSpec((tm, k), lambda i: (i, 0)),
                  pl.BlockSpec((k, d), lambda i: (0, 0), pipeline_mode=pl.Buffered(1)),
                  pl.BlockSpec((tm, d), lambda i: (i, 0)), vec, vec],
        out_specs=pl.BlockSpec((tm, d), lambda i: (i, 0)),
        out_shape=jax.ShapeDtypeStruct((s, d), F32),
        compiler_params=_params("arbitrary", vmem=60 * MIB),
        name="out_proj_final_norm",
    )(y, w, x, gate, g_final)


def _norm_proj_kernel(x_hbm, g_ref, sh_ref, sc_ref, w_ref, o_ref, x_buf, x_sem, h_scr, rstd_scr,
                      *, tail_tiles, scaled_tiles, scale, silu_tiles):
    i = pl.program_id(0)
    j = pl.program_id(1)

    @pl.when(j == 0)
    def _():
        _await_row_tile(x_hbm, x_buf, x_sem, i)
        _rms_stats(x_buf, rstd_scr)
        _modnorm_apply(x_buf, rstd_scr, g_ref, sh_ref, sc_ref, 1 if tail_tiles else 0, h_scr)
        if not tail_tiles:
            _prefetch_next_row_tile(x_hbm, x_buf, x_sem, i)

    if tail_tiles:
        @pl.when(j == tail_tiles)
        def _():
            _modnorm_apply(x_buf, rstd_scr, g_ref, sh_ref, sc_ref, 0, h_scr)
            _prefetch_next_row_tile(x_hbm, x_buf, x_sem, i)

    gated = (j >= tail_tiles + silu_tiles[0]) & (j < tail_tiles + silu_tiles[1])

    @pl.when(gated)
    def _():
        acc = jnp.dot(h_scr[...], w_ref[...], preferred_element_type=F32)
        o_ref[...] = _silu(acc).astype(o_ref.dtype)

    @pl.when(jnp.logical_not(gated))
    def _():
        acc = jnp.dot(h_scr[...], w_ref[...], preferred_element_type=F32)
        if scaled_tiles:
            scaled = (j >= tail_tiles) & (j < tail_tiles + scaled_tiles)
            acc = acc * jnp.where(scaled, scale, 1.0)
        o_ref[...] = acc.astype(o_ref.dtype)


def _norm_proj(x, g, shift, scale, w, tail_cols=0, scaled_cols=0, col_scale=1.0,
               silu_cols=(0, 0), tm=1024, tn=1024):
    s, d = x.shape
    n = w.shape[1]
    n_mod = g.shape[0]
    assert scaled_cols % tn == 0 and tail_cols % tn == 0 and n_mod == (2 if tail_cols else 1)
    assert all(c % tn == 0 and scaled_cols <= c <= n - tail_cols for c in silu_cols)
    n_tiles, tail_tiles = n // tn, tail_cols // tn
    col_tile = lambda j: (j + n_tiles - tail_tiles) % n_tiles
    vec = pl.BlockSpec((n_mod, d), lambda i, j: (0, 0))
    body = functools.partial(_norm_proj_kernel, tail_tiles=tail_tiles,
                             scaled_tiles=scaled_cols // tn, scale=col_scale,
                             silu_tiles=tuple(c // tn for c in silu_cols))
    return pl.pallas_call(
        body,
        grid=(s // tm, n_tiles),
        in_specs=[pl.BlockSpec(memory_space=pl.ANY), vec, vec, vec,
                  pl.BlockSpec((d, tn), lambda i, j: (0, col_tile(j)))],
        out_specs=pl.BlockSpec((tm, tn), lambda i, j: (i, col_tile(j))),
        out_shape=jax.ShapeDtypeStruct((s, n), BF16),
        scratch_shapes=[pltpu.VMEM((tm, d), F32), pltpu.SemaphoreType.DMA(()),
                        pltpu.VMEM((tm, d), BF16), pltpu.VMEM((tm, LANES), F32)],
        compiler_params=_params("arbitrary", "arbitrary"),
        name="norm_proj",
    )(x, g, shift, scale, w)


def _t5_bucket(dist):
    max_exact = N_BUCKETS // 2
    d = jnp.maximum(dist, 0)
    d_f = jnp.maximum(d, 1).astype(F32)
    large = max_exact + (jnp.log(d_f / max_exact) / math.log(MAX_DISTANCE / max_exact)
                         * (N_BUCKETS - max_exact)).astype(jnp.int32)
    large = jnp.minimum(large, N_BUCKETS - 1)
    return jnp.where(d < max_exact, d, large)


def _band_buckets():
    b = jnp.arange(2 * BLOCK)[:, None]
    a = jnp.arange(BLOCK)[None, :]
    dist = a + BLOCK - b
    band = (dist >= 0) & (dist < BLOCK)
    later = jnp.where(band, _t5_bucket(dist), -1)
    first = jnp.where(b >= BLOCK, later, -1)
    return jnp.stack([first, later]).astype(jnp.int32)


def _bias_table_kernel(rb_ref, bucket_ref, o_ref):
    g = pl.program_id(1)
    bucket = bucket_ref[0]
    for hh in range(GROUP):
        acc = jnp.full(bucket.shape, -jnp.inf, F32)
        for k in range(N_BUCKETS):
            acc = jnp.where(bucket == k, rb_ref[k, g * GROUP + hh] * LOG2E, acc)
        o_ref[0, 0, :, hh * BLOCK:(hh + 1) * BLOCK] = acc


def _bias_table(rel_bias):
    n_kv = rel_bias.shape[1] // GROUP
    return pl.pallas_call(
        _bias_table_kernel,
        grid=(2, n_kv),
        in_specs=[pl.BlockSpec(memory_space=pltpu.SMEM),
                  pl.BlockSpec((1, 2 * BLOCK, BLOCK), lambda v, g: (v, 0, 0))],
        out_specs=pl.BlockSpec((1, 1, 2 * BLOCK, GROUP * BLOCK), lambda v, g: (v, g, 0, 0)),
        out_shape=jax.ShapeDtypeStruct((2, n_kv, 2 * BLOCK, GROUP * BLOCK), F32),
        compiler_params=_params("arbitrary", "arbitrary"),
        name="bias_table",
    )(rel_bias, _band_buckets())


def _attn_kernel(sink_ref, qz_ref, kvp_ref, kvc_ref, bias_ref, o_ref):
    n_kv = bias_ref.shape[1]
    attn_dim, kv_dim = o_ref.shape[1], n_kv * HEAD_DIM
    q_ref, z_ref = qz_ref.at[:, :attn_dim], qz_ref.at[:, attn_dim:]
    kp_ref, vp_ref = kvp_ref.at[:, :kv_dim], kvp_ref.at[:, kv_dim:]
    kc_ref, vc_ref = kvc_ref.at[:, :kv_dim], kvc_ref.at[:, kv_dim:]
    zero_half = jnp.zeros((HEAD_DIM, GROUP * BLOCK), BF16)
    ones_rows = jnp.ones((BF16_ROWS, 2 * BLOCK), BF16)
    for g in range(n_kv):
        heads = [g * GROUP + hh for hh in range(GROUP)]
        kv_cols = slice((g // 2) * LANES, (g // 2 + 1) * LANES)
        k2 = jnp.concatenate([kp_ref[:, kv_cols], kc_ref[:, kv_cols]], axis=0)
        v2_t = jnp.concatenate([vp_ref[:, kv_cols].T, vc_ref[:, kv_cols].T], axis=1)
        v2_t = jnp.concatenate([v2_t, ones_rows], axis=0)
        q_t = []
        for h in heads[::2]:
            both = q_ref[:, (h // 2) * LANES:(h // 2 + 1) * LANES].T
            q_t += [both[:HEAD_DIM], both[HEAD_DIM:]]
        qg_t = jnp.concatenate(q_t, axis=1)
        rhs = jnp.concatenate([qg_t, zero_half] if g % 2 == 0 else [zero_half, qg_t], axis=0)
        s_t = jnp.dot(k2, rhs, preferred_element_type=F32) + bias_ref[0, g]
        sink = jnp.concatenate([jnp.full((1, BLOCK), sink_ref[h] * LOG2E, F32) for h in heads],
                               axis=1)
        m = jnp.maximum(jnp.max(s_t, axis=0, keepdims=True), sink)
        p = jnp.exp2(s_t - m).astype(BF16)
        o2_t = jnp.dot(v2_t, p, preferred_element_type=F32)
        den = o2_t[2 * HEAD_DIM:2 * HEAD_DIM + 1] + jnp.exp2(sink - m)
        o_t = (o2_t[:HEAD_DIM] if g % 2 == 0 else o2_t[HEAD_DIM:2 * HEAD_DIM]) / den
        for hh in range(0, GROUP, 2):
            cols = slice((heads[hh] // 2) * LANES, (heads[hh] // 2 + 1) * LANES)
            both = jnp.concatenate([o_t[:, hh * BLOCK:(hh + 1) * BLOCK],
                                    o_t[:, (hh + 1) * BLOCK:(hh + 2) * BLOCK]], axis=0)
            o_ref[:, cols] = (both.T * z_ref[:, cols].astype(F32)).astype(o_ref.dtype)


def _attention(qz, qz_blk, kv, kv_blk, bias, sinks):
    s = qz.shape[0]
    n_kv = bias.shape[1]
    kv_dim = n_kv * HEAD_DIM
    attn_dim = kv_dim * GROUP
    assert n_kv % 2 == 0 and GROUP % 2 == 0 and 2 * HEAD_DIM == LANES and BLOCK == LANES
    prev = lambda i: jnp.maximum(i - 1, 0)
    return pl.pallas_call(
        _attn_kernel,
        grid=(s // BLOCK,),
        in_specs=[pl.BlockSpec(memory_space=pltpu.SMEM),
                  pl.BlockSpec((BLOCK, 2 * attn_dim), lambda i: (i, qz_blk)),
                  pl.BlockSpec((BLOCK, 2 * kv_dim), lambda i: (prev(i), kv_blk)),
                  pl.BlockSpec((BLOCK, 2 * kv_dim), lambda i: (i, kv_blk)),
                  pl.BlockSpec((1, n_kv, 2 * BLOCK, GROUP * BLOCK),
                               lambda i: (jnp.minimum(i, 1), 0, 0, 0))],
        out_specs=pl.BlockSpec((BLOCK, attn_dim), lambda i: (i, 0)),
        out_shape=jax.ShapeDtypeStruct((s, attn_dim), BF16),
        compiler_params=_params("arbitrary"),
        name="swa_attention",
    )(sinks, qz, kv, kv, bias)


def _trunk(x, c, g_a, w_mod_a, b_mod_a, w_a_in, conv_a, w_a_out, g_kv, w_mod_kv, b_mod_kv,
           w_kv, rel_bias, g_b, w_mod_b, b_mod_b, w_b_in, sinks_b, w_b_out, g_final):
    s, d = x.shape
    n_a, n_b = w_a_in.shape[0], w_b_in.shape[0]
    assert n_b >= 1 and s % BLOCK == 0
    row = lambda v: v.reshape(1, -1)
    cb = jnp.broadcast_to(c.reshape(d, 1), (d, LANES))

    side_operands = (w_a_out[-1], w_b_in[0], w_kv, w_b_out[0], w_mod_kv, w_mod_b[0]) if n_a else ()
    stream = n_a > 0 and all(_side_rows(s, conv_a.shape[2], a.shape[0]) for a in side_operands)
    w_a_out_bf = w_cat = w_b_out_bf = mod_kv = mod_b = None
    for i in range(n_a):
        shift, scale, gate = jnp.split(_adaln(cb, w_mod_a[i], b_mod_a[i]), 3, axis=-1)
        args = (x, row(g_a[i]), shift, scale, w_a_in[i].astype(BF16), conv_a[i])
        if stream and i == n_a - 1:
            y, (w_a_out_bf, w_cat, w_b_out_bf), (mod_kv, mod_b) = _conv_in(
                *args, cast_jobs=((w_a_out[i],), (w_b_in[0], w_kv), (w_b_out[0],)),
                matvec_jobs=((w_mod_kv, b_mod_kv), (w_mod_b[0], b_mod_b[0])), cb=cb)
        else:
            y, _, _ = _conv_in(*args)
            w_a_out_bf = w_a_out[i].astype(BF16)
        x = _out_proj(y, w_a_out_bf, x, gate)

    if mod_kv is None:
        mod_kv = _adaln(cb, w_mod_kv, b_mod_kv)
    shift_kv, scale_kv = jnp.split(mod_kv, 2, axis=-1)
    bias = _bias_table(rel_bias)
    attn_dim = w_b_in.shape[2] // 2
    kv_dim = w_kv.shape[1] // 2
    kv = None
    for i in range(n_b):
        if i > 0 or mod_b is None:
            mod_b = _adaln(cb, w_mod_b[i], b_mod_b[i])
        shift, scale, gate = jnp.split(mod_b, 3, axis=-1)
        q_scale = dict(scaled_cols=attn_dim, col_scale=LOG2E * HEAD_DIM ** -0.5,
                       silu_cols=(attn_dim, 2 * attn_dim))
        if i == 0:
            if w_cat is None:
                w_cat = jnp.concatenate([w_b_in[i], w_kv], axis=1).astype(BF16)
            qz = _norm_proj(x, jnp.stack([g_b[i], g_kv]), jnp.concatenate([shift, shift_kv]),
                            jnp.concatenate([scale, scale_kv]), w_cat,
                            tail_cols=2 * kv_dim, **q_scale)
            kv, kv_blk = qz, attn_dim // kv_dim
        else:
            qz = _norm_proj(x, row(g_b[i]), shift, scale, w_b_in[i].astype(BF16), **q_scale)
        a = _attention(qz, 0, kv, kv_blk, bias, sinks_b[i])
        w_out = w_b_out_bf if i == 0 and w_b_out_bf is not None else w_b_out[i].astype(BF16)
        if i + 1 < n_b:
            x = _out_proj(a, w_out, x, gate)
        else:
            x = _out_proj_norm(a, w_out, x, gate, row(g_final))
    return x


def kernel(x, c, g_a, w_mod_a, b_mod_a, w_a_in, conv_a, w_a_out, g_kv, w_mod_kv, b_mod_kv, w_kv,
           rel_bias, g_b, w_mod_b, b_mod_b, w_b_in, sinks_b, w_b_out, g_final):
    outs = [_trunk(x[b], c[b], g_a, w_mod_a, b_mod_a, w_a_in, conv_a, w_a_out, g_kv, w_mod_kv,
                   b_mod_kv, w_kv, rel_bias, g_b, w_mod_b, b_mod_b, w_b_in, sinks_b, w_b_out,
                   g_final) for b in range(x.shape[0])]
    return jnp.stack(outs, axis=0)
```

```python
import functools
import math

import jax
import jax.numpy as jnp
from jax import lax
from jax.experimental import pallas as pl
from jax.experimental.pallas import tpu as pltpu

HEAD_DIM = 64
GROUP = 8
BLOCK = 128
N_BUCKETS = 32
MAX_DISTANCE = 128
CONV_WIDTH = 3
EPS = 1e-6
LOG2E = math.log2(math.e)

LANES = 128
SUBLANES = 8
MIB = 1024 * 1024
VMEM_LIMIT = 56 * MIB

F32 = jnp.float32
BF16 = jnp.bfloat16
NORM_ROWS = 128


def _params(*semantics, vmem=VMEM_LIMIT, flags=None):
    return pltpu.CompilerParams(dimension_semantics=semantics, vmem_limit_bytes=vmem, flags=flags)


def _silu(v):
    return v * jax.nn.sigmoid(v)


def _adaln_kernel(cb_ref, w_ref, b_ref, o_ref):
    ca = _silu(cb_ref[...])
    for g in range(o_ref.shape[1] // LANES):
        cols = slice(g * LANES, (g + 1) * LANES)
        o_ref[:, cols] = jnp.sum(w_ref[:, cols] * ca, axis=0, keepdims=True) + b_ref[:, cols]


def _adaln(cb, w, b, tn=1024):
    k, n = w.shape
    return pl.pallas_call(
        _adaln_kernel,
        grid=(n // tn,),
        in_specs=[
            pl.BlockSpec((k, LANES), lambda j: (0, 0)),
            pl.BlockSpec((k, tn), lambda j: (0, j)),
            pl.BlockSpec((1, tn), lambda j: (0, j)),
        ],
        out_specs=pl.BlockSpec((1, tn), lambda j: (0, j)),
        out_shape=jax.ShapeDtypeStruct((1, n), F32),
        compiler_params=_params("arbitrary"),
        name="adaln_matvec",
    )(cb, w, b.reshape(1, n))


def _row_chunk(r):
    return pl.ds(pl.multiple_of(r * NORM_ROWS, NORM_ROWS), NORM_ROWS)


def _col_groups(d):
    return [slice(k * LANES, (k + 1) * LANES) for k in range(d // LANES)]


def _rms_stats(x_ref, rstd_scr):
    tm, d = x_ref.shape

    def body(r, carry):
        rows = _row_chunk(r)
        acc = jnp.zeros((NORM_ROWS, LANES), F32)
        for cols in _col_groups(d):
            xk = x_ref[rows, cols]
            acc = acc + xk * xk
        ms = jnp.sum(acc, axis=-1, keepdims=True) * (1.0 / d)
        rstd_scr[rows, :] = jnp.broadcast_to(lax.rsqrt(ms + EPS), (NORM_ROWS, LANES))
        return carry

    lax.fori_loop(0, tm // NORM_ROWS, body, 0)


def _modnorm_apply(x_ref, rstd_scr, g_ref, sh_ref, sc_ref, mod, h_scr):
    tm, d = x_ref.shape
    vec = slice(mod, mod + 1)

    def body(r, carry):
        rows = _row_chunk(r)
        rstd = rstd_scr[rows, :]
        for cols in _col_groups(d):
            gain = g_ref[vec, cols] * (1.0 + sc_ref[vec, cols])
            h_scr[rows, cols] = ((x_ref[rows, cols] * rstd) * gain
                                 + sh_ref[vec, cols]).astype(h_scr.dtype)
        return carry

    lax.fori_loop(0, tm // NORM_ROWS, body, 0)


def _row_tile_copy(x_hbm, x_buf, sem, i):
    tm = x_buf.shape[0]
    return pltpu.make_async_copy(x_hbm.at[pl.ds(i * tm, tm), :], x_buf, sem)


def _await_row_tile(x_hbm, x_buf, sem, i):
    @pl.when(i == 0)
    def _():
        _row_tile_copy(x_hbm, x_buf, sem, 0).start()

    _row_tile_copy(x_hbm, x_buf, sem, i).wait()


def _prefetch_next_row_tile(x_hbm, x_buf, sem, i):
    @pl.when(i + 1 < pl.num_programs(0))
    def _():
        _row_tile_copy(x_hbm, x_buf, sem, i + 1).start()


SIDE_COLS = 2048
ROW_SPLIT = 2


def _cast_cols(in_ref, o_ref, c0, o0):
    n = min(SIDE_COLS, in_ref.shape[1] - c0)
    o_ref[:, o0:o0 + n] = in_ref[:, c0:c0 + n].astype(o_ref.dtype)


def _matvec_cols(cb_ref, w_ref, acc_ref, c0):
    ca = _silu(cb_ref[...])
    for k in range(c0 // LANES, min(c0 + SIDE_COLS, w_ref.shape[1]) // LANES):
        cols = slice(k * LANES, (k + 1) * LANES)
        prod = w_ref[:, cols] * ca
        part = prod[0:SUBLANES]
        for q in range(1, prod.shape[0] // SUBLANES):
            part = part + prod[q * SUBLANES:(q + 1) * SUBLANES]
        acc_ref[:, cols] += part


def _conv_in_kernel(*refs, cast_widths, n_matvec):
    refs = list(refs)
    take = lambda n: [refs.pop(0) for _ in range(n)]
    x_hbm, g_ref, sh_ref, sc_ref, wb_ref, wc_ref, wu_ref, wz_ref, cw_ref = take(9)
    cast_in = [take(len(widths)) for widths in cast_widths]
    cb_ref = take(1)[0] if n_matvec else None
    mv_in = [take(2) for _ in range(n_matvec)]
    y_ref = take(1)[0]
    cast_out = take(len(cast_widths))
    mv_out = take(n_matvec)
    x_buf, x_sem, h_scr, rstd_scr, halo_scr = take(5)
    mv_acc = take(n_matvec)

    i = pl.program_id(0)
    j = pl.program_id(1)
    tm = y_ref.shape[0]
    step = i * pl.num_programs(1) + j
    last_step = pl.num_programs(0) * pl.num_programs(1) - 1

    @pl.when(step == 0)
    def _():
        for acc_ref in mv_acc:
            acc_ref[...] = jnp.zeros(acc_ref.shape, F32)

    @pl.when(j == 0)
    def _():
        _await_row_tile(x_hbm, x_buf, x_sem, i)
        _rms_stats(x_buf, rstd_scr)
        _modnorm_apply(x_buf, rstd_scr, g_ref, sh_ref, sc_ref, 0, h_scr)
        _prefetch_next_row_tile(x_hbm, x_buf, x_sem, i)

    @pl.when(i == 0)
    def _():
        halo_scr[j] = jnp.zeros(halo_scr.shape[1:], F32)

    side_jobs = []
    for ins, out in zip(cast_in, cast_out):
        col = 0
        for r in ins:
            for c0 in range(0, r.shape[1], SIDE_COLS):
                side_jobs.append(functools.partial(_cast_cols, r, out, c0, col + c0))
            col += r.shape[1]
    for (w_ref, _), acc_ref in zip(mv_in, mv_acc):
        for c0 in range(0, w_ref.shape[1], SIDE_COLS):
            side_jobs.append(functools.partial(_matvec_cols, cb_ref, w_ref, acc_ref, c0))
    proj = [[None] * ROW_SPLIT for _ in range(4)]
    n_gaps = 4 * ROW_SPLIT - 1
    rows_per = tm // ROW_SPLIT
    for n, (half, k) in enumerate((half, k) for half in range(ROW_SPLIT) for k in range(4)):
        w_ref = (wc_ref, wu_ref, wb_ref, wz_ref)[k]
        rows = slice(half * rows_per, (half + 1) * rows_per)
        proj[k][half] = jnp.dot(h_scr[rows, :], w_ref[...], preferred_element_type=F32)
        if n < n_gaps:
            for job in side_jobs[n::n_gaps]:
                job()
    c_proj, u_proj, b_proj, z_proj = [jnp.concatenate(p, axis=0) for p in proj]

    cu = c_proj * u_proj
    w0 = cw_ref[0:1, :]
    w1 = cw_ref[1:2, :]
    w2 = cw_ref[2:3, :]
    conv = w0 * pltpu.roll(cu, 2, axis=0) + w1 * pltpu.roll(cu, 1, axis=0) + w2 * cu
    gate = b_proj * _silu(z_proj)
    y_ref[...] = (gate * conv).astype(y_ref.dtype)

    prev = halo_scr[j]
    top = cu[0:SUBLANES]
    row = lax.broadcasted_iota(jnp.int32, top.shape, 0)
    back1 = jnp.where(row < 1, pltpu.roll(prev, 1, axis=0), pltpu.roll(top, 1, axis=0))
    back2 = jnp.where(row < 2, pltpu.roll(prev, 2, axis=0), pltpu.roll(top, 2, axis=0))
    conv_top = w0 * back2 + w1 * back1 + w2 * top
    y_ref[0:SUBLANES, :] = (gate[0:SUBLANES] * conv_top).astype(y_ref.dtype)
    halo_scr[j] = cu[tm - SUBLANES:tm]

    if n_matvec:
        @pl.when(step == last_step)
        def _():
            for (_, b_ref), o_ref, acc_ref in zip(mv_in, mv_out, mv_acc):
                o_ref[...] = jnp.sum(acc_ref[...], axis=0, keepdims=True) + b_ref[...]


CONV_TM, CONV_TN = 1024, 256
BF16_ROWS = 16


def _side_rows(s, c, k):
    steps = (s // CONV_TM) * (c // CONV_TN)
    rows = k // steps
    return rows if rows * steps == k and rows % BF16_ROWS == 0 else 0


def _conv_in(x, g, shift, scale, w_in, conv_w, cast_jobs=(), matvec_jobs=(), cb=None):
    tm, tn = CONV_TM, CONV_TN
    s, d = x.shape
    c = conv_w.shape[1]
    nj = c // tn
    vec = pl.BlockSpec((1, d), lambda i, j: (0, 0))
    w_specs = [pl.BlockSpec((d, tn), lambda i, j, k=k: (0, k * nj + j)) for k in range(4)]
    row_block = lambda rows, n: pl.BlockSpec((rows, n), lambda i, j: (i * nj + j, 0))
    whole = lambda n: pl.BlockSpec((1, n), lambda i, j: (0, 0))

    side_in, side_in_specs = [], []
    out_shapes = [jax.ShapeDtypeStruct((s, c), BF16)]
    out_specs = [pl.BlockSpec((tm, tn), lambda i, j: (i, j))]
    scratch = [pltpu.VMEM((tm, d), F32), pltpu.SemaphoreType.DMA(()),
               pltpu.VMEM((tm, d), BF16), pltpu.VMEM((tm, LANES), F32),
               pltpu.VMEM((nj, SUBLANES, tn), F32)]
    for job in cast_jobs:
        k = job[0].shape[0]
        rows = _side_rows(s, c, k)
        assert rows and all(a.shape[0] == k and a.shape[1] % LANES == 0 for a in job)
        side_in += list(job)
        side_in_specs += [row_block(rows, a.shape[1]) for a in job]
        n = sum(a.shape[1] for a in job)
        out_shapes.append(jax.ShapeDtypeStruct((k, n), BF16))
        out_specs.append(row_block(rows, n))
    if matvec_jobs:
        k = cb.shape[0]
        rows = _side_rows(s, c, k)
        assert rows
        side_in.append(cb)
        side_in_specs.append(row_block(rows, LANES))
    for w, b in matvec_jobs:
        n = w.shape[1]
        assert w.shape[0] == k and n % LANES == 0
        side_in += [w, b.reshape(1, n)]
        side_in_specs += [row_block(rows, n), whole(n)]
        out_shapes.append(jax.ShapeDtypeStruct((1, n), F32))
        out_specs.append(whole(n))
        scratch.append(pltpu.VMEM((SUBLANES, n), F32))

    body = functools.partial(_conv_in_kernel, cast_widths=tuple(tuple(a.shape[1] for a in job)
                                                                for job in cast_jobs),
                             n_matvec=len(matvec_jobs))
    outs = pl.pallas_call(
        body,
        grid=(s // tm, nj),
        in_specs=[pl.BlockSpec(memory_space=pl.ANY), vec, vec, vec, *w_specs,
                  pl.BlockSpec((CONV_WIDTH, tn), lambda i, j: (0, j)), *side_in_specs],
        out_specs=out_specs,
        out_shape=out_shapes,
        scratch_shapes=scratch,
        compiler_params=_params("arbitrary", "arbitrary"),
        name="conv_in_proj",
    )(x, g, shift, scale, w_in, w_in, w_in, w_in, conv_w, *side_in)
    n_cast = len(cast_jobs)
    return outs[0], list(outs[1:1 + n_cast]), list(outs[1 + n_cast:])


def _out_proj_kernel(y_ref, w_ref, x_ref, gate_ref, o_ref):
    acc = jnp.dot(y_ref[...], w_ref[...], preferred_element_type=F32)
    o_ref[...] = x_ref[...] + gate_ref[...] * acc


def _out_proj(y, w, x, gate, tm=1024, tn=1024):
    s, k = y.shape
    d = w.shape[1]
    return pl.pallas_call(
        _out_proj_kernel,
        grid=(s // tm, d // tn),
        in_specs=[pl.BlockSpec((tm, k), lambda i, j: (i, 0)),
                  pl.BlockSpec((k, tn), lambda i, j: (0, j)),
                  pl.BlockSpec((tm, tn), lambda i, j: (i, j)),
                  pl.BlockSpec((1, tn), lambda i, j: (0, j))],
        out_specs=pl.BlockSpec((tm, tn), lambda i, j: (i, j)),
        out_shape=jax.ShapeDtypeStruct((s, d), F32),
        compiler_params=_params("arbitrary", "arbitrary"),
        name="out_proj",
    )(y, w, x, gate)


def _out_proj_norm_kernel(y_ref, w_ref, x_ref, gate_ref, gf_ref, o_ref):
    acc = jnp.dot(y_ref[...], w_ref[...], preferred_element_type=F32)
    x2 = x_ref[...] + gate_ref[...] * acc
    ms = jnp.mean(x2 * x2, axis=-1, keepdims=True)
    o_ref[...] = (x2 * lax.rsqrt(ms + EPS)) * gf_ref[...]


def _out_proj_norm(y, w, x, gate, g_final, tm=256):
    s, k = y.shape
    d = w.shape[1]
    vec = pl.BlockSpec((1, d), lambda i: (0, 0))
    return pl.pallas_call(
        _out_proj_norm_kernel,
        grid=(s // tm,),
        in_specs=[pl.BlockSpec((tm, k), lambda i: (i, 0)),
                  pl.BlockSpec((k, d), lambda i: (0, 0), pipeline_mode=pl.Buffered(1)),
                  pl.BlockSpec((tm, d), lambda i: (i, 0)), vec, vec],
        out_specs=pl.BlockSpec((tm, d), lambda i: (i, 0)),
        out_shape=jax.ShapeDtypeStruct((s, d), F32),
        compiler_params=_params("arbitrary", vmem=60 * MIB),
        name="out_proj_final_norm",
    )(y, w, x, gate, g_final)


def _norm_proj_kernel(x_hbm, g_ref, sh_ref, sc_ref, w_ref, o_ref, x_buf, x_sem, h_scr, rstd_scr,
                      *, tail_tiles, scaled_tiles, scale, silu_tiles):
    i = pl.program_id(0)
    j = pl.program_id(1)

    @pl.when(j == 0)
    def _():
        _await_row_tile(x_hbm, x_buf, x_sem, i)
        _rms_stats(x_buf, rstd_scr)
        _modnorm_apply(x_buf, rstd_scr, g_ref, sh_ref, sc_ref, 1 if tail_tiles else 0, h_scr)
        if not tail_tiles:
            _prefetch_next_row_tile(x_hbm, x_buf, x_sem, i)

    if tail_tiles:
        @pl.when(j == tail_tiles)
        def _():
            _modnorm_apply(x_buf, rstd_scr, g_ref, sh_ref, sc_ref, 0, h_scr)
            _prefetch_next_row_tile(x_hbm, x_buf, x_sem, i)

    gated = (j >= tail_tiles + silu_tiles[0]) & (j < tail_tiles + silu_tiles[1])

    @pl.when(gated)
    def _():
        acc = jnp.dot(h_scr[...], w_ref[...], preferred_element_type=F32)
        o_ref[...] = _silu(acc).astype(o_ref.dtype)

    @pl.when(jnp.logical_not(gated))
    def _():
        acc = jnp.dot(h_scr[...], w_ref[...], preferred_element_type=F32)
        if scaled_tiles:
            scaled = (j >= tail_tiles) & (j < tail_tiles + scaled_tiles)
            acc = acc * jnp.where(scaled, scale, 1.0)
        o_ref[...] = acc.astype(o_ref.dtype)


def _norm_proj(x, g, shift, scale, w, tail_cols=0, scaled_cols=0, col_scale=1.0,
               silu_cols=(0, 0), tm=1024, tn=1024):
    s, d = x.shape
    n = w.shape[1]
    n_mod = g.shape[0]
    assert scaled_cols % tn == 0 and tail_cols % tn == 0 and n_mod == (2 if tail_cols else 1)
    assert all(c % tn == 0 and scaled_cols <= c <= n - tail_cols for c in silu_cols)
    n_tiles, tail_tiles = n // tn, tail_cols // tn
    col_tile = lambda j: (j + n_tiles - tail_tiles) % n_tiles
    vec = pl.BlockSpec((n_mod, d), lambda i, j: (0, 0))
    body = functools.partial(_norm_proj_kernel, tail_tiles=tail_tiles,
                             scaled_tiles=scaled_cols // tn, scale=col_scale,
                             silu_tiles=tuple(c // tn for c in silu_cols))
    return pl.pallas_call(
        body,
        grid=(s // tm, n_tiles),
        in_specs=[pl.BlockSpec(memory_space=pl.ANY), vec, vec, vec,
                  pl.BlockSpec((d, tn), lambda i, j: (0, col_tile(j)))],
        out_specs=pl.BlockSpec((tm, tn), lambda i, j: (i, col_tile(j))),
        out_shape=jax.ShapeDtypeStruct((s, n), BF16),
        scratch_shapes=[pltpu.VMEM((tm, d), F32), pltpu.SemaphoreType.DMA(()),
                        pltpu.VMEM((tm, d), BF16), pltpu.VMEM((tm, LANES), F32)],
        compiler_params=_params("arbitrary", "arbitrary"),
        name="norm_proj",
    )(x, g, shift, scale, w)


def _t5_bucket(dist):
    max_exact = N_BUCKETS // 2
    d = jnp.maximum(dist, 0)
    d_f = jnp.maximum(d, 1).astype(F32)
    large = max_exact + (jnp.log(d_f / max_exact) / math.log(MAX_DISTANCE / max_exact)
                         * (N_BUCKETS - max_exact)).astype(jnp.int32)
    large = jnp.minimum(large, N_BUCKETS - 1)
    return jnp.where(d < max_exact, d, large)


def _band_buckets():
    b = jnp.arange(2 * BLOCK)[:, None]
    a = jnp.arange(BLOCK)[None, :]
    dist = a + BLOCK - b
    band = (dist >= 0) & (dist < BLOCK)
    later = jnp.where(band, _t5_bucket(dist), -1)
    first = jnp.where(b >= BLOCK, later, -1)
    return jnp.stack([first, later]).astype(jnp.int32)


HEAD_ORDER = tuple(range(0, GROUP, 2)) + tuple(range(1, GROUP, 2))


def _bias_table_kernel(rb_ref, bucket_ref, o_ref):
    g = pl.program_id(1)
    bucket = bucket_ref[0]
    for c, hh in enumerate(HEAD_ORDER):
        acc = jnp.full(bucket.shape, -jnp.inf, F32)
        for k in range(N_BUCKETS):
            acc = jnp.where(bucket == k, rb_ref[k, g * GROUP + hh] * LOG2E, acc)
        o_ref[0, 0, :, c * BLOCK:(c + 1) * BLOCK] = acc


def _bias_table(rel_bias):
    n_kv = rel_bias.shape[1] // GROUP
    return pl.pallas_call(
        _bias_table_kernel,
        grid=(2, n_kv),
        in_specs=[pl.BlockSpec(memory_space=pltpu.SMEM),
                  pl.BlockSpec((1, 2 * BLOCK, BLOCK), lambda v, g: (v, 0, 0))],
        out_specs=pl.BlockSpec((1, 1, 2 * BLOCK, GROUP * BLOCK), lambda v, g: (v, g, 0, 0)),
        out_shape=jax.ShapeDtypeStruct((2, n_kv, 2 * BLOCK, GROUP * BLOCK), F32),
        compiler_params=_params("arbitrary", "arbitrary"),
        name="bias_table",
    )(rel_bias, _band_buckets())


def _attn_kernel(sink_ref, qz_ref, kvp_ref, kvc_ref, bias_ref, o_ref):
    n_kv = bias_ref.shape[1]
    attn_dim, kv_dim = o_ref.shape[1], n_kv * HEAD_DIM
    q_ref, z_ref = qz_ref.at[:, :attn_dim], qz_ref.at[:, attn_dim:]
    kp_ref, vp_ref = kvp_ref.at[:, :kv_dim], kvp_ref.at[:, kv_dim:]
    kc_ref, vc_ref = kvc_ref.at[:, :kv_dim], kvc_ref.at[:, kv_dim:]
    ones_rows = jnp.ones((BF16_ROWS, 2 * BLOCK), BF16)
    low_half = lax.broadcasted_iota(jnp.int32, (2 * BLOCK, LANES), 1) < HEAD_DIM
    nt_dot = functools.partial(lax.dot_general, dimension_numbers=(((1,), (1,)), ((), ())),
                               preferred_element_type=F32)
    tiles = GROUP // 2
    for g in range(n_kv):
        heads = [g * GROUP + hh for hh in HEAD_ORDER]
        kv_cols = slice((g // 2) * LANES, (g // 2 + 1) * LANES)
        k2 = jnp.concatenate([kp_ref[:, kv_cols], kc_ref[:, kv_cols]], axis=0)
        v2_t = jnp.concatenate([vp_ref[:, kv_cols].T, vc_ref[:, kv_cols].T], axis=1)
        v2_t = jnp.concatenate([v2_t, ones_rows], axis=0)
        k_other = pltpu.roll(k2, HEAD_DIM, axis=1)
        k_low, k_high = (k2, k_other) if g % 2 == 0 else (k_other, k2)
        k_low = jnp.where(low_half, k_low, jnp.zeros_like(k_low))
        k_high = jnp.where(low_half, jnp.zeros_like(k_high), k_high)
        q_rows = jnp.concatenate([q_ref[:, (g * tiles + t) * LANES:(g * tiles + t + 1) * LANES]
                                  for t in range(tiles)], axis=0)
        s_t = jnp.concatenate([nt_dot(k_low, q_rows), nt_dot(k_high, q_rows)], axis=1)
        s_t = s_t + bias_ref[0, g]
        sink = jnp.concatenate([jnp.full((1, BLOCK), sink_ref[h] * LOG2E, F32) for h in heads],
                               axis=1)
        m = jnp.maximum(jnp.max(s_t, axis=0, keepdims=True), sink)
        p = jnp.exp2(s_t - m).astype(BF16)
        o2_t = jnp.dot(v2_t, p, preferred_element_type=F32)
        den = o2_t[2 * HEAD_DIM:2 * HEAD_DIM + 1] + jnp.exp2(sink - m)
        o_t = (o2_t[:HEAD_DIM] if g % 2 == 0 else o2_t[HEAD_DIM:2 * HEAD_DIM]) / den
        o_t = o_t.astype(o_ref.dtype)
        for t in range(tiles):
            cols = slice((g * tiles + t) * LANES, (g * tiles + t + 1) * LANES)
            both = jnp.concatenate([o_t[:, t * BLOCK:(t + 1) * BLOCK],
                                    o_t[:, (tiles + t) * BLOCK:(tiles + t + 1) * BLOCK]], axis=0)
            o_ref[:, cols] = both.T * z_ref[:, cols]


def _attention(qz, qz_blk, kv, kv_blk, bias, sinks):
    s = qz.shape[0]
    n_kv = bias.shape[1]
    kv_dim = n_kv * HEAD_DIM
    attn_dim = kv_dim * GROUP
    assert n_kv % 2 == 0 and GROUP % 2 == 0 and 2 * HEAD_DIM == LANES and BLOCK == LANES
    prev = lambda i: jnp.maximum(i - 1, 0)
    return pl.pallas_call(
        _attn_kernel,
        grid=(s // BLOCK,),
        in_specs=[pl.BlockSpec(memory_space=pltpu.SMEM),
                  pl.BlockSpec((BLOCK, 2 * attn_dim), lambda i: (i, qz_blk)),
                  pl.BlockSpec((BLOCK, 2 * kv_dim), lambda i: (prev(i), kv_blk)),
                  pl.BlockSpec((BLOCK, 2 * kv_dim), lambda i: (i, kv_blk)),
                  pl.BlockSpec((1, n_kv, 2 * BLOCK, GROUP * BLOCK),
                               lambda i: (jnp.minimum(i, 1), 0, 0, 0))],
        out_specs=pl.BlockSpec((BLOCK, attn_dim), lambda i: (i, 0)),
        out_shape=jax.ShapeDtypeStruct((s, attn_dim), BF16),
        compiler_params=_params("arbitrary"),
        name="swa_attention",
    )(sinks, qz, kv, kv, bias)


def _trunk(x, c, g_a, w_mod_a, b_mod_a, w_a_in, conv_a, w_a_out, g_kv, w_mod_kv, b_mod_kv,
           w_kv, rel_bias, g_b, w_mod_b, b_mod_b, w_b_in, sinks_b, w_b_out, g_final):
    s, d = x.shape
    n_a, n_b = w_a_in.shape[0], w_b_in.shape[0]
    assert n_b >= 1 and s % BLOCK == 0
    row = lambda v: v.reshape(1, -1)
    cb = jnp.broadcast_to(c.reshape(d, 1), (d, LANES))

    side_operands = (w_a_out[-1], w_b_in[0], w_kv, w_b_out[0], w_mod_kv, w_mod_b[0]) if n_a else ()
    stream = n_a > 0 and all(_side_rows(s, conv_a.shape[2], a.shape[0]) for a in side_operands)
    w_a_out_bf = w_cat = w_b_out_bf = mod_kv = mod_b = None
    for i in range(n_a):
        shift, scale, gate = jnp.split(_adaln(cb, w_mod_a[i], b_mod_a[i]), 3, axis=-1)
        args = (x, row(g_a[i]), shift, scale, w_a_in[i].astype(BF16), conv_a[i])
        if stream and i == n_a - 1:
            y, (w_a_out_bf, w_cat, w_b_out_bf), (mod_kv, mod_b) = _conv_in(
                *args, cast_jobs=((w_a_out[i],), (w_b_in[0], w_kv), (w_b_out[0],)),
                matvec_jobs=((w_mod_kv, b_mod_kv), (w_mod_b[0], b_mod_b[0])), cb=cb)
        else:
            y, _, _ = _conv_in(*args)
            w_a_out_bf = w_a_out[i].astype(BF16)
        x = _out_proj(y, w_a_out_bf, x, gate)

    if mod_kv is None:
        mod_kv = _adaln(cb, w_mod_kv, b_mod_kv)
    shift_kv, scale_kv = jnp.split(mod_kv, 2, axis=-1)
    bias = _bias_table(rel_bias)
    attn_dim = w_b_in.shape[2] // 2
    kv_dim = w_kv.shape[1] // 2
    kv = None
    for i in range(n_b):
        if i > 0 or mod_b is None:
            mod_b = _adaln(cb, w_mod_b[i], b_mod_b[i])
        shift, scale, gate = jnp.split(mod_b, 3, axis=-1)
        q_scale = dict(scaled_cols=attn_dim, col_scale=LOG2E * HEAD_DIM ** -0.5,
                       silu_cols=(attn_dim, 2 * attn_dim))
        if i == 0:
            if w_cat is None:
                w_cat = jnp.concatenate([w_b_in[i], w_kv], axis=1).astype(BF16)
            qz = _norm_proj(x, jnp.stack([g_b[i], g_kv]), jnp.concatenate([shift, shift_kv]),
                            jnp.concatenate([scale, scale_kv]), w_cat,
                            tail_cols=2 * kv_dim, **q_scale)
            kv, kv_blk = qz, attn_dim // kv_dim
        else:
            qz = _norm_proj(x, row(g_b[i]), shift, scale, w_b_in[i].astype(BF16), **q_scale)
        a = _attention(qz, 0, kv, kv_blk, bias, sinks_b[i])
        w_out = w_b_out_bf if i == 0 and w_b_out_bf is not None else w_b_out[i].astype(BF16)
        if i + 1 < n_b:
            x = _out_proj(a, w_out, x, gate)
        else:
            x = _out_proj_norm(a, w_out, x, gate, row(g_final))
    return x


def kernel(x, c, g_a, w_mod_a, b_mod_a, w_a_in, conv_a, w_a_out, g_kv, w_mod_kv, b_mod_kv, w_kv,
           rel_bias, g_b, w_mod_b, b_mod_b, w_b_in, sinks_b, w_b_out, g_final):
    outs = [_trunk(x[b], c[b], g_a, w_mod_a, b_mod_a, w_a_in, conv_a, w_a_out, g_kv, w_mod_kv,
                   b_mod_kv, w_kv, rel_bias, g_b, w_mod_b, b_mod_b, w_b_in, sinks_b, w_b_out,
                   g_final) for b in range(x.shape[0])]
    return jnp.stack(outs, axis=0)
```

```python
import functools
import math

import jax
import jax.numpy as jnp
from jax import lax
from jax.experimental import pallas as pl
from jax.experimental.pallas import tpu as pltpu

HEAD_DIM = 64
GROUP = 8
BLOCK = 128
N_BUCKETS = 32
MAX_DISTANCE = 128
CONV_WIDTH = 3
EPS = 1e-6
LOG2E = math.log2(math.e)

LANES = 128
SUBLANES = 8
MIB = 1024 * 1024
VMEM_LIMIT = 56 * MIB

F32 = jnp.float32
BF16 = jnp.bfloat16
NORM_ROWS = 128


def _params(*semantics, vmem=VMEM_LIMIT, flags=None):
    return pltpu.CompilerParams(dimension_semantics=semantics, vmem_limit_bytes=vmem, flags=flags)


def _silu(v):
    return v * jax.nn.sigmoid(v)


def _adaln_kernel(cb_ref, w_ref, b_ref, o_ref):
    ca = _silu(cb_ref[...])
    for g in range(o_ref.shape[1] // LANES):
        cols = slice(g * LANES, (g + 1) * LANES)
        o_ref[:, cols] = jnp.sum(w_ref[:, cols] * ca, axis=0, keepdims=True) + b_ref[:, cols]


def _adaln(cb, w, b, tn=1024):
    k, n = w.shape
    return pl.pallas_call(
        _adaln_kernel,
        grid=(n // tn,),
        in_specs=[
            pl.BlockSpec((k, LANES), lambda j: (0, 0)),
            pl.BlockSpec((k, tn), lambda j: (0, j)),
            pl.BlockSpec((1, tn), lambda j: (0, j)),
        ],
        out_specs=pl.BlockSpec((1, tn), lambda j: (0, j)),
        out_shape=jax.ShapeDtypeStruct((1, n), F32),
        compiler_params=_params("arbitrary"),
        name="adaln_matvec",
    )(cb, w, b.reshape(1, n))


def _row_chunk(r):
    return pl.ds(pl.multiple_of(r * NORM_ROWS, NORM_ROWS), NORM_ROWS)


def _col_groups(d):
    return [slice(k * LANES, (k + 1) * LANES) for k in range(d // LANES)]


def _rms_stats(x_ref, rstd_scr):
    tm, d = x_ref.shape

    def body(r, carry):
        rows = _row_chunk(r)
        acc = jnp.zeros((NORM_ROWS, LANES), F32)
        for cols in _col_groups(d):
            xk = x_ref[rows, cols]
            acc = acc + xk * xk
        ms = jnp.sum(acc, axis=-1, keepdims=True) * (1.0 / d)
        rstd_scr[rows, :] = jnp.broadcast_to(lax.rsqrt(ms + EPS), (NORM_ROWS, LANES))
        return carry

    lax.fori_loop(0, tm // NORM_ROWS, body, 0)


def _modnorm_apply(x_ref, rstd_scr, g_ref, sh_ref, sc_ref, mod, h_scr):
    tm, d = x_ref.shape
    vec = slice(mod, mod + 1)

    def body(r, carry):
        rows = _row_chunk(r)
        rstd = rstd_scr[rows, :]
        for cols in _col_groups(d):
            gain = g_ref[vec, cols] * (1.0 + sc_ref[vec, cols])
            h_scr[rows, cols] = ((x_ref[rows, cols] * rstd) * gain
                                 + sh_ref[vec, cols]).astype(h_scr.dtype)
        return carry

    lax.fori_loop(0, tm // NORM_ROWS, body, 0)


def _row_tile_copy(x_hbm, x_buf, sem, i):
    tm = x_buf.shape[0]
    return pltpu.make_async_copy(x_hbm.at[pl.ds(i * tm, tm), :], x_buf, sem)


def _await_row_tile(x_hbm, x_buf, sem, i):
    @pl.when(i == 0)
    def _():
        _row_tile_copy(x_hbm, x_buf, sem, 0).start()

    _row_tile_copy(x_hbm, x_buf, sem, i).wait()


def _prefetch_next_row_tile(x_hbm, x_buf, sem, i):
    @pl.when(i + 1 < pl.num_programs(0))
    def _():
        _row_tile_copy(x_hbm, x_buf, sem, i + 1).start()


SIDE_COLS = 2048
ROW_SPLIT = 2


def _cast_cols(in_ref, o_ref, c0, o0):
    n = min(SIDE_COLS, in_ref.shape[1] - c0)
    o_ref[:, o0:o0 + n] = in_ref[:, c0:c0 + n].astype(o_ref.dtype)


def _matvec_cols(cb_ref, w_ref, acc_ref, c0):
    ca = _silu(cb_ref[...])
    for k in range(c0 // LANES, min(c0 + SIDE_COLS, w_ref.shape[1]) // LANES):
        cols = slice(k * LANES, (k + 1) * LANES)
        prod = w_ref[:, cols] * ca
        part = prod[0:SUBLANES]
        for q in range(1, prod.shape[0] // SUBLANES):
            part = part + prod[q * SUBLANES:(q + 1) * SUBLANES]
        acc_ref[:, cols] += part


def _conv_in_kernel(*refs, cast_widths, n_matvec):
    refs = list(refs)
    take = lambda n: [refs.pop(0) for _ in range(n)]
    x_hbm, g_ref, sh_ref, sc_ref, wb_ref, wc_ref, wu_ref, wz_ref, cw_ref = take(9)
    cast_in = [take(len(widths)) for widths in cast_widths]
    cb_ref = take(1)[0] if n_matvec else None
    mv_in = [take(2) for _ in range(n_matvec)]
    y_ref = take(1)[0]
    cast_out = take(len(cast_widths))
    mv_out = take(n_matvec)
    x_buf, x_sem, h_scr, rstd_scr, halo_scr = take(5)
    mv_acc = take(n_matvec)

    i = pl.program_id(0)
    j = pl.program_id(1)
    tm = y_ref.shape[0]
    step = i * pl.num_programs(1) + j
    last_step = pl.num_programs(0) * pl.num_programs(1) - 1

    @pl.when(step == 0)
    def _():
        for acc_ref in mv_acc:
            acc_ref[...] = jnp.zeros(acc_ref.shape, F32)

    @pl.when(j == 0)
    def _():
        _await_row_tile(x_hbm, x_buf, x_sem, i)
        _rms_stats(x_buf, rstd_scr)
        _modnorm_apply(x_buf, rstd_scr, g_ref, sh_ref, sc_ref, 0, h_scr)
        _prefetch_next_row_tile(x_hbm, x_buf, x_sem, i)

    @pl.when(i == 0)
    def _():
        halo_scr[j] = jnp.zeros(halo_scr.shape[1:], F32)

    side_jobs = []
    for ins, out in zip(cast_in, cast_out):
        col = 0
        for r in ins:
            for c0 in range(0, r.shape[1], SIDE_COLS):
                side_jobs.append(functools.partial(_cast_cols, r, out, c0, col + c0))
            col += r.shape[1]
    for (w_ref, _), acc_ref in zip(mv_in, mv_acc):
        for c0 in range(0, w_ref.shape[1], SIDE_COLS):
            side_jobs.append(functools.partial(_matvec_cols, cb_ref, w_ref, acc_ref, c0))
    proj = [[None] * ROW_SPLIT for _ in range(4)]
    n_gaps = 4 * ROW_SPLIT - 1
    rows_per = tm // ROW_SPLIT
    for n, (half, k) in enumerate((half, k) for half in range(ROW_SPLIT) for k in range(4)):
        w_ref = (wc_ref, wu_ref, wb_ref, wz_ref)[k]
        rows = slice(half * rows_per, (half + 1) * rows_per)
        proj[k][half] = jnp.dot(h_scr[rows, :], w_ref[...], preferred_element_type=F32)
        if n < n_gaps:
            for job in side_jobs[n::n_gaps]:
                job()
    c_proj, u_proj, b_proj, z_proj = [jnp.concatenate(p, axis=0) for p in proj]

    cu = c_proj * u_proj
    w0 = cw_ref[0:1, :]
    w1 = cw_ref[1:2, :]
    w2 = cw_ref[2:3, :]
    conv = w0 * pltpu.roll(cu, 2, axis=0) + w1 * pltpu.roll(cu, 1, axis=0) + w2 * cu
    gate = b_proj * _silu(z_proj)
    y_ref[...] = (gate * conv).astype(y_ref.dtype)

    prev = halo_scr[j]
    top = cu[0:SUBLANES]
    row = lax.broadcasted_iota(jnp.int32, top.shape, 0)
    back1 = jnp.where(row < 1, pltpu.roll(prev, 1, axis=0), pltpu.roll(top, 1, axis=0))
    back2 = jnp.where(row < 2, pltpu.roll(prev, 2, axis=0), pltpu.roll(top, 2, axis=0))
    conv_top = w0 * back2 + w1 * back1 + w2 * top
    y_ref[0:SUBLANES, :] = (gate[0:SUBLANES] * conv_top).astype(y_ref.dtype)
    halo_scr[j] = cu[tm - SUBLANES:tm]

    if n_matvec:
        @pl.when(step == last_step)
        def _():
            for (_, b_ref), o_ref, acc_ref in zip(mv_in, mv_out, mv_acc):
                o_ref[...] = jnp.sum(acc_ref[...], axis=0, keepdims=True) + b_ref[...]


CONV_TM, CONV_TN = 1024, 256
BF16_ROWS = 16


def _side_rows(s, c, k):
    steps = (s // CONV_TM) * (c // CONV_TN)
    rows = k // steps
    return rows if rows * steps == k and rows % BF16_ROWS == 0 else 0


def _conv_in(x, g, shift, scale, w_in, conv_w, cast_jobs=(), matvec_jobs=(), cb=None):
    tm, tn = CONV_TM, CONV_TN
    s, d = x.shape
    c = conv_w.shape[1]
    nj = c // tn
    vec = pl.BlockSpec((1, d), lambda i, j: (0, 0))
    w_specs = [pl.BlockSpec((d, tn), lambda i, j, k=k: (0, k * nj + j)) for k in range(4)]
    row_block = lambda rows, n: pl.BlockSpec((rows, n), lambda i, j: (i * nj + j, 0))
    whole = lambda n: pl.BlockSpec((1, n), lambda i, j: (0, 0))

    side_in, side_in_specs = [], []
    out_shapes = [jax.ShapeDtypeStruct((s, c), BF16)]
    out_specs = [pl.BlockSpec((tm, tn), lambda i, j: (i, j))]
    scratch = [pltpu.VMEM((tm, d), F32), pltpu.SemaphoreType.DMA(()),
               pltpu.VMEM((tm, d), BF16), pltpu.VMEM((tm, LANES), F32),
               pltpu.VMEM((nj, SUBLANES, tn), F32)]
    for job in cast_jobs:
        k = job[0].shape[0]
        rows = _side_rows(s, c, k)
        assert rows and all(a.shape[0] == k and a.shape[1] % LANES == 0 for a in job)
        side_in += list(job)
        side_in_specs += [row_block(rows, a.shape[1]) for a in job]
        n = sum(a.shape[1] for a in job)
        out_shapes.append(jax.ShapeDtypeStruct((k, n), BF16))
        out_specs.append(row_block(rows, n))
    if matvec_jobs:
        k = cb.shape[0]
        rows = _side_rows(s, c, k)
        assert rows
        side_in.append(cb)
        side_in_specs.append(row_block(rows, LANES))
    for w, b in matvec_jobs:
        n = w.shape[1]
        assert w.shape[0] == k and n % LANES == 0
        side_in += [w, b.reshape(1, n)]
        side_in_specs += [row_block(rows, n), whole(n)]
        out_shapes.append(jax.ShapeDtypeStruct((1, n), F32))
        out_specs.append(whole(n))
        scratch.append(pltpu.VMEM((SUBLANES, n), F32))

    body = functools.partial(_conv_in_kernel, cast_widths=tuple(tuple(a.shape[1] for a in job)
                                                                for job in cast_jobs),
                             n_matvec=len(matvec_jobs))
    outs = pl.pallas_call(
        body,
        grid=(s // tm, nj),
        in_specs=[pl.BlockSpec(memory_space=pl.ANY), vec, vec, vec, *w_specs,
                  pl.BlockSpec((CONV_WIDTH, tn), lambda i, j: (0, j)), *side_in_specs],
        out_specs=out_specs,
        out_shape=out_shapes,
        scratch_shapes=scratch,
        compiler_params=_params("arbitrary", "arbitrary"),
        name="conv_in_proj",
    )(x, g, shift, scale, w_in, w_in, w_in, w_in, conv_w, *side_in)
    n_cast = len(cast_jobs)
    return outs[0], list(outs[1:1 + n_cast]), list(outs[1 + n_cast:])


def _out_proj_kernel(y_ref, w_ref, x_ref, gate_ref, o_ref):
    acc = jnp.dot(y_ref[...], w_ref[...], preferred_element_type=F32)
    o_ref[...] = x_ref[...] + gate_ref[...] * acc


def _out_proj(y, w, x, gate, tm=1024, tn=1024):
    s, k = y.shape
    d = w.shape[1]
    return pl.pallas_call(
        _out_proj_kernel,
        grid=(s // tm, d // tn),
        in_specs=[pl.BlockSpec((tm, k), lambda i, j: (i, 0)),
                  pl.BlockSpec((k, tn), lambda i, j: (0, j)),
                  pl.BlockSpec((tm, tn), lambda i, j: (i, j)),
                  pl.BlockSpec((1, tn), lambda i, j: (0, j))],
        out_specs=pl.BlockSpec((tm, tn), lambda i, j: (i, j)),
        out_shape=jax.ShapeDtypeStruct((s, d), F32),
        compiler_params=_params("arbitrary", "arbitrary"),
        name="out_proj",
    )(y, w, x, gate)


def _out_proj_norm_kernel(y_ref, w_ref, x_ref, gate_ref, gf_ref, o_ref):
    acc = jnp.dot(y_ref[...], w_ref[...], preferred_element_type=F32)
    x2 = x_ref[...] + gate_ref[...] * acc
    ms = jnp.mean(x2 * x2, axis=-1, keepdims=True)
    o_ref[...] = (x2 * lax.rsqrt(ms + EPS)) * gf_ref[...]


def _out_proj_norm(y, w, x, gate, g_final, tm=256):
    s, k = y.shape
    d = w.shape[1]
    vec = pl.BlockSpec((1, d), lambda i: (0, 0))
    return pl.pallas_call(
        _out_proj_norm_kernel,
        grid=(s // tm,),
        in_specs=[pl.BlockSpec((tm, k), lambda i: (i, 0)),
                  pl.BlockSpec((k, d), lambda i: (0, 0), pipeline_mode=pl.Buffered(1)),
                  pl.BlockSpec((tm, d), lambda i: (i, 0)), vec, vec],
        out_specs=pl.BlockSpec((tm, d), lambda i: (i, 0)),
        out_shape=jax.ShapeDtypeStruct((s, d), F32),
        compiler_params=_params("arbitrary", vmem=60 * MIB),
        name="out_proj_final_norm",
    )(y, w, x, gate, g_final)


def _norm_proj_kernel(x_hbm, g_ref, sh_ref, sc_ref, w_ref, o_ref, x_buf, x_sem, h_scr, rstd_scr,
                      *, tail_tiles, scaled_tiles, scale, silu_tiles):
    i = pl.program_id(0)
    j = pl.program_id(1)

    @pl.when(j == 0)
    def _():
        _await_row_tile(x_hbm, x_buf, x_sem, i)
        _rms_stats(x_buf, rstd_scr)
        _modnorm_apply(x_buf, rstd_scr, g_ref, sh_ref, sc_ref, 1 if tail_tiles else 0, h_scr)
        if not tail_tiles:
            _prefetch_next_row_tile(x_hbm, x_buf, x_sem, i)

    if tail_tiles:
        @pl.when(j == tail_tiles)
        def _():
            _modnorm_apply(x_buf, rstd_scr, g_ref, sh_ref, sc_ref, 0, h_scr)
            _prefetch_next_row_tile(x_hbm, x_buf, x_sem, i)

    gated = (j >= tail_tiles + silu_tiles[0]) & (j < tail_tiles + silu_tiles[1])

    @pl.when(gated)
    def _():
        acc = jnp.dot(h_scr[...], w_ref[...], preferred_element_type=F32)
        o_ref[...] = _silu(acc).astype(o_ref.dtype)

    @pl.when(jnp.logical_not(gated))
    def _():
        acc = jnp.dot(h_scr[...], w_ref[...], preferred_element_type=F32)
        if scaled_tiles:
            scaled = (j >= tail_tiles) & (j < tail_tiles + scaled_tiles)
            acc = acc * jnp.where(scaled, scale, 1.0)
        o_ref[...] = acc.astype(o_ref.dtype)


def _norm_proj(x, g, shift, scale, w, tail_cols=0, scaled_cols=0, col_scale=1.0,
               silu_cols=(0, 0), tm=1024, tn=1024):
    s, d = x.shape
    n = w.shape[1]
    n_mod = g.shape[0]
    assert scaled_cols % tn == 0 and tail_cols % tn == 0 and n_mod == (2 if tail_cols else 1)
    assert all(c % tn == 0 and scaled_cols <= c <= n - tail_cols for c in silu_cols)
    n_tiles, tail_tiles = n // tn, tail_cols // tn
    col_tile = lambda j: (j + n_tiles - tail_tiles) % n_tiles
    vec = pl.BlockSpec((n_mod, d), lambda i, j: (0, 0))
    body = functools.partial(_norm_proj_kernel, tail_tiles=tail_tiles,
                             scaled_tiles=scaled_cols // tn, scale=col_scale,
                             silu_tiles=tuple(c // tn for c in silu_cols))
    return pl.pallas_call(
        body,
        grid=(s // tm, n_tiles),
        in_specs=[pl.BlockSpec(memory_space=pl.ANY), vec, vec, vec,
                  pl.BlockSpec((d, tn), lambda i, j: (0, col_tile(j)))],
        out_specs=pl.BlockSpec((tm, tn), lambda i, j: (i, col_tile(j))),
        out_shape=jax.ShapeDtypeStruct((s, n), BF16),
        scratch_shapes=[pltpu.VMEM((tm, d), F32), pltpu.SemaphoreType.DMA(()),
                        pltpu.VMEM((tm, d), BF16), pltpu.VMEM((tm, LANES), F32)],
        compiler_params=_params("arbitrary", "arbitrary"),
        name="norm_proj",
    )(x, g, shift, scale, w)


def _t5_bucket(dist):
    max_exact = N_BUCKETS // 2
    d = jnp.maximum(dist, 0)
    d_f = jnp.maximum(d, 1).astype(F32)
    large = max_exact + (jnp.log(d_f / max_exact) / math.log(MAX_DISTANCE / max_exact)
                         * (N_BUCKETS - max_exact)).astype(jnp.int32)
    large = jnp.minimum(large, N_BUCKETS - 1)
    return jnp.where(d < max_exact, d, large)


def _band_buckets():
    b = jnp.arange(2 * BLOCK)[:, None]
    a = jnp.arange(BLOCK)[None, :]
    dist = a + BLOCK - b
    band = (dist >= 0) & (dist < BLOCK)
    later = jnp.where(band, _t5_bucket(dist), -1)
    first = jnp.where(b >= BLOCK, later, -1)
    return jnp.stack([first, later]).astype(jnp.int32)


HEAD_ORDER = tuple(range(0, GROUP, 2)) + tuple(range(1, GROUP, 2))


def _bias_table_kernel(rb_ref, bucket_ref, o_ref):
    g = pl.program_id(1)
    bucket = bucket_ref[0]
    for c, hh in enumerate(HEAD_ORDER):
        acc = jnp.full(bucket.shape, -jnp.inf, F32)
        for k in range(N_BUCKETS):
            acc = jnp.where(bucket == k, rb_ref[k, g * GROUP + hh] * LOG2E, acc)
        o_ref[0, 0, :, c * BLOCK:(c + 1) * BLOCK] = acc


def _bias_table(rel_bias):
    n_kv = rel_bias.shape[1] // GROUP
    return pl.pallas_call(
        _bias_table_kernel,
        grid=(2, n_kv),
        in_specs=[pl.BlockSpec(memory_space=pltpu.SMEM),
                  pl.BlockSpec((1, 2 * BLOCK, BLOCK), lambda v, g: (v, 0, 0))],
        out_specs=pl.BlockSpec((1, 1, 2 * BLOCK, GROUP * BLOCK), lambda v, g: (v, g, 0, 0)),
        out_shape=jax.ShapeDtypeStruct((2, n_kv, 2 * BLOCK, GROUP * BLOCK), F32),
        compiler_params=_params("arbitrary", "arbitrary"),
        name="bias_table",
    )(rel_bias, _band_buckets())


ATTN_BLOCKS = 2


def _attn_kernel(sink_ref, qz_ref, kvp_ref, kvc_ref, bias_ref, o_ref):
    n_kv = bias_ref.shape[1]
    attn_dim, kv_dim = o_ref.shape[1], n_kv * HEAD_DIM
    for blk in range(ATTN_BLOCKS):
        rows = slice(blk * BLOCK, (blk + 1) * BLOCK)
        kv_prev = kvp_ref if blk == 0 else kvc_ref.at[(blk - 1) * BLOCK:blk * BLOCK, :]
        kv_cur = kvc_ref.at[rows, :]
        entry = jnp.where(pl.program_id(0) == 0, 0, 1) if blk == 0 else 1
        _attn_block(sink_ref, qz_ref.at[rows, :attn_dim], qz_ref.at[rows, attn_dim:],
                    kv_prev.at[:, :kv_dim], kv_cur.at[:, :kv_dim],
                    kv_prev.at[:, kv_dim:], kv_cur.at[:, kv_dim:],
                    bias_ref.at[entry], o_ref.at[rows, :])


def _attn_block(sink_ref, q_ref, z_ref, kp_ref, kc_ref, vp_ref, vc_ref, bias_ref, o_ref):
    n_kv = bias_ref.shape[0]
    ones_rows = jnp.ones((BF16_ROWS, 2 * BLOCK), BF16)
    low_half = lax.broadcasted_iota(jnp.int32, (2 * BLOCK, LANES), 1) < HEAD_DIM
    nt_dot = functools.partial(lax.dot_general, dimension_numbers=(((1,), (1,)), ((), ())),
                               preferred_element_type=F32)
    tiles = GROUP // 2
    for g in range(n_kv):
        heads = [g * GROUP + hh for hh in HEAD_ORDER]
        kv_cols = slice((g // 2) * LANES, (g // 2 + 1) * LANES)
        k2 = jnp.concatenate([kp_ref[:, kv_cols], kc_ref[:, kv_cols]], axis=0)
        v2_t = jnp.concatenate([vp_ref[:, kv_cols].T, vc_ref[:, kv_cols].T], axis=1)
        v2_t = jnp.concatenate([v2_t, ones_rows], axis=0)
        k_other = pltpu.roll(k2, HEAD_DIM, axis=1)
        k_low, k_high = (k2, k_other) if g % 2 == 0 else (k_other, k2)
        k_low = jnp.where(low_half, k_low, jnp.zeros_like(k_low))
        k_high = jnp.where(low_half, jnp.zeros_like(k_high), k_high)
        q_rows = jnp.concatenate([q_ref[:, (g * tiles + t) * LANES:(g * tiles + t + 1) * LANES]
                                  for t in range(tiles)], axis=0)
        s_t = jnp.concatenate([nt_dot(k_low, q_rows), nt_dot(k_high, q_rows)], axis=1)
        s_t = s_t + bias_ref[g]
        sink = jnp.concatenate([jnp.full((1, BLOCK), sink_ref[h] * LOG2E, F32) for h in heads],
                               axis=1)
        m = jnp.maximum(jnp.max(s_t, axis=0, keepdims=True), sink)
        p = jnp.exp2(s_t - m).astype(BF16)
        o2_t = jnp.dot(v2_t, p, preferred_element_type=F32)
        den = o2_t[2 * HEAD_DIM:2 * HEAD_DIM + 1] + jnp.exp2(sink - m)
        o_t = (o2_t[:HEAD_DIM] if g % 2 == 0 else o2_t[HEAD_DIM:2 * HEAD_DIM]) / den
        o_t = o_t.astype(o_ref.dtype)
        for t in range(tiles):
            cols = slice((g * tiles + t) * LANES, (g * tiles + t + 1) * LANES)
            both = jnp.concatenate([o_t[:, t * BLOCK:(t + 1) * BLOCK],
                                    o_t[:, (tiles + t) * BLOCK:(tiles + t + 1) * BLOCK]], axis=0)
            o_ref[:, cols] = both.T * z_ref[:, cols]


def _attention(qz, qz_blk, kv, kv_blk, bias, sinks):
    s = qz.shape[0]
    n_kv = bias.shape[1]
    kv_dim = n_kv * HEAD_DIM
    attn_dim = kv_dim * GROUP
    assert n_kv % 2 == 0 and GROUP % 2 == 0 and 2 * HEAD_DIM == LANES and BLOCK == LANES
    rows = ATTN_BLOCKS * BLOCK
    assert s % rows == 0
    before = lambda i: jnp.maximum(i * ATTN_BLOCKS - 1, 0)
    return pl.pallas_call(
        _attn_kernel,
        grid=(s // rows,),
        in_specs=[pl.BlockSpec(memory_space=pltpu.SMEM),
                  pl.BlockSpec((rows, 2 * attn_dim), lambda i: (i, qz_blk)),
                  pl.BlockSpec((BLOCK, 2 * kv_dim), lambda i: (before(i), kv_blk)),
                  pl.BlockSpec((rows, 2 * kv_dim), lambda i: (i, kv_blk)),
                  pl.BlockSpec(bias.shape, lambda i: (0, 0, 0, 0),
                               pipeline_mode=pl.Buffered(1))],
        out_specs=pl.BlockSpec((rows, attn_dim), lambda i: (i, 0)),
        out_shape=jax.ShapeDtypeStruct((s, attn_dim), BF16),
        compiler_params=_params("arbitrary"),
        name="swa_attention",
    )(sinks, qz, kv, kv, bias)


def _trunk(x, c, g_a, w_mod_a, b_mod_a, w_a_in, conv_a, w_a_out, g_kv, w_mod_kv, b_mod_kv,
           w_kv, rel_bias, g_b, w_mod_b, b_mod_b, w_b_in, sinks_b, w_b_out, g_final):
    s, d = x.shape
    n_a, n_b = w_a_in.shape[0], w_b_in.shape[0]
    assert n_b >= 1 and s % BLOCK == 0
    row = lambda v: v.reshape(1, -1)
    cb = jnp.broadcast_to(c.reshape(d, 1), (d, LANES))

    side_operands = (w_a_out[-1], w_b_in[0], w_kv, w_b_out[0], w_mod_kv, w_mod_b[0]) if n_a else ()
    stream = n_a > 0 and all(_side_rows(s, conv_a.shape[2], a.shape[0]) for a in side_operands)
    w_a_out_bf = w_cat = w_b_out_bf = mod_kv = mod_b = None
    for i in range(n_a):
        shift, scale, gate = jnp.split(_adaln(cb, w_mod_a[i], b_mod_a[i]), 3, axis=-1)
        args = (x, row(g_a[i]), shift, scale, w_a_in[i].astype(BF16), conv_a[i])
        if stream and i == n_a - 1:
            y, (w_a_out_bf, w_cat, w_b_out_bf), (mod_kv, mod_b) = _conv_in(
                *args, cast_jobs=((w_a_out[i],), (w_b_in[0], w_kv), (w_b_out[0],)),
                matvec_jobs=((w_mod_kv, b_mod_kv), (w_mod_b[0], b_mod_b[0])), cb=cb)
        else:
            y, _, _ = _conv_in(*args)
            w_a_out_bf = w_a_out[i].astype(BF16)
        x = _out_proj(y, w_a_out_bf, x, gate)

    if mod_kv is None:
        mod_kv = _adaln(cb, w_mod_kv, b_mod_kv)
    shift_kv, scale_kv = jnp.split(mod_kv, 2, axis=-1)
    bias = _bias_table(rel_bias)
    attn_dim = w_b_in.shape[2] // 2
    kv_dim = w_kv.shape[1] // 2
    kv = None
    for i in range(n_b):
        if i > 0 or mod_b is None:
            mod_b = _adaln(cb, w_mod_b[i], b_mod_b[i])
        shift, scale, gate = jnp.split(mod_b, 3, axis=-1)
        q_scale = dict(scaled_cols=attn_dim, col_scale=LOG2E * HEAD_DIM ** -0.5,
                       silu_cols=(attn_dim, 2 * attn_dim))
        if i == 0:
            if w_cat is None:
                w_cat = jnp.concatenate([w_b_in[i], w_kv], axis=1).astype(BF16)
            qz = _norm_proj(x, jnp.stack([g_b[i], g_kv]), jnp.concatenate([shift, shift_kv]),
                            jnp.concatenate([scale, scale_kv]), w_cat,
                            tail_cols=2 * kv_dim, **q_scale)
            kv, kv_blk = qz, attn_dim // kv_dim
        else:
            qz = _norm_proj(x, row(g_b[i]), shift, scale, w_b_in[i].astype(BF16), **q_scale)
        a = _attention(qz, 0, kv, kv_blk, bias, sinks_b[i])
        w_out = w_b_out_bf if i == 0 and w_b_out_bf is not None else w_b_out[i].astype(BF16)
        if i + 1 < n_b:
            x = _out_proj(a, w_out, x, gate)
        else:
            x = _out_proj_norm(a, w_out, x, gate, row(g_final))
    return x


def kernel(x, c, g_a, w_mod_a, b_mod_a, w_a_in, conv_a, w_a_out, g_kv, w_mod_kv, b_mod_kv, w_kv,
           rel_bias, g_b, w_mod_b, b_mod_b, w_b_in, sinks_b, w_b_out, g_final):
    outs = [_trunk(x[b], c[b], g_a, w_mod_a, b_mod_a, w_a_in, conv_a, w_a_out, g_kv, w_mod_kv,
                   b_mod_kv, w_kv, rel_bias, g_b, w_mod_b, b_mod_b, w_b_in, sinks_b, w_b_out,
                   g_final) for b in range(x.shape[0])]
    return jnp.stack(outs, axis=0)
```

```python
import functools
import math

import jax
import jax.numpy as jnp
from jax import lax
from jax.experimental import pallas as pl
from jax.experimental.pallas import tpu as pltpu

HEAD_DIM = 64
GROUP = 8
BLOCK = 128
N_BUCKETS = 32
MAX_DISTANCE = 128
CONV_WIDTH = 3
EPS = 1e-6
LOG2E = math.log2(math.e)

LANES = 128
SUBLANES = 8
MIB = 1024 * 1024
VMEM_LIMIT = 56 * MIB

F32 = jnp.float32
BF16 = jnp.bfloat16
STATS_ROWS = 512
NORM_ROWS = 256


def _params(*semantics, vmem=VMEM_LIMIT, flags=None):
    return pltpu.CompilerParams(dimension_semantics=semantics, vmem_limit_bytes=vmem, flags=flags)


def _silu(v):
    return v * jax.nn.sigmoid(v)


def _adaln_kernel(cb_ref, w_ref, b_ref, o_ref):
    ca = _silu(cb_ref[...])
    for g in range(o_ref.shape[1] // LANES):
        cols = slice(g * LANES, (g + 1) * LANES)
        o_ref[:, cols] = jnp.sum(w_ref[:, cols] * ca, axis=0, keepdims=True) + b_ref[:, cols]


def _adaln(cb, w, b, tn=1024):
    k, n = w.shape
    return pl.pallas_call(
        _adaln_kernel,
        grid=(n // tn,),
        in_specs=[
            pl.BlockSpec((k, LANES), lambda j: (0, 0)),
            pl.BlockSpec((k, tn), lambda j: (0, j)),
            pl.BlockSpec((1, tn), lambda j: (0, j)),
        ],
        out_specs=pl.BlockSpec((1, tn), lambda j: (0, j)),
        out_shape=jax.ShapeDtypeStruct((1, n), F32),
        compiler_params=_params("arbitrary"),
        name="adaln_matvec",
    )(cb, w, b.reshape(1, n))


def _row_chunk(r, n):
    return pl.ds(pl.multiple_of(r * n, n), n)


def _col_groups(d):
    return [slice(k * LANES, (k + 1) * LANES) for k in range(d // LANES)]


def _rms_stats(x_ref, rstd_scr):
    tm, d = x_ref.shape

    def body(r, carry):
        rows = _row_chunk(r, STATS_ROWS)
        acc = jnp.zeros((STATS_ROWS, LANES), F32)
        for cols in _col_groups(d):
            xk = x_ref[rows, cols]
            acc = acc + xk * xk
        ms = jnp.sum(acc, axis=-1, keepdims=True) * (1.0 / d)
        rstd_scr[rows, :] = jnp.broadcast_to(lax.rsqrt(ms + EPS), (STATS_ROWS, LANES))
        return carry

    lax.fori_loop(0, tm // STATS_ROWS, body, 0)


def _modnorm_apply(x_ref, rstd_scr, g_ref, sh_ref, sc_ref, mod, h_scr):
    tm, d = x_ref.shape
    vec = slice(mod, mod + 1)

    def body(r, carry):
        rows = _row_chunk(r, NORM_ROWS)
        rstd = rstd_scr[rows, :]
        for cols in _col_groups(d):
            gain = g_ref[vec, cols] * (1.0 + sc_ref[vec, cols])
            h_scr[rows, cols] = ((x_ref[rows, cols] * rstd) * gain
                                 + sh_ref[vec, cols]).astype(h_scr.dtype)
        return carry

    lax.fori_loop(0, tm // NORM_ROWS, body, 0)


def _row_tile_copy(x_hbm, x_buf, sem, i):
    tm = x_buf.shape[0]
    return pltpu.make_async_copy(x_hbm.at[pl.ds(i * tm, tm), :], x_buf, sem)


def _await_row_tile(x_hbm, x_buf, sem, i):
    @pl.when(i == 0)
    def _():
        _row_tile_copy(x_hbm, x_buf, sem, 0).start()

    _row_tile_copy(x_hbm, x_buf, sem, i).wait()


def _prefetch_next_row_tile(x_hbm, x_buf, sem, i):
    @pl.when(i + 1 < pl.num_programs(0))
    def _():
        _row_tile_copy(x_hbm, x_buf, sem, i + 1).start()


SIDE_COLS = 2048
ROW_SPLIT = 2


def _cast_cols(in_ref, o_ref, c0, o0):
    n = min(SIDE_COLS, in_ref.shape[1] - c0)
    o_ref[:, o0:o0 + n] = in_ref[:, c0:c0 + n].astype(o_ref.dtype)


def _matvec_cols(cb_ref, w_ref, acc_ref, c0):
    ca = _silu(cb_ref[...])
    for k in range(c0 // LANES, min(c0 + SIDE_COLS, w_ref.shape[1]) // LANES):
        cols = slice(k * LANES, (k + 1) * LANES)
        prod = w_ref[:, cols] * ca
        part = prod[0:SUBLANES]
        for q in range(1, prod.shape[0] // SUBLANES):
            part = part + prod[q * SUBLANES:(q + 1) * SUBLANES]
        acc_ref[:, cols] += part


def _conv_in_kernel(*refs, cast_widths, n_matvec):
    refs = list(refs)
    take = lambda n: [refs.pop(0) for _ in range(n)]
    x_hbm, g_ref, sh_ref, sc_ref, wb_ref, wc_ref, wu_ref, wz_ref, cw_ref = take(9)
    cast_in = [take(len(widths)) for widths in cast_widths]
    cb_ref = take(1)[0] if n_matvec else None
    mv_in = [take(2) for _ in range(n_matvec)]
    y_ref = take(1)[0]
    cast_out = take(len(cast_widths))
    mv_out = take(n_matvec)
    x_buf, x_sem, h_scr, rstd_scr, halo_scr = take(5)
    mv_acc = take(n_matvec)

    i = pl.program_id(0)
    j = pl.program_id(1)
    tm = y_ref.shape[0]
    step = i * pl.num_programs(1) + j
    last_step = pl.num_programs(0) * pl.num_programs(1) - 1

    @pl.when(step == 0)
    def _():
        for acc_ref in mv_acc:
            acc_ref[...] = jnp.zeros(acc_ref.shape, F32)

    @pl.when(j == 0)
    def _():
        _await_row_tile(x_hbm, x_buf, x_sem, i)
        _rms_stats(x_buf, rstd_scr)
        _modnorm_apply(x_buf, rstd_scr, g_ref, sh_ref, sc_ref, 0, h_scr)
        _prefetch_next_row_tile(x_hbm, x_buf, x_sem, i)

    @pl.when(i == 0)
    def _():
        halo_scr[j] = jnp.zeros(halo_scr.shape[1:], F32)

    side_jobs = []
    for ins, out in zip(cast_in, cast_out):
        col = 0
        for r in ins:
            for c0 in range(0, r.shape[1], SIDE_COLS):
                side_jobs.append(functools.partial(_cast_cols, r, out, c0, col + c0))
            col += r.shape[1]
    for (w_ref, _), acc_ref in zip(mv_in, mv_acc):
        for c0 in range(0, w_ref.shape[1], SIDE_COLS):
            side_jobs.append(functools.partial(_matvec_cols, cb_ref, w_ref, acc_ref, c0))
    proj = [[None] * ROW_SPLIT for _ in range(4)]
    n_gaps = 4 * ROW_SPLIT - 1
    rows_per = tm // ROW_SPLIT
    for n, (half, k) in enumerate((half, k) for half in range(ROW_SPLIT) for k in range(4)):
        w_ref = (wc_ref, wu_ref, wb_ref, wz_ref)[k]
        rows = slice(half * rows_per, (half + 1) * rows_per)
        proj[k][half] = jnp.dot(h_scr[rows, :], w_ref[...], preferred_element_type=F32)
        if n < n_gaps:
            for job in side_jobs[n::n_gaps]:
                job()
    c_proj, u_proj, b_proj, z_proj = [jnp.concatenate(p, axis=0) for p in proj]

    cu = c_proj * u_proj
    w0 = cw_ref[0:1, :]
    w1 = cw_ref[1:2, :]
    w2 = cw_ref[2:3, :]
    conv = w0 * pltpu.roll(cu, 2, axis=0) + w1 * pltpu.roll(cu, 1, axis=0) + w2 * cu
    gate = b_proj * _silu(z_proj)
    y_ref[...] = (gate * conv).astype(y_ref.dtype)

    prev = halo_scr[j]
    top = cu[0:SUBLANES]
    row = lax.broadcasted_iota(jnp.int32, top.shape, 0)
    back1 = jnp.where(row < 1, pltpu.roll(prev, 1, axis=0), pltpu.roll(top, 1, axis=0))
    back2 = jnp.where(row < 2, pltpu.roll(prev, 2, axis=0), pltpu.roll(top, 2, axis=0))
    conv_top = w0 * back2 + w1 * back1 + w2 * top
    y_ref[0:SUBLANES, :] = (gate[0:SUBLANES] * conv_top).astype(y_ref.dtype)
    halo_scr[j] = cu[tm - SUBLANES:tm]

    if n_matvec:
        @pl.when(step == last_step)
        def _():
            for (_, b_ref), o_ref, acc_ref in zip(mv_in, mv_out, mv_acc):
                o_ref[...] = jnp.sum(acc_ref[...], axis=0, keepdims=True) + b_ref[...]


CONV_TM, CONV_TN = 1024, 256
BF16_ROWS = 16


def _side_rows(s, c, k):
    steps = (s // CONV_TM) * (c // CONV_TN)
    rows = k // steps
    return rows if rows * steps == k and rows % BF16_ROWS == 0 else 0


def _conv_in(x, g, shift, scale, w_in, conv_w, cast_jobs=(), matvec_jobs=(), cb=None):
    tm, tn = CONV_TM, CONV_TN
    s, d = x.shape
    c = conv_w.shape[1]
    nj = c // tn
    vec = pl.BlockSpec((1, d), lambda i, j: (0, 0))
    w_specs = [pl.BlockSpec((d, tn), lambda i, j, k=k: (0, k * nj + j)) for k in range(4)]
    row_block = lambda rows, n: pl.BlockSpec((rows, n), lambda i, j: (i * nj + j, 0))
    whole = lambda n: pl.BlockSpec((1, n), lambda i, j: (0, 0))

    side_in, side_in_specs = [], []
    out_shapes = [jax.ShapeDtypeStruct((s, c), BF16)]
    out_specs = [pl.BlockSpec((tm, tn), lambda i, j: (i, j))]
    scratch = [pltpu.VMEM((tm, d), F32), pltpu.SemaphoreType.DMA(()),
               pltpu.VMEM((tm, d), BF16), pltpu.VMEM((tm, LANES), F32),
               pltpu.VMEM((nj, SUBLANES, tn), F32)]
    for job in cast_jobs:
        k = job[0].shape[0]
        rows = _side_rows(s, c, k)
        assert rows and all(a.shape[0] == k and a.shape[1] % LANES == 0 for a in job)
        side_in += list(job)
        side_in_specs += [row_block(rows, a.shape[1]) for a in job]
        n = sum(a.shape[1] for a in job)
        out_shapes.append(jax.ShapeDtypeStruct((k, n), BF16))
        out_specs.append(row_block(rows, n))
    if matvec_jobs:
        k = cb.shape[0]
        rows = _side_rows(s, c, k)
        assert rows
        side_in.append(cb)
        side_in_specs.append(row_block(rows, LANES))
    for w, b in matvec_jobs:
        n = w.shape[1]
        assert w.shape[0] == k and n % LANES == 0
        side_in += [w, b.reshape(1, n)]
        side_in_specs += [row_block(rows, n), whole(n)]
        out_shapes.append(jax.ShapeDtypeStruct((1, n), F32))
        out_specs.append(whole(n))
        scratch.append(pltpu.VMEM((SUBLANES, n), F32))

    body = functools.partial(_conv_in_kernel, cast_widths=tuple(tuple(a.shape[1] for a in job)
                                                                for job in cast_jobs),
                             n_matvec=len(matvec_jobs))
    outs = pl.pallas_call(
        body,
        grid=(s // tm, nj),
        in_specs=[pl.BlockSpec(memory_space=pl.ANY), vec, vec, vec, *w_specs,
                  pl.BlockSpec((CONV_WIDTH, tn), lambda i, j: (0, j)), *side_in_specs],
        out_specs=out_specs,
        out_shape=out_shapes,
        scratch_shapes=scratch,
        compiler_params=_params("arbitrary", "arbitrary"),
        name="conv_in_proj",
    )(x, g, shift, scale, w_in, w_in, w_in, w_in, conv_w, *side_in)
    n_cast = len(cast_jobs)
    return outs[0], list(outs[1:1 + n_cast]), list(outs[1 + n_cast:])


def _out_proj_kernel(y_ref, w_ref, x_ref, gate_ref, o_ref):
    acc = jnp.dot(y_ref[...], w_ref[...], preferred_element_type=F32)
    o_ref[...] = x_ref[...] + gate_ref[...] * acc


def _out_proj(y, w, x, gate, tm=1024, tn=1024):
    s, k = y.shape
    d = w.shape[1]
    return pl.pallas_call(
        _out_proj_kernel,
        grid=(s // tm, d // tn),
        in_specs=[pl.BlockSpec((tm, k), lambda i, j: (i, 0)),
                  pl.BlockSpec((k, tn), lambda i, j: (0, j)),
                  pl.BlockSpec((tm, tn), lambda i, j: (i, j)),
                  pl.BlockSpec((1, tn), lambda i, j: (0, j))],
        out_specs=pl.BlockSpec((tm, tn), lambda i, j: (i, j)),
        out_shape=jax.ShapeDtypeStruct((s, d), F32),
        compiler_params=_params("arbitrary", "arbitrary"),
        name="out_proj",
    )(y, w, x, gate)


def _out_proj_norm_kernel(y_ref, w_ref, x_ref, gate_ref, gf_ref, o_ref):
    acc = jnp.dot(y_ref[...], w_ref[...], preferred_element_type=F32)
    x2 = x_ref[...] + gate_ref[...] * acc
    ms = jnp.mean(x2 * x2, axis=-1, keepdims=True)
    o_ref[...] = (x2 * lax.rsqrt(ms + EPS)) * gf_ref[...]


def _out_proj_norm(y, w, x, gate, g_final, tm=256):
    s, k = y.shape
    d = w.shape[1]
    vec = pl.BlockSpec((1, d), lambda i: (0, 0))
    return pl.pallas_call(
        _out_proj_norm_kernel,
        grid=(s // tm,),
        in_specs=[pl.BlockSpec((tm, k), lambda i: (i, 0)),
                  pl.BlockSpec((k, d), lambda i: (0, 0), pipeline_mode=pl.Buffered(1)),
                  pl.BlockSpec((tm, d), lambda i: (i, 0)), vec, vec],
        out_specs=pl.BlockSpec((tm, d), lambda i: (i, 0)),
        out_shape=jax.ShapeDtypeStruct((s, d), F32),
        compiler_params=_params("arbitrary", vmem=60 * MIB),
        name="out_proj_final_norm",
    )(y, w, x, gate, g_final)


def _norm_proj_kernel(x_hbm, g_ref, sh_ref, sc_ref, w_ref, o_ref, x_buf, x_sem, h_scr, rstd_scr,
                      *, tail_tiles, scaled_tiles, scale, silu_tiles):
    i = pl.program_id(0)
    j = pl.program_id(1)

    @pl.when(j == 0)
    def _():
        _await_row_tile(x_hbm, x_buf, x_sem, i)
        _rms_stats(x_buf, rstd_scr)
        _modnorm_apply(x_buf, rstd_scr, g_ref, sh_ref, sc_ref, 1 if tail_tiles else 0, h_scr)
        if not tail_tiles:
            _prefetch_next_row_tile(x_hbm, x_buf, x_sem, i)

    if tail_tiles:
        @pl.when(j == tail_tiles)
        def _():
            _modnorm_apply(x_buf, rstd_scr, g_ref, sh_ref, sc_ref, 0, h_scr)
            _prefetch_next_row_tile(x_hbm, x_buf, x_sem, i)

    gated = (j >= tail_tiles + silu_tiles[0]) & (j < tail_tiles + silu_tiles[1])

    @pl.when(gated)
    def _():
        acc = jnp.dot(h_scr[...], w_ref[...], preferred_element_type=F32)
        o_ref[...] = _silu(acc).astype(o_ref.dtype)

    @pl.when(jnp.logical_not(gated))
    def _():
        acc = jnp.dot(h_scr[...], w_ref[...], preferred_element_type=F32)
        if scaled_tiles:
            scaled = (j >= tail_tiles) & (j < tail_tiles + scaled_tiles)
            acc = acc * jnp.where(scaled, scale, 1.0)
        o_ref[...] = acc.astype(o_ref.dtype)


def _norm_proj(x, g, shift, scale, w, tail_cols=0, scaled_cols=0, col_scale=1.0,
               silu_cols=(0, 0), tm=1024, tn=1024):
    s, d = x.shape
    n = w.shape[1]
    n_mod = g.shape[0]
    assert scaled_cols % tn == 0 and tail_cols % tn == 0 and n_mod == (2 if tail_cols else 1)
    assert all(c % tn == 0 and scaled_cols <= c <= n - tail_cols for c in silu_cols)
    n_tiles, tail_tiles = n // tn, tail_cols // tn
    col_tile = lambda j: (j + n_tiles - tail_tiles) % n_tiles
    vec = pl.BlockSpec((n_mod, d), lambda i, j: (0, 0))
    body = functools.partial(_norm_proj_kernel, tail_tiles=tail_tiles,
                             scaled_tiles=scaled_cols // tn, scale=col_scale,
                             silu_tiles=tuple(c // tn for c in silu_cols))
    return pl.pallas_call(
        body,
        grid=(s // tm, n_tiles),
        in_specs=[pl.BlockSpec(memory_space=pl.ANY), vec, vec, vec,
                  pl.BlockSpec((d, tn), lambda i, j: (0, col_tile(j)))],
        out_specs=pl.BlockSpec((tm, tn), lambda i, j: (i, col_tile(j))),
        out_shape=jax.ShapeDtypeStruct((s, n), BF16),
        scratch_shapes=[pltpu.VMEM((tm, d), F32), pltpu.SemaphoreType.DMA(()),
                        pltpu.VMEM((tm, d), BF16), pltpu.VMEM((tm, LANES), F32)],
        compiler_params=_params("arbitrary", "arbitrary"),
        name="norm_proj",
    )(x, g, shift, scale, w)


def _t5_bucket(dist):
    max_exact = N_BUCKETS // 2
    d = jnp.maximum(dist, 0)
    d_f = jnp.maximum(d, 1).astype(F32)
    large = max_exact + (jnp.log(d_f / max_exact) / math.log(MAX_DISTANCE / max_exact)
                         * (N_BUCKETS - max_exact)).astype(jnp.int32)
    large = jnp.minimum(large, N_BUCKETS - 1)
    return jnp.where(d < max_exact, d, large)


HEAD_ORDER = tuple(range(0, GROUP, 2)) + tuple(range(1, GROUP, 2))


def _bias_table_kernel(rb_ref, bucket_ref, o_ref):
    g = pl.program_id(0)
    bucket = bucket_ref[...]
    b = lax.broadcasted_iota(jnp.int32, (2 * BLOCK, BLOCK), 0)
    a = lax.broadcasted_iota(jnp.int32, (2 * BLOCK, BLOCK), 1)
    dist = a + BLOCK - b
    band = (dist >= 0) & (dist < BLOCK)
    band_first = band & (b >= BLOCK)
    for c, hh in enumerate(HEAD_ORDER):
        by_dist = jnp.zeros(bucket.shape, F32)
        for k in range(N_BUCKETS):
            by_dist = jnp.where(bucket == k, rb_ref[k, g * GROUP + hh] * LOG2E, by_dist)
        rows = jnp.concatenate([by_dist] * (2 * BLOCK // SUBLANES), axis=0)
        tile = pltpu.roll(rows, 0, axis=1, stride=1, stride_axis=0)
        cols = slice(c * BLOCK, (c + 1) * BLOCK)
        o_ref[0, 0, :, cols] = jnp.where(band_first, tile, -jnp.inf)
        o_ref[1, 0, :, cols] = jnp.where(band, tile, -jnp.inf)


def _bias_table(rel_bias):
    n_kv = rel_bias.shape[1] // GROUP
    dist_bucket = jnp.broadcast_to(_t5_bucket(jnp.arange(BLOCK)), (SUBLANES, BLOCK))
    return pl.pallas_call(
        _bias_table_kernel,
        grid=(n_kv,),
        in_specs=[pl.BlockSpec(memory_space=pltpu.SMEM),
                  pl.BlockSpec((SUBLANES, BLOCK), lambda g: (0, 0))],
        out_specs=pl.BlockSpec((2, 1, 2 * BLOCK, GROUP * BLOCK), lambda g: (0, g, 0, 0)),
        out_shape=jax.ShapeDtypeStruct((2, n_kv, 2 * BLOCK, GROUP * BLOCK), F32),
        compiler_params=_params("arbitrary"),
        name="bias_table",
    )(rel_bias, dist_bucket.astype(jnp.int32))


ATTN_BLOCKS = 2


def _attn_kernel(sink_ref, qz_ref, kvp_ref, kvc_ref, bias_ref, o_ref):
    n_kv = bias_ref.shape[1]
    attn_dim, kv_dim = o_ref.shape[1], n_kv * HEAD_DIM
    for blk in range(ATTN_BLOCKS):
        rows = slice(blk * BLOCK, (blk + 1) * BLOCK)
        kv_prev = kvp_ref if blk == 0 else kvc_ref.at[(blk - 1) * BLOCK:blk * BLOCK, :]
        kv_cur = kvc_ref.at[rows, :]
        entry = jnp.where(pl.program_id(0) == 0, 0, 1) if blk == 0 else 1
        _attn_block(sink_ref, qz_ref.at[rows, :attn_dim], qz_ref.at[rows, attn_dim:],
                    kv_prev.at[:, :kv_dim], kv_cur.at[:, :kv_dim],
                    kv_prev.at[:, kv_dim:], kv_cur.at[:, kv_dim:],
                    bias_ref.at[entry], o_ref.at[rows, :])


def _attn_block(sink_ref, q_ref, z_ref, kp_ref, kc_ref, vp_ref, vc_ref, bias_ref, o_ref):
    n_kv = bias_ref.shape[0]
    ones_rows = jnp.ones((BF16_ROWS, 2 * BLOCK), BF16)
    low_half = lax.broadcasted_iota(jnp.int32, (2 * BLOCK, LANES), 1) < HEAD_DIM
    nt_dot = functools.partial(lax.dot_general, dimension_numbers=(((1,), (1,)), ((), ())),
                               preferred_element_type=F32)
    tiles = GROUP // 2
    for g in range(n_kv):
        heads = [g * GROUP + hh for hh in HEAD_ORDER]
        kv_cols = slice((g // 2) * LANES, (g // 2 + 1) * LANES)
        k2 = jnp.concatenate([kp_ref[:, kv_cols], kc_ref[:, kv_cols]], axis=0)
        v2_t = jnp.concatenate([vp_ref[:, kv_cols].T, vc_ref[:, kv_cols].T], axis=1)
        v2_t = jnp.concatenate([v2_t, ones_rows], axis=0)
        k_other = pltpu.roll(k2, HEAD_DIM, axis=1)
        k_low, k_high = (k2, k_other) if g % 2 == 0 else (k_other, k2)
        k_low = jnp.where(low_half, k_low, jnp.zeros_like(k_low))
        k_high = jnp.where(low_half, jnp.zeros_like(k_high), k_high)
        q_rows = jnp.concatenate([q_ref[:, (g * tiles + t) * LANES:(g * tiles + t + 1) * LANES]
                                  for t in range(tiles)], axis=0)
        s_t = jnp.concatenate([nt_dot(k_low, q_rows), nt_dot(k_high, q_rows)], axis=1)
        s_t = s_t + bias_ref[g]
        sink = jnp.concatenate([jnp.full((1, BLOCK), sink_ref[h] * LOG2E, F32) for h in heads],
                               axis=1)
        m = jnp.maximum(jnp.max(s_t, axis=0, keepdims=True), sink)
        p = jnp.exp2(s_t - m).astype(BF16)
        o2_t = jnp.dot(v2_t, p, preferred_element_type=F32)
        den = o2_t[2 * HEAD_DIM:2 * HEAD_DIM + 1] + jnp.exp2(sink - m)
        o_t = (o2_t[:HEAD_DIM] if g % 2 == 0 else o2_t[HEAD_DIM:2 * HEAD_DIM]) / den
        o_t = o_t.astype(o_ref.dtype)
        for t in range(tiles):
            cols = slice((g * tiles + t) * LANES, (g * tiles + t + 1) * LANES)
            both = jnp.concatenate([o_t[:, t * BLOCK:(t + 1) * BLOCK],
                                    o_t[:, (tiles + t) * BLOCK:(tiles + t + 1) * BLOCK]], axis=0)
            o_ref[:, cols] = both.T * z_ref[:, cols]


def _attention(qz, qz_blk, kv, kv_blk, bias, sinks):
    s = qz.shape[0]
    n_kv = bias.shape[1]
    kv_dim = n_kv * HEAD_DIM
    attn_dim = kv_dim * GROUP
    assert n_kv % 2 == 0 and GROUP % 2 == 0 and 2 * HEAD_DIM == LANES and BLOCK == LANES
    rows = ATTN_BLOCKS * BLOCK
    assert s % rows == 0
    before = lambda i: jnp.maximum(i * ATTN_BLOCKS - 1, 0)
    return pl.pallas_call(
        _attn_kernel,
        grid=(s // rows,),
        in_specs=[pl.BlockSpec(memory_space=pltpu.SMEM),
                  pl.BlockSpec((rows, 2 * attn_dim), lambda i: (i, qz_blk)),
                  pl.BlockSpec((BLOCK, 2 * kv_dim), lambda i: (before(i), kv_blk)),
                  pl.BlockSpec((rows, 2 * kv_dim), lambda i: (i, kv_blk)),
                  pl.BlockSpec(bias.shape, lambda i: (0, 0, 0, 0),
                               pipeline_mode=pl.Buffered(1))],
        out_specs=pl.BlockSpec((rows, attn_dim), lambda i: (i, 0)),
        out_shape=jax.ShapeDtypeStruct((s, attn_dim), BF16),
        compiler_params=_params("arbitrary"),
        name="swa_attention",
    )(sinks, qz, kv, kv, bias)


def _trunk(x, c, g_a, w_mod_a, b_mod_a, w_a_in, conv_a, w_a_out, g_kv, w_mod_kv, b_mod_kv,
           w_kv, rel_bias, g_b, w_mod_b, b_mod_b, w_b_in, sinks_b, w_b_out, g_final):
    s, d = x.shape
    n_a, n_b = w_a_in.shape[0], w_b_in.shape[0]
    assert n_b >= 1 and s % BLOCK == 0
    row = lambda v: v.reshape(1, -1)
    cb = jnp.broadcast_to(c.reshape(d, 1), (d, LANES))

    side_operands = (w_a_out[-1], w_b_in[0], w_kv, w_b_out[0], w_mod_kv, w_mod_b[0]) if n_a else ()
    stream = n_a > 0 and all(_side_rows(s, conv_a.shape[2], a.shape[0]) for a in side_operands)
    w_a_out_bf = w_cat = w_b_out_bf = mod_kv = mod_b = None
    for i in range(n_a):
        shift, scale, gate = jnp.split(_adaln(cb, w_mod_a[i], b_mod_a[i]), 3, axis=-1)
        args = (x, row(g_a[i]), shift, scale, w_a_in[i].astype(BF16), conv_a[i])
        if stream and i == n_a - 1:
            y, (w_a_out_bf, w_cat, w_b_out_bf), (mod_kv, mod_b) = _conv_in(
                *args, cast_jobs=((w_a_out[i],), (w_b_in[0], w_kv), (w_b_out[0],)),
                matvec_jobs=((w_mod_kv, b_mod_kv), (w_mod_b[0], b_mod_b[0])), cb=cb)
        else:
            y, _, _ = _conv_in(*args)
            w_a_out_bf = w_a_out[i].astype(BF16)
        x = _out_proj(y, w_a_out_bf, x, gate)

    if mod_kv is None:
        mod_kv = _adaln(cb, w_mod_kv, b_mod_kv)
    shift_kv, scale_kv = jnp.split(mod_kv, 2, axis=-1)
    bias = _bias_table(rel_bias)
    attn_dim = w_b_in.shape[2] // 2
    kv_dim = w_kv.shape[1] // 2
    kv = None
    for i in range(n_b):
        if i > 0 or mod_b is None:
            mod_b = _adaln(cb, w_mod_b[i], b_mod_b[i])
        shift, scale, gate = jnp.split(mod_b, 3, axis=-1)
        q_scale = dict(scaled_cols=attn_dim, col_scale=LOG2E * HEAD_DIM ** -0.5,
                       silu_cols=(attn_dim, 2 * attn_dim))
        if i == 0:
            if w_cat is None:
                w_cat = jnp.concatenate([w_b_in[i], w_kv], axis=1).astype(BF16)
            qz = _norm_proj(x, jnp.stack([g_b[i], g_kv]), jnp.concatenate([shift, shift_kv]),
                            jnp.concatenate([scale, scale_kv]), w_cat,
                            tail_cols=2 * kv_dim, **q_scale)
            kv, kv_blk = qz, attn_dim // kv_dim
        else:
            qz = _norm_proj(x, row(g_b[i]), shift, scale, w_b_in[i].astype(BF16), **q_scale)
        a = _attention(qz, 0, kv, kv_blk, bias, sinks_b[i])
        w_out = w_b_out_bf if i == 0 and w_b_out_bf is not None else w_b_out[i].astype(BF16)
        if i + 1 < n_b:
            x = _out_proj(a, w_out, x, gate)
        else:
            x = _out_proj_norm(a, w_out, x, gate, row(g_final))
    return x


def kernel(x, c, g_a, w_mod_a, b_mod_a, w_a_in, conv_a, w_a_out, g_kv, w_mod_kv, b_mod_kv, w_kv,
           rel_bias, g_b, w_mod_b, b_mod_b, w_b_in, sinks_b, w_b_out, g_final):
    outs = [_trunk(x[b], c[b], g_a, w_mod_a, b_mod_a, w_a_in, conv_a, w_a_out, g_kv, w_mod_kv,
                   b_mod_kv, w_kv, rel_bias, g_b, w_mod_b, b_mod_b, w_b_in, sinks_b, w_b_out,
                   g_final) for b in range(x.shape[0])]
    return jnp.stack(outs, axis=0)
```

```python
import functools
import math

import jax
import jax.numpy as jnp
from jax import lax
from jax.experimental import pallas as pl
from jax.experimental.pallas import tpu as pltpu

HEAD_DIM = 64
GROUP = 8
BLOCK = 128
N_BUCKETS = 32
MAX_DISTANCE = 128
CONV_WIDTH = 3
EPS = 1e-6
LOG2E = math.log2(math.e)

LANES = 128
SUBLANES = 8
MIB = 1024 * 1024
VMEM_LIMIT = 56 * MIB

F32 = jnp.float32
BF16 = jnp.bfloat16
STATS_ROWS = 512
NORM_ROWS = 256


def _params(*semantics, vmem=VMEM_LIMIT):
    return pltpu.CompilerParams(dimension_semantics=semantics, vmem_limit_bytes=vmem)


def _silu(v):
    return v * jax.nn.sigmoid(v)


def _adaln_kernel(cb_ref, w_ref, b_ref, o_ref):
    ca = _silu(cb_ref[...])
    for g in range(o_ref.shape[1] // LANES):
        cols = slice(g * LANES, (g + 1) * LANES)
        o_ref[:, cols] = jnp.sum(w_ref[:, cols] * ca, axis=0, keepdims=True) + b_ref[:, cols]


def _adaln(cb, w, b, tn=1024):
    k, n = w.shape
    return pl.pallas_call(
        _adaln_kernel,
        grid=(n // tn,),
        in_specs=[
            pl.BlockSpec((k, LANES), lambda j: (0, 0)),
            pl.BlockSpec((k, tn), lambda j: (0, j)),
            pl.BlockSpec((1, tn), lambda j: (0, j)),
        ],
        out_specs=pl.BlockSpec((1, tn), lambda j: (0, j)),
        out_shape=jax.ShapeDtypeStruct((1, n), F32),
        compiler_params=_params("arbitrary"),
        name="adaln_matvec",
    )(cb, w, b.reshape(1, n))


def _col_groups(d):
    return [slice(k * LANES, (k + 1) * LANES) for k in range(d // LANES)]


def _rms_stats(x_ref, rstd_scr):
    tm, d = x_ref.shape

    def body(r, carry):
        base = pl.multiple_of(r * STATS_ROWS, STATS_ROWS)
        parts = []
        for s in range(STATS_ROWS // SUBLANES):
            xr = x_ref[pl.ds(base + s * SUBLANES, SUBLANES), :]
            sq = xr * xr
            part = sq[:, :LANES]
            for cols in _col_groups(d)[1:]:
                part = part + sq[:, cols]
            parts.append(part)
        acc = jnp.concatenate(parts, axis=0)
        ms = jnp.sum(acc, axis=-1, keepdims=True) * (1.0 / d)
        rstd_scr[pl.ds(base, STATS_ROWS), :] = jnp.broadcast_to(lax.rsqrt(ms + EPS),
                                                               (STATS_ROWS, LANES))
        return carry

    lax.fori_loop(0, tm // STATS_ROWS, body, 0)


def _modnorm_apply(x_ref, rstd_scr, g_ref, sh_ref, sc_ref, mod, h_scr):
    tm, d = x_ref.shape
    vec = slice(mod, mod + 1)

    def body(r, carry):
        base = pl.multiple_of(r * NORM_ROWS, NORM_ROWS)
        gains = [g_ref[vec, cols] * (1.0 + sc_ref[vec, cols]) for cols in _col_groups(d)]
        for s in range(NORM_ROWS // BF16_ROWS):
            rows = pl.ds(base + s * BF16_ROWS, BF16_ROWS)
            rstd = rstd_scr[rows, :]
            for cols, gain in zip(_col_groups(d), gains):
                h_scr[rows, cols] = ((x_ref[rows, cols] * rstd) * gain
                                     + sh_ref[vec, cols]).astype(h_scr.dtype)
        return carry

    lax.fori_loop(0, tm // NORM_ROWS, body, 0)


def _row_tile_copy(x_hbm, x_buf, sem, i):
    tm = x_buf.shape[0]
    return pltpu.make_async_copy(x_hbm.at[pl.ds(i * tm, tm), :], x_buf, sem)


def _await_row_tile(x_hbm, x_buf, sem, i):
    @pl.when(i == 0)
    def _():
        _row_tile_copy(x_hbm, x_buf, sem, 0).start()

    _row_tile_copy(x_hbm, x_buf, sem, i).wait()


def _prefetch_next_row_tile(x_hbm, x_buf, sem, i):
    @pl.when(i + 1 < pl.num_programs(0))
    def _():
        _row_tile_copy(x_hbm, x_buf, sem, i + 1).start()


SIDE_COLS = 2048
ROW_SPLIT = 2


def _cast_cols(in_ref, o_ref, c0, o0):
    n = min(SIDE_COLS, in_ref.shape[1] - c0)
    o_ref[:, o0:o0 + n] = in_ref[:, c0:c0 + n].astype(o_ref.dtype)


def _matvec_cols(cb_ref, w_ref, acc_ref, c0):
    ca = _silu(cb_ref[...])
    for k in range(c0 // LANES, min(c0 + SIDE_COLS, w_ref.shape[1]) // LANES):
        cols = slice(k * LANES, (k + 1) * LANES)
        prod = w_ref[:, cols] * ca
        part = prod[0:SUBLANES]
        for q in range(1, prod.shape[0] // SUBLANES):
            part = part + prod[q * SUBLANES:(q + 1) * SUBLANES]
        acc_ref[:, cols] += part


def _conv_in_kernel(*refs, cast_widths, n_matvec):
    refs = list(refs)
    take = lambda n: [refs.pop(0) for _ in range(n)]
    x_hbm, g_ref, sh_ref, sc_ref, wb_ref, wc_ref, wu_ref, wz_ref, cw_ref = take(9)
    cast_in = [take(len(widths)) for widths in cast_widths]
    cb_ref = take(1)[0] if n_matvec else None
    mv_in = [take(2) for _ in range(n_matvec)]
    y_ref = take(1)[0]
    cast_out = take(len(cast_widths))
    mv_out = take(n_matvec)
    x_buf, x_sem, h_scr, rstd_scr, halo_scr = take(5)
    mv_acc = take(n_matvec)

    i = pl.program_id(0)
    j = pl.program_id(1)
    tm = y_ref.shape[0]
    step = i * pl.num_programs(1) + j
    last_step = pl.num_programs(0) * pl.num_programs(1) - 1

    @pl.when(step == 0)
    def _():
        for acc_ref in mv_acc:
            acc_ref[...] = jnp.zeros(acc_ref.shape, F32)

    @pl.when(j == 0)
    def _():
        _await_row_tile(x_hbm, x_buf, x_sem, i)
        _rms_stats(x_buf, rstd_scr)
        _modnorm_apply(x_buf, rstd_scr, g_ref, sh_ref, sc_ref, 0, h_scr)
        _prefetch_next_row_tile(x_hbm, x_buf, x_sem, i)

    @pl.when(i == 0)
    def _():
        halo_scr[j] = jnp.zeros(halo_scr.shape[1:], F32)

    side_jobs = []
    for ins, out in zip(cast_in, cast_out):
        col = 0
        for r in ins:
            for c0 in range(0, r.shape[1], SIDE_COLS):
                side_jobs.append(functools.partial(_cast_cols, r, out, c0, col + c0))
            col += r.shape[1]
    for (w_ref, _), acc_ref in zip(mv_in, mv_acc):
        for c0 in range(0, w_ref.shape[1], SIDE_COLS):
            side_jobs.append(functools.partial(_matvec_cols, cb_ref, w_ref, acc_ref, c0))
    proj = [[None] * ROW_SPLIT for _ in range(4)]
    n_gaps = 4 * ROW_SPLIT - 1
    rows_per = tm // ROW_SPLIT
    for n, (half, k) in enumerate((half, k) for half in range(ROW_SPLIT) for k in range(4)):
        w_ref = (wc_ref, wu_ref, wb_ref, wz_ref)[k]
        rows = slice(half * rows_per, (half + 1) * rows_per)
        proj[k][half] = jnp.dot(h_scr[rows, :], w_ref[...], preferred_element_type=F32)
        if n < n_gaps:
            for job in side_jobs[n::n_gaps]:
                job()
    c_proj, u_proj, b_proj, z_proj = [jnp.concatenate(p, axis=0) for p in proj]

    cu = c_proj * u_proj
    w0 = cw_ref[0:1, :]
    w1 = cw_ref[1:2, :]
    w2 = cw_ref[2:3, :]
    conv = w0 * pltpu.roll(cu, 2, axis=0) + w1 * pltpu.roll(cu, 1, axis=0) + w2 * cu
    gate = b_proj * _silu(z_proj)
    y_ref[...] = (gate * conv).astype(y_ref.dtype)

    prev = halo_scr[j]
    top = cu[0:SUBLANES]
    row = lax.broadcasted_iota(jnp.int32, top.shape, 0)
    back1 = jnp.where(row < 1, pltpu.roll(prev, 1, axis=0), pltpu.roll(top, 1, axis=0))
    back2 = jnp.where(row < 2, pltpu.roll(prev, 2, axis=0), pltpu.roll(top, 2, axis=0))
    conv_top = w0 * back2 + w1 * back1 + w2 * top
    y_ref[0:SUBLANES, :] = (gate[0:SUBLANES] * conv_top).astype(y_ref.dtype)
    halo_scr[j] = cu[tm - SUBLANES:tm]

    if n_matvec:
        @pl.when(step == last_step)
        def _():
            for (_, b_ref), o_ref, acc_ref in zip(mv_in, mv_out, mv_acc):
                o_ref[...] = jnp.sum(acc_ref[...], axis=0, keepdims=True) + b_ref[...]


CONV_TM, CONV_TN = 1024, 256
BF16_ROWS = 16


def _side_rows(s, c, k):
    steps = (s // CONV_TM) * (c // CONV_TN)
    rows = k // steps
    return rows if rows * steps == k and rows % BF16_ROWS == 0 else 0


def _conv_in(x, g, shift, scale, w_in, conv_w, cast_jobs=(), matvec_jobs=(), cb=None):
    tm, tn = CONV_TM, CONV_TN
    s, d = x.shape
    c = conv_w.shape[1]
    nj = c // tn
    vec = pl.BlockSpec((1, d), lambda i, j: (0, 0))
    w_specs = [pl.BlockSpec((d, tn), lambda i, j, k=k: (0, k * nj + j)) for k in range(4)]
    row_block = lambda rows, n: pl.BlockSpec((rows, n), lambda i, j: (i * nj + j, 0))
    whole = lambda n: pl.BlockSpec((1, n), lambda i, j: (0, 0))

    side_in, side_in_specs = [], []
    out_shapes = [jax.ShapeDtypeStruct((s, c), BF16)]
    out_specs = [pl.BlockSpec((tm, tn), lambda i, j: (i, j))]
    scratch = [pltpu.VMEM((tm, d), F32), pltpu.SemaphoreType.DMA(()),
               pltpu.VMEM((tm, d), BF16), pltpu.VMEM((tm, LANES), F32),
               pltpu.VMEM((nj, SUBLANES, tn), F32)]
    for job in cast_jobs:
        k = job[0].shape[0]
        rows = _side_rows(s, c, k)
        assert rows and all(a.shape[0] == k and a.shape[1] % LANES == 0 for a in job)
        side_in += list(job)
        side_in_specs += [row_block(rows, a.shape[1]) for a in job]
        n = sum(a.shape[1] for a in job)
        out_shapes.append(jax.ShapeDtypeStruct((k, n), BF16))
        out_specs.append(row_block(rows, n))
    if matvec_jobs:
        k = cb.shape[0]
        rows = _side_rows(s, c, k)
        assert rows
        side_in.append(cb)
        side_in_specs.append(row_block(rows, LANES))
    for w, b in matvec_jobs:
        n = w.shape[1]
        assert w.shape[0] == k and n % LANES == 0
        side_in += [w, b.reshape(1, n)]
        side_in_specs += [row_block(rows, n), whole(n)]
        out_shapes.append(jax.ShapeDtypeStruct((1, n), F32))
        out_specs.append(whole(n))
        scratch.append(pltpu.VMEM((SUBLANES, n), F32))

    body = functools.partial(_conv_in_kernel, cast_widths=tuple(tuple(a.shape[1] for a in job)
                                                                for job in cast_jobs),
                             n_matvec=len(matvec_jobs))
    outs = pl.pallas_call(
        body,
        grid=(s // tm, nj),
        in_specs=[pl.BlockSpec(memory_space=pl.ANY), vec, vec, vec, *w_specs,
                  pl.BlockSpec((CONV_WIDTH, tn), lambda i, j: (0, j)), *side_in_specs],
        out_specs=out_specs,
        out_shape=out_shapes,
        scratch_shapes=scratch,
        compiler_params=_params("arbitrary", "arbitrary"),
        name="conv_in_proj",
    )(x, g, shift, scale, w_in, w_in, w_in, w_in, conv_w, *side_in)
    n_cast = len(cast_jobs)
    return outs[0], list(outs[1:1 + n_cast]), list(outs[1 + n_cast:])


def _out_proj_kernel(y_ref, w_ref, x_ref, gate_ref, o_ref):
    acc = jnp.dot(y_ref[...], w_ref[...], preferred_element_type=F32)
    o_ref[...] = x_ref[...] + gate_ref[...] * acc


def _out_proj(y, w, x, gate, tm=1024, tn=1024):
    s, k = y.shape
    d = w.shape[1]
    return pl.pallas_call(
        _out_proj_kernel,
        grid=(s // tm, d // tn),
        in_specs=[pl.BlockSpec((tm, k), lambda i, j: (i, 0)),
                  pl.BlockSpec((k, tn), lambda i, j: (0, j)),
                  pl.BlockSpec((tm, tn), lambda i, j: (i, j)),
                  pl.BlockSpec((1, tn), lambda i, j: (0, j))],
        out_specs=pl.BlockSpec((tm, tn), lambda i, j: (i, j)),
        out_shape=jax.ShapeDtypeStruct((s, d), F32),
        compiler_params=_params("arbitrary", "arbitrary"),
        name="out_proj",
    )(y, w, x, gate)


def _out_proj_norm_kernel(y_ref, w_ref, x_ref, gate_ref, gf_ref, o_ref):
    acc = jnp.dot(y_ref[...], w_ref[...], preferred_element_type=F32)
    x2 = x_ref[...] + gate_ref[...] * acc
    ms = jnp.mean(x2 * x2, axis=-1, keepdims=True)
    o_ref[...] = (x2 * lax.rsqrt(ms + EPS)) * gf_ref[...]


def _out_proj_norm(y, w, x, gate, g_final, tm=256):
    s, k = y.shape
    d = w.shape[1]
    vec = pl.BlockSpec((1, d), lambda i: (0, 0))
    return pl.pallas_call(
        _out_proj_norm_kernel,
        grid=(s // tm,),
        in_specs=[pl.BlockSpec((tm, k), lambda i: (i, 0)),
                  pl.BlockSpec((k, d), lambda i: (0, 0), pipeline_mode=pl.Buffered(1)),
                  pl.BlockSpec((tm, d), lambda i: (i, 0)), vec, vec],
        out_specs=pl.BlockSpec((tm, d), lambda i: (i, 0)),
        out_shape=jax.ShapeDtypeStruct((s, d), F32),
        compiler_params=_params("arbitrary", vmem=60 * MIB),
        name="out_proj_final_norm",
    )(y, w, x, gate, g_final)


def _norm_proj_kernel(x_hbm, g_ref, sh_ref, sc_ref, w_ref, o_ref, x_buf, x_sem, h_scr, rstd_scr,
                      *, tail_tiles, scaled_tiles, scale, silu_tiles):
    i = pl.program_id(0)
    j = pl.program_id(1)

    @pl.when(j == 0)
    def _():
        _await_row_tile(x_hbm, x_buf, x_sem, i)
        _rms_stats(x_buf, rstd_scr)
        _modnorm_apply(x_buf, rstd_scr, g_ref, sh_ref, sc_ref, 1 if tail_tiles else 0, h_scr)
        if not tail_tiles:
            _prefetch_next_row_tile(x_hbm, x_buf, x_sem, i)

    if tail_tiles:
        @pl.when(j == tail_tiles)
        def _():
            _modnorm_apply(x_buf, rstd_scr, g_ref, sh_ref, sc_ref, 0, h_scr)
            _prefetch_next_row_tile(x_hbm, x_buf, x_sem, i)

    gated = (j >= tail_tiles + silu_tiles[0]) & (j < tail_tiles + silu_tiles[1])

    @pl.when(gated)
    def _():
        acc = jnp.dot(h_scr[...], w_ref[...], preferred_element_type=F32)
        o_ref[...] = _silu(acc).astype(o_ref.dtype)

    @pl.when(jnp.logical_not(gated))
    def _():
        acc = jnp.dot(h_scr[...], w_ref[...], preferred_element_type=F32)
        if scaled_tiles:
            scaled = (j >= tail_tiles) & (j < tail_tiles + scaled_tiles)
            acc = acc * jnp.where(scaled, scale, 1.0)
        o_ref[...] = acc.astype(o_ref.dtype)


def _norm_proj(x, g, shift, scale, w, tail_cols=0, scaled_cols=0, col_scale=1.0,
               silu_cols=(0, 0), tm=1024, tn=1024):
    s, d = x.shape
    n = w.shape[1]
    n_mod = g.shape[0]
    assert scaled_cols % tn == 0 and tail_cols % tn == 0 and n_mod == (2 if tail_cols else 1)
    assert all(c % tn == 0 and scaled_cols <= c <= n - tail_cols for c in silu_cols)
    n_tiles, tail_tiles = n // tn, tail_cols // tn
    col_tile = lambda j: (j + n_tiles - tail_tiles) % n_tiles
    vec = pl.BlockSpec((n_mod, d), lambda i, j: (0, 0))
    body = functools.partial(_norm_proj_kernel, tail_tiles=tail_tiles,
                             scaled_tiles=scaled_cols // tn, scale=col_scale,
                             silu_tiles=tuple(c // tn for c in silu_cols))
    return pl.pallas_call(
        body,
        grid=(s // tm, n_tiles),
        in_specs=[pl.BlockSpec(memory_space=pl.ANY), vec, vec, vec,
                  pl.BlockSpec((d, tn), lambda i, j: (0, col_tile(j)))],
        out_specs=pl.BlockSpec((tm, tn), lambda i, j: (i, col_tile(j))),
        out_shape=jax.ShapeDtypeStruct((s, n), BF16),
        scratch_shapes=[pltpu.VMEM((tm, d), F32), pltpu.SemaphoreType.DMA(()),
                        pltpu.VMEM((tm, d), BF16), pltpu.VMEM((tm, LANES), F32)],
        compiler_params=_params("arbitrary", "arbitrary"),
        name="norm_proj",
    )(x, g, shift, scale, w)


def _t5_bucket(dist):
    max_exact = N_BUCKETS // 2
    d = jnp.maximum(dist, 0)
    d_f = jnp.maximum(d, 1).astype(F32)
    large = max_exact + (jnp.log(d_f / max_exact) / math.log(MAX_DISTANCE / max_exact)
                         * (N_BUCKETS - max_exact)).astype(jnp.int32)
    large = jnp.minimum(large, N_BUCKETS - 1)
    return jnp.where(d < max_exact, d, large)


HEAD_ORDER = tuple(range(0, GROUP, 2)) + tuple(range(1, GROUP, 2))


def _bias_table_kernel(rb_ref, bucket_ref, o_ref):
    g = pl.program_id(0)
    bucket = bucket_ref[...]
    b = lax.broadcasted_iota(jnp.int32, (2 * BLOCK, BLOCK), 0)
    a = lax.broadcasted_iota(jnp.int32, (2 * BLOCK, BLOCK), 1)
    dist = a + BLOCK - b
    band = (dist >= 0) & (dist < BLOCK)
    band_first = band & (b >= BLOCK)
    for c, hh in enumerate(HEAD_ORDER):
        by_dist = jnp.zeros(bucket.shape, F32)
        for k in range(N_BUCKETS):
            by_dist = jnp.where(bucket == k, rb_ref[k, g * GROUP + hh] * LOG2E, by_dist)
        rows = jnp.concatenate([by_dist] * (2 * BLOCK // SUBLANES), axis=0)
        tile = pltpu.roll(rows, 0, axis=1, stride=1, stride_axis=0)
        cols = slice(c * BLOCK, (c + 1) * BLOCK)
        o_ref[0, 0, :, cols] = jnp.where(band_first, tile, -jnp.inf)
        o_ref[1, 0, :, cols] = jnp.where(band, tile, -jnp.inf)


def _bias_table(rel_bias):
    n_kv = rel_bias.shape[1] // GROUP
    dist_bucket = jnp.broadcast_to(_t5_bucket(jnp.arange(BLOCK)), (SUBLANES, BLOCK))
    return pl.pallas_call(
        _bias_table_kernel,
        grid=(n_kv,),
        in_specs=[pl.BlockSpec(memory_space=pltpu.SMEM),
                  pl.BlockSpec((SUBLANES, BLOCK), lambda g: (0, 0))],
        out_specs=pl.BlockSpec((2, 1, 2 * BLOCK, GROUP * BLOCK), lambda g: (0, g, 0, 0)),
        out_shape=jax.ShapeDtypeStruct((2, n_kv, 2 * BLOCK, GROUP * BLOCK), F32),
        compiler_params=_params("arbitrary"),
        name="bias_table",
    )(rel_bias, dist_bucket.astype(jnp.int32))


ATTN_BLOCKS = 2


def _attn_kernel(sink_ref, qz_ref, kvp_ref, kvc_ref, bias_ref, o_ref):
    n_kv = bias_ref.shape[1]
    attn_dim, kv_dim = o_ref.shape[1], n_kv * HEAD_DIM
    for blk in range(ATTN_BLOCKS):
        rows = slice(blk * BLOCK, (blk + 1) * BLOCK)
        kv_prev = kvp_ref if blk == 0 else kvc_ref.at[(blk - 1) * BLOCK:blk * BLOCK, :]
        kv_cur = kvc_ref.at[rows, :]
        entry = jnp.where(pl.program_id(0) == 0, 0, 1) if blk == 0 else 1
        _attn_block(sink_ref, qz_ref.at[rows, :attn_dim], qz_ref.at[rows, attn_dim:],
                    kv_prev.at[:, :kv_dim], kv_cur.at[:, :kv_dim],
                    kv_prev.at[:, kv_dim:], kv_cur.at[:, kv_dim:],
                    bias_ref.at[entry], o_ref.at[rows, :])


def _attn_block(sink_ref, q_ref, z_ref, kp_ref, kc_ref, vp_ref, vc_ref, bias_ref, o_ref):
    n_kv = bias_ref.shape[0]
    ones_rows = jnp.ones((BF16_ROWS, 2 * BLOCK), BF16)
    low_half = lax.broadcasted_iota(jnp.int32, (2 * BLOCK, LANES), 1) < HEAD_DIM
    nt_dot = functools.partial(lax.dot_general, dimension_numbers=(((1,), (1,)), ((), ())),
                               preferred_element_type=F32)
    tiles = GROUP // 2
    for g in range(n_kv):
        heads = [g * GROUP + hh for hh in HEAD_ORDER]
        kv_cols = slice((g // 2) * LANES, (g // 2 + 1) * LANES)
        k2 = jnp.concatenate([kp_ref[:, kv_cols], kc_ref[:, kv_cols]], axis=0)
        v2_t = jnp.concatenate([vp_ref[:, kv_cols].T, vc_ref[:, kv_cols].T], axis=1)
        v2_t = jnp.concatenate([v2_t, ones_rows], axis=0)
        k_other = pltpu.roll(k2, HEAD_DIM, axis=1)
        k_low, k_high = (k2, k_other) if g % 2 == 0 else (k_other, k2)
        k_low = jnp.where(low_half, k_low, jnp.zeros_like(k_low))
        k_high = jnp.where(low_half, jnp.zeros_like(k_high), k_high)
        q_rows = jnp.concatenate([q_ref[:, (g * tiles + t) * LANES:(g * tiles + t + 1) * LANES]
                                  for t in range(tiles)], axis=0)
        s_t = jnp.concatenate([nt_dot(k_low, q_rows), nt_dot(k_high, q_rows)], axis=1)
        s_t = s_t + bias_ref[g]
        sink = jnp.concatenate([jnp.full((1, BLOCK), sink_ref[h] * LOG2E, F32) for h in heads],
                               axis=1)
        m = jnp.maximum(jnp.max(s_t, axis=0, keepdims=True), sink)
        p = jnp.exp2(s_t - m).astype(BF16)
        o2_t = jnp.dot(v2_t, p, preferred_element_type=F32)
        den = o2_t[2 * HEAD_DIM:2 * HEAD_DIM + 1] + jnp.exp2(sink - m)
        o_t = (o2_t[:HEAD_DIM] if g % 2 == 0 else o2_t[HEAD_DIM:2 * HEAD_DIM]) / den
        o_t = o_t.astype(o_ref.dtype)
        for t in range(tiles):
            cols = slice((g * tiles + t) * LANES, (g * tiles + t + 1) * LANES)
            both = jnp.concatenate([o_t[:, t * BLOCK:(t + 1) * BLOCK],
                                    o_t[:, (tiles + t) * BLOCK:(tiles + t + 1) * BLOCK]], axis=0)
            o_ref[:, cols] = both.T * z_ref[:, cols]


def _attention(qz, qz_blk, kv, kv_blk, bias, sinks):
    s = qz.shape[0]
    n_kv = bias.shape[1]
    kv_dim = n_kv * HEAD_DIM
    attn_dim = kv_dim * GROUP
    assert n_kv % 2 == 0 and GROUP % 2 == 0 and 2 * HEAD_DIM == LANES and BLOCK == LANES
    rows = ATTN_BLOCKS * BLOCK
    assert s % rows == 0
    before = lambda i: jnp.maximum(i * ATTN_BLOCKS - 1, 0)
    return pl.pallas_call(
        _attn_kernel,
        grid=(s // rows,),
        in_specs=[pl.BlockSpec(memory_space=pltpu.SMEM),
                  pl.BlockSpec((rows, 2 * attn_dim), lambda i: (i, qz_blk)),
                  pl.BlockSpec((BLOCK, 2 * kv_dim), lambda i: (before(i), kv_blk)),
                  pl.BlockSpec((rows, 2 * kv_dim), lambda i: (i, kv_blk)),
                  pl.BlockSpec(bias.shape, lambda i: (0, 0, 0, 0),
                               pipeline_mode=pl.Buffered(1))],
        out_specs=pl.BlockSpec((rows, attn_dim), lambda i: (i, 0)),
        out_shape=jax.ShapeDtypeStruct((s, attn_dim), BF16),
        compiler_params=_params("arbitrary"),
        name="swa_attention",
    )(sinks, qz, kv, kv, bias)


def _trunk(x, c, g_a, w_mod_a, b_mod_a, w_a_in, conv_a, w_a_out, g_kv, w_mod_kv, b_mod_kv,
           w_kv, rel_bias, g_b, w_mod_b, b_mod_b, w_b_in, sinks_b, w_b_out, g_final):
    s, d = x.shape
    n_a, n_b = w_a_in.shape[0], w_b_in.shape[0]
    assert n_b >= 1 and s % BLOCK == 0
    row = lambda v: v.reshape(1, -1)
    cb = jnp.broadcast_to(c.reshape(d, 1), (d, LANES))

    side_operands = (w_a_out[-1], w_b_in[0], w_kv, w_b_out[0], w_mod_kv, w_mod_b[0]) if n_a else ()
    stream = n_a > 0 and all(_side_rows(s, conv_a.shape[2], a.shape[0]) for a in side_operands)
    w_a_out_bf = w_cat = w_b_out_bf = mod_kv = mod_b = None
    for i in range(n_a):
        shift, scale, gate = jnp.split(_adaln(cb, w_mod_a[i], b_mod_a[i]), 3, axis=-1)
        args = (x, row(g_a[i]), shift, scale, w_a_in[i].astype(BF16), conv_a[i])
        if stream and i == n_a - 1:
            y, (w_a_out_bf, w_cat, w_b_out_bf), (mod_kv, mod_b) = _conv_in(
                *args, cast_jobs=((w_a_out[i],), (w_b_in[0], w_kv), (w_b_out[0],)),
                matvec_jobs=((w_mod_kv, b_mod_kv), (w_mod_b[0], b_mod_b[0])), cb=cb)
        else:
            y, _, _ = _conv_in(*args)
            w_a_out_bf = w_a_out[i].astype(BF16)
        x = _out_proj(y, w_a_out_bf, x, gate)

    if mod_kv is None:
        mod_kv = _adaln(cb, w_mod_kv, b_mod_kv)
    shift_kv, scale_kv = jnp.split(mod_kv, 2, axis=-1)
    bias = _bias_table(rel_bias)
    attn_dim = w_b_in.shape[2] // 2
    kv_dim = w_kv.shape[1] // 2
    kv = None
    for i in range(n_b):
        if i > 0 or mod_b is None:
            mod_b = _adaln(cb, w_mod_b[i], b_mod_b[i])
        shift, scale, gate = jnp.split(mod_b, 3, axis=-1)
        q_scale = dict(scaled_cols=attn_dim, col_scale=LOG2E * HEAD_DIM ** -0.5,
                       silu_cols=(attn_dim, 2 * attn_dim))
        if i == 0:
            if w_cat is None:
                w_cat = jnp.concatenate([w_b_in[i], w_kv], axis=1).astype(BF16)
            qz = _norm_proj(x, jnp.stack([g_b[i], g_kv]), jnp.concatenate([shift, shift_kv]),
                            jnp.concatenate([scale, scale_kv]), w_cat,
                            tail_cols=2 * kv_dim, **q_scale)
            kv, kv_blk = qz, attn_dim // kv_dim
        else:
            qz = _norm_proj(x, row(g_b[i]), shift, scale, w_b_in[i].astype(BF16), **q_scale)
        a = _attention(qz, 0, kv, kv_blk, bias, sinks_b[i])
        w_out = w_b_out_bf if i == 0 and w_b_out_bf is not None else w_b_out[i].astype(BF16)
        if i + 1 < n_b:
            x = _out_proj(a, w_out, x, gate)
        else:
            x = _out_proj_norm(a, w_out, x, gate, row(g_final))
    return x


def kernel(x, c, g_a, w_mod_a, b_mod_a, w_a_in, conv_a, w_a_out, g_kv, w_mod_kv, b_mod_kv, w_kv,
           rel_bias, g_b, w_mod_b, b_mod_b, w_b_in, sinks_b, w_b_out, g_final):
    outs = [_trunk(x[b], c[b], g_a, w_mod_a, b_mod_a, w_a_in, conv_a, w_a_out, g_kv, w_mod_kv,
                   b_mod_kv, w_kv, rel_bias, g_b, w_mod_b, b_mod_b, w_b_in, sinks_b, w_b_out,
                   g_final) for b in range(x.shape[0])]
    return jnp.stack(outs, axis=0)
```

```python
import functools
import math

import jax
import jax.numpy as jnp
from jax import lax
from jax.experimental import pallas as pl
from jax.experimental.pallas import tpu as pltpu

HEAD_DIM = 64
GROUP = 8
BLOCK = 128
N_BUCKETS = 32
MAX_DISTANCE = 128
CONV_WIDTH = 3
EPS = 1e-6
LOG2E = math.log2(math.e)

LANES = 128
SUBLANES = 8
MIB = 1024 * 1024
VMEM_LIMIT = 56 * MIB

F32 = jnp.float32
BF16 = jnp.bfloat16
STATS_ROWS = 512
NORM_ROWS = 256


def _params(*semantics, vmem=VMEM_LIMIT):
    return pltpu.CompilerParams(dimension_semantics=semantics, vmem_limit_bytes=vmem)


def _silu(v):
    return v * jax.nn.sigmoid(v)


def _adaln_kernel(cb_ref, w_ref, b_ref, o_ref):
    ca = _silu(cb_ref[...])
    for g in range(o_ref.shape[1] // LANES):
        cols = slice(g * LANES, (g + 1) * LANES)
        o_ref[:, cols] = jnp.sum(w_ref[:, cols] * ca, axis=0, keepdims=True) + b_ref[:, cols]


def _adaln(cb, w, b, tn=1024):
    k, n = w.shape
    return pl.pallas_call(
        _adaln_kernel,
        grid=(n // tn,),
        in_specs=[
            pl.BlockSpec((k, LANES), lambda j: (0, 0)),
            pl.BlockSpec((k, tn), lambda j: (0, j)),
            pl.BlockSpec((1, tn), lambda j: (0, j)),
        ],
        out_specs=pl.BlockSpec((1, tn), lambda j: (0, j)),
        out_shape=jax.ShapeDtypeStruct((1, n), F32),
        compiler_params=_params("arbitrary"),
        name="adaln_matvec",
    )(cb, w, b.reshape(1, n))


def _col_groups(d):
    return [slice(k * LANES, (k + 1) * LANES) for k in range(d // LANES)]


def _rms_stats(x_ref, rstd_scr):
    tm, d = x_ref.shape

    def body(r, carry):
        base = pl.multiple_of(r * STATS_ROWS, STATS_ROWS)
        parts = []
        for s in range(STATS_ROWS // SUBLANES):
            xr = x_ref[pl.ds(base + s * SUBLANES, SUBLANES), :]
            sq = xr * xr
            part = sq[:, :LANES]
            for cols in _col_groups(d)[1:]:
                part = part + sq[:, cols]
            parts.append(part)
        acc = jnp.concatenate(parts, axis=0)
        ms = jnp.sum(acc, axis=-1, keepdims=True) * (1.0 / d)
        rstd_scr[pl.ds(base, STATS_ROWS), :] = jnp.broadcast_to(lax.rsqrt(ms + EPS),
                                                               (STATS_ROWS, LANES))
        return carry

    lax.fori_loop(0, tm // STATS_ROWS, body, 0)


def _modnorm_apply(x_ref, rstd_scr, g_ref, sh_ref, sc_ref, mod, h_scr):
    tm, d = x_ref.shape
    vec = slice(mod, mod + 1)

    def body(r, carry):
        base = pl.multiple_of(r * NORM_ROWS, NORM_ROWS)
        gains = [g_ref[vec, cols] * (1.0 + sc_ref[vec, cols]) for cols in _col_groups(d)]
        for s in range(NORM_ROWS // BF16_ROWS):
            rows = pl.ds(base + s * BF16_ROWS, BF16_ROWS)
            rstd = rstd_scr[rows, :]
            for cols, gain in zip(_col_groups(d), gains):
                h_scr[rows, cols] = ((x_ref[rows, cols] * rstd) * gain
                                     + sh_ref[vec, cols]).astype(h_scr.dtype)
        return carry

    lax.fori_loop(0, tm // NORM_ROWS, body, 0)


def _row_tile_copy(x_hbm, x_buf, sem, i):
    tm = x_buf.shape[0]
    return pltpu.make_async_copy(x_hbm.at[pl.ds(i * tm, tm), :], x_buf, sem)


def _await_row_tile(x_hbm, x_buf, sem, i):
    @pl.when(i == 0)
    def _():
        _row_tile_copy(x_hbm, x_buf, sem, 0).start()

    _row_tile_copy(x_hbm, x_buf, sem, i).wait()


def _prefetch_next_row_tile(x_hbm, x_buf, sem, i):
    @pl.when(i + 1 < pl.num_programs(0))
    def _():
        _row_tile_copy(x_hbm, x_buf, sem, i + 1).start()


SIDE_COLS = 2048
ROW_SPLIT = 2


def _cast_cols(in_ref, o_ref, c0, o0):
    n = min(SIDE_COLS, in_ref.shape[1] - c0)
    o_ref[:, o0:o0 + n] = in_ref[:, c0:c0 + n].astype(o_ref.dtype)


def _matvec_cols(cb_ref, w_ref, acc_ref, c0):
    ca = _silu(cb_ref[...])
    for k in range(c0 // LANES, min(c0 + SIDE_COLS, w_ref.shape[1]) // LANES):
        cols = slice(k * LANES, (k + 1) * LANES)
        prod = w_ref[:, cols] * ca
        part = prod[0:SUBLANES]
        for q in range(1, prod.shape[0] // SUBLANES):
            part = part + prod[q * SUBLANES:(q + 1) * SUBLANES]
        acc_ref[:, cols] += part


def _conv_in_kernel(*refs, cast_widths, n_matvec):
    refs = list(refs)
    take = lambda n: [refs.pop(0) for _ in range(n)]
    x_hbm, g_ref, sh_ref, sc_ref, wb_ref, wc_ref, wu_ref, wz_ref, cw_ref = take(9)
    cast_in = [take(len(widths)) for widths in cast_widths]
    cb_ref = take(1)[0] if n_matvec else None
    mv_in = [take(2) for _ in range(n_matvec)]
    y_ref = take(1)[0]
    cast_out = take(len(cast_widths))
    mv_out = take(n_matvec)
    x_buf, x_sem, h_scr, rstd_scr, halo_scr = take(5)
    mv_acc = take(n_matvec)

    i = pl.program_id(0)
    j = pl.program_id(1)
    tm = y_ref.shape[0]
    step = i * pl.num_programs(1) + j
    last_step = pl.num_programs(0) * pl.num_programs(1) - 1

    @pl.when(step == 0)
    def _():
        for acc_ref in mv_acc:
            acc_ref[...] = jnp.zeros(acc_ref.shape, F32)

    @pl.when(j == 0)
    def _():
        _await_row_tile(x_hbm, x_buf, x_sem, i)
        _rms_stats(x_buf, rstd_scr)
        _modnorm_apply(x_buf, rstd_scr, g_ref, sh_ref, sc_ref, 0, h_scr)
        _prefetch_next_row_tile(x_hbm, x_buf, x_sem, i)

    @pl.when(i == 0)
    def _():
        halo_scr[j] = jnp.zeros(halo_scr.shape[1:], F32)

    side_jobs = []
    for ins, out in zip(cast_in, cast_out):
        col = 0
        for r in ins:
            for c0 in range(0, r.shape[1], SIDE_COLS):
                side_jobs.append(functools.partial(_cast_cols, r, out, c0, col + c0))
            col += r.shape[1]
    for (w_ref, _), acc_ref in zip(mv_in, mv_acc):
        for c0 in range(0, w_ref.shape[1], SIDE_COLS):
            side_jobs.append(functools.partial(_matvec_cols, cb_ref, w_ref, acc_ref, c0))
    proj = [[None] * ROW_SPLIT for _ in range(4)]
    n_gaps = 4 * ROW_SPLIT - 1
    rows_per = tm // ROW_SPLIT
    for n, (half, k) in enumerate((half, k) for half in range(ROW_SPLIT) for k in range(4)):
        w_ref = (wc_ref, wu_ref, wb_ref, wz_ref)[k]
        rows = slice(half * rows_per, (half + 1) * rows_per)
        proj[k][half] = jnp.dot(h_scr[rows, :], w_ref[...], preferred_element_type=F32)
        if n < n_gaps:
            for job in side_jobs[n::n_gaps]:
                job()
    c_proj, u_proj, b_proj, z_proj = [jnp.concatenate(p, axis=0) for p in proj]

    cu = c_proj * u_proj
    w0 = cw_ref[0:1, :]
    w1 = cw_ref[1:2, :]
    w2 = cw_ref[2:3, :]
    conv = w0 * pltpu.roll(cu, 2, axis=0) + w1 * pltpu.roll(cu, 1, axis=0) + w2 * cu
    gate = b_proj * _silu(z_proj)
    y_ref[...] = (gate * conv).astype(y_ref.dtype)

    prev = halo_scr[j]
    top = cu[0:SUBLANES]
    row = lax.broadcasted_iota(jnp.int32, top.shape, 0)
    back1 = jnp.where(row < 1, pltpu.roll(prev, 1, axis=0), pltpu.roll(top, 1, axis=0))
    back2 = jnp.where(row < 2, pltpu.roll(prev, 2, axis=0), pltpu.roll(top, 2, axis=0))
    conv_top = w0 * back2 + w1 * back1 + w2 * top
    y_ref[0:SUBLANES, :] = (gate[0:SUBLANES] * conv_top).astype(y_ref.dtype)
    halo_scr[j] = cu[tm - SUBLANES:tm]

    if n_matvec:
        @pl.when(step == last_step)
        def _():
            for (_, b_ref), o_ref, acc_ref in zip(mv_in, mv_out, mv_acc):
                o_ref[...] = jnp.sum(acc_ref[...], axis=0, keepdims=True) + b_ref[...]


CONV_TM, CONV_TN = 1024, 256
BF16_ROWS = 16


def _side_rows(s, c, k):
    steps = (s // CONV_TM) * (c // CONV_TN)
    rows = k // steps
    return rows if rows * steps == k and rows % BF16_ROWS == 0 else 0


def _conv_in(x, g, shift, scale, w_in, conv_w, cast_jobs=(), matvec_jobs=(), cb=None):
    tm, tn = CONV_TM, CONV_TN
    s, d = x.shape
    c = conv_w.shape[1]
    nj = c // tn
    vec = pl.BlockSpec((1, d), lambda i, j: (0, 0))
    w_specs = [pl.BlockSpec((d, tn), lambda i, j, k=k: (0, k * nj + j)) for k in range(4)]
    row_block = lambda rows, n: pl.BlockSpec((rows, n), lambda i, j: (i * nj + j, 0))
    whole = lambda n: pl.BlockSpec((1, n), lambda i, j: (0, 0))

    side_in, side_in_specs = [], []
    out_shapes = [jax.ShapeDtypeStruct((s, c), BF16)]
    out_specs = [pl.BlockSpec((tm, tn), lambda i, j: (i, j))]
    scratch = [pltpu.VMEM((tm, d), F32), pltpu.SemaphoreType.DMA(()),
               pltpu.VMEM((tm, d), BF16), pltpu.VMEM((tm, LANES), F32),
               pltpu.VMEM((nj, SUBLANES, tn), F32)]
    for job in cast_jobs:
        k = job[0].shape[0]
        rows = _side_rows(s, c, k)
        assert rows and all(a.shape[0] == k and a.shape[1] % LANES == 0 for a in job)
        side_in += list(job)
        side_in_specs += [row_block(rows, a.shape[1]) for a in job]
        n = sum(a.shape[1] for a in job)
        out_shapes.append(jax.ShapeDtypeStruct((k, n), BF16))
        out_specs.append(row_block(rows, n))
    if matvec_jobs:
        k = cb.shape[0]
        rows = _side_rows(s, c, k)
        assert rows
        side_in.append(cb)
        side_in_specs.append(row_block(rows, LANES))
    for w, b in matvec_jobs:
        n = w.shape[1]
        assert w.shape[0] == k and n % LANES == 0
        side_in += [w, b.reshape(1, n)]
        side_in_specs += [row_block(rows, n), whole(n)]
        out_shapes.append(jax.ShapeDtypeStruct((1, n), F32))
        out_specs.append(whole(n))
        scratch.append(pltpu.VMEM((SUBLANES, n), F32))

    body = functools.partial(_conv_in_kernel, cast_widths=tuple(tuple(a.shape[1] for a in job)
                                                                for job in cast_jobs),
                             n_matvec=len(matvec_jobs))
    outs = pl.pallas_call(
        body,
        grid=(s // tm, nj),
        in_specs=[pl.BlockSpec(memory_space=pl.ANY), vec, vec, vec, *w_specs,
                  pl.BlockSpec((CONV_WIDTH, tn), lambda i, j: (0, j)), *side_in_specs],
        out_specs=out_specs,
        out_shape=out_shapes,
        scratch_shapes=scratch,
        compiler_params=_params("arbitrary", "arbitrary"),
        name="conv_in_proj",
    )(x, g, shift, scale, w_in, w_in, w_in, w_in, conv_w, *side_in)
    n_cast = len(cast_jobs)
    return outs[0], list(outs[1:1 + n_cast]), list(outs[1 + n_cast:])


def _out_proj_kernel(y_ref, w_ref, x_ref, gate_ref, o_ref):
    acc = jnp.dot(y_ref[...], w_ref[...], preferred_element_type=F32)
    o_ref[...] = x_ref[...] + gate_ref[...] * acc


def _out_proj(y, w, x, gate, tm=1024, tn=1024):
    s, k = y.shape
    d = w.shape[1]
    return pl.pallas_call(
        _out_proj_kernel,
        grid=(s // tm, d // tn),
        in_specs=[pl.BlockSpec((tm, k), lambda i, j: (i, 0)),
                  pl.BlockSpec((k, tn), lambda i, j: (0, j)),
                  pl.BlockSpec((tm, tn), lambda i, j: (i, j)),
                  pl.BlockSpec((1, tn), lambda i, j: (0, j))],
        out_specs=pl.BlockSpec((tm, tn), lambda i, j: (i, j)),
        out_shape=jax.ShapeDtypeStruct((s, d), F32),
        compiler_params=_params("arbitrary", "arbitrary"),
        name="out_proj",
    )(y, w, x, gate)


def _out_proj_norm_kernel(y_ref, w_ref, x_ref, gate_ref, gf_ref, o_ref):
    acc = jnp.dot(y_ref[...], w_ref[...], preferred_element_type=F32)
    x2 = x_ref[...] + gate_ref[...] * acc
    ms = jnp.mean(x2 * x2, axis=-1, keepdims=True)
    o_ref[...] = (x2 * lax.rsqrt(ms + EPS)) * gf_ref[...]


def _out_proj_norm(y, w, x, gate, g_final, tm=256):
    s, k = y.shape
    d = w.shape[1]
    vec = pl.BlockSpec((1, d), lambda i: (0, 0))
    return pl.pallas_call(
        _out_proj_norm_kernel,
        grid=(s // tm,),
        in_specs=[pl.BlockSpec((tm, k), lambda i: (i, 0)),
                  pl.BlockSpec((k, d), lambda i: (0, 0), pipeline_mode=pl.Buffered(1)),
                  pl.BlockSpec((tm, d), lambda i: (i, 0)), vec, vec],
        out_specs=pl.BlockSpec((tm, d), lambda i: (i, 0)),
        out_shape=jax.ShapeDtypeStruct((s, d), F32),
        compiler_params=_params("arbitrary", vmem=60 * MIB),
        name="out_proj_final_norm",
    )(y, w, x, gate, g_final)


def _norm_proj_kernel(x_hbm, g_ref, sh_ref, sc_ref, w_ref, o_ref, x_buf, x_sem, h_scr, rstd_scr,
                      *, tail_tiles, scaled_tiles, scale, silu_tiles):
    i = pl.program_id(0)
    j = pl.program_id(1)

    @pl.when(j == 0)
    def _():
        _await_row_tile(x_hbm, x_buf, x_sem, i)
        _rms_stats(x_buf, rstd_scr)
        _modnorm_apply(x_buf, rstd_scr, g_ref, sh_ref, sc_ref, 1 if tail_tiles else 0, h_scr)
        if not tail_tiles:
            _prefetch_next_row_tile(x_hbm, x_buf, x_sem, i)

    if tail_tiles:
        @pl.when(j == tail_tiles)
        def _():
            _modnorm_apply(x_buf, rstd_scr, g_ref, sh_ref, sc_ref, 0, h_scr)
            _prefetch_next_row_tile(x_hbm, x_buf, x_sem, i)

    gated = (j >= tail_tiles + silu_tiles[0]) & (j < tail_tiles + silu_tiles[1])

    @pl.when(gated)
    def _():
        acc = jnp.dot(h_scr[...], w_ref[...], preferred_element_type=F32)
        o_ref[...] = _silu(acc).astype(o_ref.dtype)

    @pl.when(jnp.logical_not(gated))
    def _():
        acc = jnp.dot(h_scr[...], w_ref[...], preferred_element_type=F32)
        if scaled_tiles:
            scaled = (j >= tail_tiles) & (j < tail_tiles + scaled_tiles)
            acc = acc * jnp.where(scaled, scale, 1.0)
        o_ref[...] = acc.astype(o_ref.dtype)


def _norm_proj(x, g, shift, scale, w, tail_cols=0, scaled_cols=0, col_scale=1.0,
               silu_cols=(0, 0), tm=1024, tn=1024):
    s, d = x.shape
    n = w.shape[1]
    n_mod = g.shape[0]
    assert scaled_cols % tn == 0 and tail_cols % tn == 0 and n_mod == (2 if tail_cols else 1)
    assert all(c % tn == 0 and scaled_cols <= c <= n - tail_cols for c in silu_cols)
    n_tiles, tail_tiles = n // tn, tail_cols // tn
    col_tile = lambda j: (j + n_tiles - tail_tiles) % n_tiles
    vec = pl.BlockSpec((n_mod, d), lambda i, j: (0, 0))
    body = functools.partial(_norm_proj_kernel, tail_tiles=tail_tiles,
                             scaled_tiles=scaled_cols // tn, scale=col_scale,
                             silu_tiles=tuple(c // tn for c in silu_cols))
    return pl.pallas_call(
        body,
        grid=(s // tm, n_tiles),
        in_specs=[pl.BlockSpec(memory_space=pl.ANY), vec, vec, vec,
                  pl.BlockSpec((d, tn), lambda i, j: (0, col_tile(j)))],
        out_specs=pl.BlockSpec((tm, tn), lambda i, j: (i, col_tile(j))),
        out_shape=jax.ShapeDtypeStruct((s, n), BF16),
        scratch_shapes=[pltpu.VMEM((tm, d), F32), pltpu.SemaphoreType.DMA(()),
                        pltpu.VMEM((tm, d), BF16), pltpu.VMEM((tm, LANES), F32)],
        compiler_params=_params("arbitrary", "arbitrary"),
        name="norm_proj",
    )(x, g, shift, scale, w)


def _t5_bucket(dist):
    max_exact = N_BUCKETS // 2
    d = jnp.maximum(dist, 0)
    d_f = jnp.maximum(d, 1).astype(F32)
    large = max_exact + (jnp.log(d_f / max_exact) / math.log(MAX_DISTANCE / max_exact)
                         * (N_BUCKETS - max_exact)).astype(jnp.int32)
    large = jnp.minimum(large, N_BUCKETS - 1)
    return jnp.where(d < max_exact, d, large)


HEAD_ORDER = tuple(range(0, GROUP, 2)) + tuple(range(1, GROUP, 2))


def _bias_table_kernel(rb_ref, bucket_ref, o_ref):
    g = pl.program_id(0)
    bucket = bucket_ref[...]
    b = lax.broadcasted_iota(jnp.int32, (2 * BLOCK, BLOCK), 0)
    a = lax.broadcasted_iota(jnp.int32, (2 * BLOCK, BLOCK), 1)
    dist = a + BLOCK - b
    band = (dist >= 0) & (dist < BLOCK)
    band_first = band & (b >= BLOCK)
    for c, hh in enumerate(HEAD_ORDER):
        by_dist = jnp.zeros(bucket.shape, F32)
        for k in range(N_BUCKETS):
            by_dist = jnp.where(bucket == k, rb_ref[k, g * GROUP + hh] * LOG2E, by_dist)
        rows = jnp.concatenate([by_dist] * (2 * BLOCK // SUBLANES), axis=0)
        tile = pltpu.roll(rows, 0, axis=1, stride=1, stride_axis=0)
        cols = slice(c * BLOCK, (c + 1) * BLOCK)
        o_ref[0, 0, :, cols] = jnp.where(band_first, tile, -jnp.inf)
        o_ref[1, 0, :, cols] = jnp.where(band, tile, -jnp.inf)


def _bias_table(rel_bias):
    n_kv = rel_bias.shape[1] // GROUP
    dist_bucket = jnp.broadcast_to(_t5_bucket(jnp.arange(BLOCK)), (SUBLANES, BLOCK))
    return pl.pallas_call(
        _bias_table_kernel,
        grid=(n_kv,),
        in_specs=[pl.BlockSpec(memory_space=pltpu.SMEM),
                  pl.BlockSpec((SUBLANES, BLOCK), lambda g: (0, 0))],
        out_specs=pl.BlockSpec((2, 1, 2 * BLOCK, GROUP * BLOCK), lambda g: (0, g, 0, 0)),
        out_shape=jax.ShapeDtypeStruct((2, n_kv, 2 * BLOCK, GROUP * BLOCK), F32),
        compiler_params=_params("arbitrary"),
        name="bias_table",
    )(rel_bias, dist_bucket.astype(jnp.int32))


ATTN_BLOCKS = 2


def _attn_kernel(sink_ref, qz_ref, kvp_ref, kvc_ref, bias_ref, o_ref):
    n_kv = bias_ref.shape[1]
    attn_dim, kv_dim = o_ref.shape[1], n_kv * HEAD_DIM
    for blk in range(ATTN_BLOCKS):
        rows = slice(blk * BLOCK, (blk + 1) * BLOCK)
        kv_prev = kvp_ref if blk == 0 else kvc_ref.at[(blk - 1) * BLOCK:blk * BLOCK, :]
        kv_cur = kvc_ref.at[rows, :]
        entry = jnp.where(pl.program_id(0) == 0, 0, 1) if blk == 0 else 1
        _attn_block(sink_ref, qz_ref.at[rows, :attn_dim], qz_ref.at[rows, attn_dim:],
                    kv_prev.at[:, :kv_dim], kv_cur.at[:, :kv_dim],
                    kv_prev.at[:, kv_dim:], kv_cur.at[:, kv_dim:],
                    bias_ref.at[entry], o_ref.at[rows, :])


def _attn_block(sink_ref, q_ref, z_ref, kp_ref, kc_ref, vp_ref, vc_ref, bias_ref, o_ref):
    n_kv = bias_ref.shape[0]
    ones_rows = jnp.ones((BF16_ROWS, 2 * BLOCK), BF16)
    low_half = lax.broadcasted_iota(jnp.int32, (2 * BLOCK, LANES), 1) < HEAD_DIM
    nt_dot = functools.partial(lax.dot_general, dimension_numbers=(((1,), (1,)), ((), ())),
                               preferred_element_type=F32)
    tiles = GROUP // 2
    for g in range(n_kv):
        heads = [g * GROUP + hh for hh in HEAD_ORDER]
        kv_cols = slice((g // 2) * LANES, (g // 2 + 1) * LANES)
        k2 = jnp.concatenate([kp_ref[:, kv_cols], kc_ref[:, kv_cols]], axis=0)
        v2_t = jnp.concatenate([vp_ref[:, kv_cols].T, vc_ref[:, kv_cols].T], axis=1)
        v2_t = jnp.concatenate([v2_t, ones_rows], axis=0)
        k_other = pltpu.roll(k2, HEAD_DIM, axis=1)
        k_low, k_high = (k2, k_other) if g % 2 == 0 else (k_other, k2)
        k_low = jnp.where(low_half, k_low, jnp.zeros_like(k_low))
        k_high = jnp.where(low_half, jnp.zeros_like(k_high), k_high)
        group_cols = slice(g * tiles * LANES, (g + 1) * tiles * LANES)
        q_slab = q_ref[:, group_cols]
        q_rows = jnp.concatenate([q_slab[:, t * LANES:(t + 1) * LANES] for t in range(tiles)],
                                 axis=0)
        s_t = jnp.concatenate([nt_dot(k_low, q_rows), nt_dot(k_high, q_rows)], axis=1)
        s_t = s_t + bias_ref[g]
        sink = jnp.concatenate([jnp.full((1, BLOCK), sink_ref[h] * LOG2E, F32) for h in heads],
                               axis=1)
        m = jnp.maximum(jnp.max(s_t, axis=0, keepdims=True), sink)
        p = jnp.exp2(s_t - m).astype(BF16)
        o2_t = jnp.dot(v2_t, p, preferred_element_type=F32)
        den = o2_t[2 * HEAD_DIM:2 * HEAD_DIM + 1] + jnp.exp2(sink - m)
        o_t = (o2_t[:HEAD_DIM] if g % 2 == 0 else o2_t[HEAD_DIM:2 * HEAD_DIM]) / den
        o_t = o_t.astype(o_ref.dtype)
        out_tiles = []
        for t in range(tiles):
            both = jnp.concatenate([o_t[:, t * BLOCK:(t + 1) * BLOCK],
                                    o_t[:, (tiles + t) * BLOCK:(tiles + t + 1) * BLOCK]], axis=0)
            out_tiles.append(both.T)
        o_ref[:, group_cols] = jnp.concatenate(out_tiles, axis=1) * z_ref[:, group_cols]


def _attention(qz, qz_blk, kv, kv_blk, bias, sinks):
    s = qz.shape[0]
    n_kv = bias.shape[1]
    kv_dim = n_kv * HEAD_DIM
    attn_dim = kv_dim * GROUP
    assert n_kv % 2 == 0 and GROUP % 2 == 0 and 2 * HEAD_DIM == LANES and BLOCK == LANES
    rows = ATTN_BLOCKS * BLOCK
    assert s % rows == 0
    before = lambda i: jnp.maximum(i * ATTN_BLOCKS - 1, 0)
    return pl.pallas_call(
        _attn_kernel,
        grid=(s // rows,),
        in_specs=[pl.BlockSpec(memory_space=pltpu.SMEM),
                  pl.BlockSpec((rows, 2 * attn_dim), lambda i: (i, qz_blk)),
                  pl.BlockSpec((BLOCK, 2 * kv_dim), lambda i: (before(i), kv_blk)),
                  pl.BlockSpec((rows, 2 * kv_dim), lambda i: (i, kv_blk)),
                  pl.BlockSpec(bias.shape, lambda i: (0, 0, 0, 0),
                               pipeline_mode=pl.Buffered(1))],
        out_specs=pl.BlockSpec((rows, attn_dim), lambda i: (i, 0)),
        out_shape=jax.ShapeDtypeStruct((s, attn_dim), BF16),
        compiler_params=_params("arbitrary"),
        name="swa_attention",
    )(sinks, qz, kv, kv, bias)


def _trunk(x, c, g_a, w_mod_a, b_mod_a, w_a_in, conv_a, w_a_out, g_kv, w_mod_kv, b_mod_kv,
           w_kv, rel_bias, g_b, w_mod_b, b_mod_b, w_b_in, sinks_b, w_b_out, g_final):
    s, d = x.shape
    n_a, n_b = w_a_in.shape[0], w_b_in.shape[0]
    assert n_b >= 1 and s % BLOCK == 0
    row = lambda v: v.reshape(1, -1)
    cb = jnp.broadcast_to(c.reshape(d, 1), (d, LANES))

    side_operands = (w_a_out[-1], w_b_in[0], w_kv, w_b_out[0], w_mod_kv, w_mod_b[0]) if n_a else ()
    stream = n_a > 0 and all(_side_rows(s, conv_a.shape[2], a.shape[0]) for a in side_operands)
    w_a_out_bf = w_cat = w_b_out_bf = mod_kv = mod_b = None
    for i in range(n_a):
        shift, scale, gate = jnp.split(_adaln(cb, w_mod_a[i], b_mod_a[i]), 3, axis=-1)
        args = (x, row(g_a[i]), shift, scale, w_a_in[i].astype(BF16), conv_a[i])
        if stream and i == n_a - 1:
            y, (w_a_out_bf, w_cat, w_b_out_bf), (mod_kv, mod_b) = _conv_in(
                *args, cast_jobs=((w_a_out[i],), (w_b_in[0], w_kv), (w_b_out[0],)),
                matvec_jobs=((w_mod_kv, b_mod_kv), (w_mod_b[0], b_mod_b[0])), cb=cb)
        else:
            y, _, _ = _conv_in(*args)
            w_a_out_bf = w_a_out[i].astype(BF16)
        x = _out_proj(y, w_a_out_bf, x, gate)

    if mod_kv is None:
        mod_kv = _adaln(cb, w_mod_kv, b_mod_kv)
    shift_kv, scale_kv = jnp.split(mod_kv, 2, axis=-1)
    bias = _bias_table(rel_bias)
    attn_dim = w_b_in.shape[2] // 2
    kv_dim = w_kv.shape[1] // 2
    kv = None
    for i in range(n_b):
        if i > 0 or mod_b is None:
            mod_b = _adaln(cb, w_mod_b[i], b_mod_b[i])
        shift, scale, gate = jnp.split(mod_b, 3, axis=-1)
        q_scale = dict(scaled_cols=attn_dim, col_scale=LOG2E * HEAD_DIM ** -0.5,
                       silu_cols=(attn_dim, 2 * attn_dim))
        if i == 0:
            if w_cat is None:
                w_cat = jnp.concatenate([w_b_in[i], w_kv], axis=1).astype(BF16)
            qz = _norm_proj(x, jnp.stack([g_b[i], g_kv]), jnp.concatenate([shift, shift_kv]),
                            jnp.concatenate([scale, scale_kv]), w_cat,
                            tail_cols=2 * kv_dim, **q_scale)
            kv, kv_blk = qz, attn_dim // kv_dim
        else:
            qz = _norm_proj(x, row(g_b[i]), shift, scale, w_b_in[i].astype(BF16), **q_scale)
        a = _attention(qz, 0, kv, kv_blk, bias, sinks_b[i])
        w_out = w_b_out_bf if i == 0 and w_b_out_bf is not None else w_b_out[i].astype(BF16)
        if i + 1 < n_b:
            x = _out_proj(a, w_out, x, gate)
        else:
            x = _out_proj_norm(a, w_out, x, gate, row(g_final))
    return x


def kernel(x, c, g_a, w_mod_a, b_mod_a, w_a_in, conv_a, w_a_out, g_kv, w_mod_kv, b_mod_kv, w_kv,
           rel_bias, g_b, w_mod_b, b_mod_b, w_b_in, sinks_b, w_b_out, g_final):
    outs = [_trunk(x[b], c[b], g_a, w_mod_a, b_mod_a, w_a_in, conv_a, w_a_out, g_kv, w_mod_kv,
                   b_mod_kv, w_kv, rel_bias, g_b, w_mod_b, b_mod_b, w_b_in, sinks_b, w_b_out,
                   g_final) for b in range(x.shape[0])]
    return jnp.stack(outs, axis=0)
```

```python
import functools
import math

import jax
import jax.numpy as jnp
from jax import lax
from jax.experimental import pallas as pl
from jax.experimental.pallas import tpu as pltpu

HEAD_DIM = 64
GROUP = 8
BLOCK = 128
N_BUCKETS = 32
MAX_DISTANCE = 128
CONV_WIDTH = 3
EPS = 1e-6
LOG2E = math.log2(math.e)

LANES = 128
SUBLANES = 8
MIB = 1024 * 1024
VMEM_LIMIT = 56 * MIB

F32 = jnp.float32
BF16 = jnp.bfloat16
STATS_ROWS = 512
NORM_ROWS = 256


def _params(*semantics, vmem=VMEM_LIMIT):
    return pltpu.CompilerParams(dimension_semantics=semantics, vmem_limit_bytes=vmem)


def _silu(v):
    return v * jax.nn.sigmoid(v)


def _adaln_kernel(cb_ref, w_ref, b_ref, o_ref):
    ca = _silu(cb_ref[...])
    for g in range(o_ref.shape[1] // LANES):
        cols = slice(g * LANES, (g + 1) * LANES)
        o_ref[:, cols] = jnp.sum(w_ref[:, cols] * ca, axis=0, keepdims=True) + b_ref[:, cols]


def _adaln(cb, w, b, tn=1024):
    k, n = w.shape
    return pl.pallas_call(
        _adaln_kernel,
        grid=(n // tn,),
        in_specs=[
            pl.BlockSpec((k, LANES), lambda j: (0, 0)),
            pl.BlockSpec((k, tn), lambda j: (0, j)),
            pl.BlockSpec((1, tn), lambda j: (0, j)),
        ],
        out_specs=pl.BlockSpec((1, tn), lambda j: (0, j)),
        out_shape=jax.ShapeDtypeStruct((1, n), F32),
        compiler_params=_params("arbitrary"),
        name="adaln_matvec",
    )(cb, w, b.reshape(1, n))


def _col_groups(d):
    return [slice(k * LANES, (k + 1) * LANES) for k in range(d // LANES)]


def _rms_stats(x_ref, rstd_scr):
    tm, d = x_ref.shape

    def body(r, carry):
        base = pl.multiple_of(r * STATS_ROWS, STATS_ROWS)
        parts = []
        for s in range(STATS_ROWS // SUBLANES):
            xr = x_ref[pl.ds(base + s * SUBLANES, SUBLANES), :]
            sq = xr * xr
            part = sq[:, :LANES]
            for cols in _col_groups(d)[1:]:
                part = part + sq[:, cols]
            parts.append(part)
        acc = jnp.concatenate(parts, axis=0)
        ms = jnp.sum(acc, axis=-1, keepdims=True) * (1.0 / d)
        rstd_scr[pl.ds(base, STATS_ROWS), :] = jnp.broadcast_to(lax.rsqrt(ms + EPS),
                                                               (STATS_ROWS, LANES))
        return carry

    lax.fori_loop(0, tm // STATS_ROWS, body, 0)


def _modnorm_apply(x_ref, rstd_scr, g_ref, sh_ref, sc_ref, mod, h_scr):
    tm, d = x_ref.shape
    vec = slice(mod, mod + 1)

    def body(r, carry):
        base = pl.multiple_of(r * NORM_ROWS, NORM_ROWS)
        gains = [g_ref[vec, cols] * (1.0 + sc_ref[vec, cols]) for cols in _col_groups(d)]
        for s in range(NORM_ROWS // BF16_ROWS):
            rows = pl.ds(base + s * BF16_ROWS, BF16_ROWS)
            rstd = rstd_scr[rows, :]
            for cols, gain in zip(_col_groups(d), gains):
                h_scr[rows, cols] = ((x_ref[rows, cols] * rstd) * gain
                                     + sh_ref[vec, cols]).astype(h_scr.dtype)
        return carry

    lax.fori_loop(0, tm // NORM_ROWS, body, 0)


def _row_tile_copy(x_hbm, x_buf, sem, i):
    tm = x_buf.shape[0]
    return pltpu.make_async_copy(x_hbm.at[pl.ds(i * tm, tm), :], x_buf, sem)


def _await_row_tile(x_hbm, x_buf, sem, i):
    @pl.when(i == 0)
    def _():
        _row_tile_copy(x_hbm, x_buf, sem, 0).start()

    _row_tile_copy(x_hbm, x_buf, sem, i).wait()


def _prefetch_next_row_tile(x_hbm, x_buf, sem, i):
    @pl.when(i + 1 < pl.num_programs(0))
    def _():
        _row_tile_copy(x_hbm, x_buf, sem, i + 1).start()


SIDE_COLS = 2048
ROW_SPLIT = 2


def _cast_cols(in_ref, o_ref, c0, o0):
    n = min(SIDE_COLS, in_ref.shape[1] - c0)
    o_ref[:, o0:o0 + n] = in_ref[:, c0:c0 + n].astype(o_ref.dtype)


def _matvec_cols(cb_ref, w_ref, acc_ref, c0):
    ca = _silu(cb_ref[...])
    for k in range(c0 // LANES, min(c0 + SIDE_COLS, w_ref.shape[1]) // LANES):
        cols = slice(k * LANES, (k + 1) * LANES)
        prod = w_ref[:, cols] * ca
        part = prod[0:SUBLANES]
        for q in range(1, prod.shape[0] // SUBLANES):
            part = part + prod[q * SUBLANES:(q + 1) * SUBLANES]
        acc_ref[:, cols] += part


def _conv_in_kernel(*refs, cast_widths, n_matvec):
    refs = list(refs)
    take = lambda n: [refs.pop(0) for _ in range(n)]
    x_hbm, g_ref, sh_ref, sc_ref, wb_ref, wc_ref, wu_ref, wz_ref, cw_ref = take(9)
    cast_in = [take(len(widths)) for widths in cast_widths]
    cb_ref = take(1)[0] if n_matvec else None
    mv_in = [take(2) for _ in range(n_matvec)]
    y_ref = take(1)[0]
    cast_out = take(len(cast_widths))
    mv_out = take(n_matvec)
    x_buf, x_sem, h_scr, rstd_scr, halo_scr = take(5)
    mv_acc = take(n_matvec)

    i = pl.program_id(0)
    j = pl.program_id(1)
    tm = y_ref.shape[0]
    step = i * pl.num_programs(1) + j
    last_step = pl.num_programs(0) * pl.num_programs(1) - 1

    @pl.when(step == 0)
    def _():
        for acc_ref in mv_acc:
            acc_ref[...] = jnp.zeros(acc_ref.shape, F32)

    @pl.when(j == 0)
    def _():
        _await_row_tile(x_hbm, x_buf, x_sem, i)
        _rms_stats(x_buf, rstd_scr)
        _modnorm_apply(x_buf, rstd_scr, g_ref, sh_ref, sc_ref, 0, h_scr)
        _prefetch_next_row_tile(x_hbm, x_buf, x_sem, i)

    @pl.when(i == 0)
    def _():
        halo_scr[j] = jnp.zeros(halo_scr.shape[1:], F32)

    side_jobs = []
    for ins, out in zip(cast_in, cast_out):
        col = 0
        for r in ins:
            for c0 in range(0, r.shape[1], SIDE_COLS):
                side_jobs.append(functools.partial(_cast_cols, r, out, c0, col + c0))
            col += r.shape[1]
    for (w_ref, _), acc_ref in zip(mv_in, mv_acc):
        for c0 in range(0, w_ref.shape[1], SIDE_COLS):
            side_jobs.append(functools.partial(_matvec_cols, cb_ref, w_ref, acc_ref, c0))
    proj = [[None] * ROW_SPLIT for _ in range(4)]
    n_gaps = 4 * ROW_SPLIT - 1
    rows_per = tm // ROW_SPLIT
    for n, (half, k) in enumerate((half, k) for half in range(ROW_SPLIT) for k in range(4)):
        w_ref = (wc_ref, wu_ref, wb_ref, wz_ref)[k]
        rows = slice(half * rows_per, (half + 1) * rows_per)
        proj[k][half] = jnp.dot(h_scr[rows, :], w_ref[...], preferred_element_type=F32)
        if n < n_gaps:
            for job in side_jobs[n::n_gaps]:
                job()
    c_proj, u_proj, b_proj, z_proj = [jnp.concatenate(p, axis=0) for p in proj]

    cu = c_proj * u_proj
    w0 = cw_ref[0:1, :]
    w1 = cw_ref[1:2, :]
    w2 = cw_ref[2:3, :]
    conv = w0 * pltpu.roll(cu, 2, axis=0) + w1 * pltpu.roll(cu, 1, axis=0) + w2 * cu
    gate = b_proj * _silu(z_proj)
    y_ref[...] = (gate * conv).astype(y_ref.dtype)

    prev = halo_scr[j]
    top = cu[0:SUBLANES]
    row = lax.broadcasted_iota(jnp.int32, top.shape, 0)
    back1 = jnp.where(row < 1, pltpu.roll(prev, 1, axis=0), pltpu.roll(top, 1, axis=0))
    back2 = jnp.where(row < 2, pltpu.roll(prev, 2, axis=0), pltpu.roll(top, 2, axis=0))
    conv_top = w0 * back2 + w1 * back1 + w2 * top
    y_ref[0:SUBLANES, :] = (gate[0:SUBLANES] * conv_top).astype(y_ref.dtype)
    halo_scr[j] = cu[tm - SUBLANES:tm]

    if n_matvec:
        @pl.when(step == last_step)
        def _():
            for (_, b_ref), o_ref, acc_ref in zip(mv_in, mv_out, mv_acc):
                o_ref[...] = jnp.sum(acc_ref[...], axis=0, keepdims=True) + b_ref[...]


CONV_TM, CONV_TN = 1024, 256
BF16_ROWS = 16


def _side_rows(s, c, k):
    steps = (s // CONV_TM) * (c // CONV_TN)
    rows = k // steps
    return rows if rows * steps == k and rows % BF16_ROWS == 0 else 0


def _conv_in(x, g, shift, scale, w_in, conv_w, cast_jobs=(), matvec_jobs=(), cb=None):
    tm, tn = CONV_TM, CONV_TN
    s, d = x.shape
    c = conv_w.shape[1]
    nj = c // tn
    vec = pl.BlockSpec((1, d), lambda i, j: (0, 0))
    w_specs = [pl.BlockSpec((d, tn), lambda i, j, k=k: (0, k * nj + j)) for k in range(4)]
    row_block = lambda rows, n: pl.BlockSpec((rows, n), lambda i, j: (i * nj + j, 0))
    whole = lambda n: pl.BlockSpec((1, n), lambda i, j: (0, 0))

    side_in, side_in_specs = [], []
    out_shapes = [jax.ShapeDtypeStruct((s, c), BF16)]
    out_specs = [pl.BlockSpec((tm, tn), lambda i, j: (i, j))]
    scratch = [pltpu.VMEM((tm, d), F32), pltpu.SemaphoreType.DMA(()),
               pltpu.VMEM((tm, d), BF16), pltpu.VMEM((tm, LANES), F32),
               pltpu.VMEM((nj, SUBLANES, tn), F32)]
    for job in cast_jobs:
        k = job[0].shape[0]
        rows = _side_rows(s, c, k)
        assert rows and all(a.shape[0] == k and a.shape[1] % LANES == 0 for a in job)
        side_in += list(job)
        side_in_specs += [row_block(rows, a.shape[1]) for a in job]
        n = sum(a.shape[1] for a in job)
        out_shapes.append(jax.ShapeDtypeStruct((k, n), BF16))
        out_specs.append(row_block(rows, n))
    if matvec_jobs:
        k = cb.shape[0]
        rows = _side_rows(s, c, k)
        assert rows
        side_in.append(cb)
        side_in_specs.append(row_block(rows, LANES))
    for w, b in matvec_jobs:
        n = w.shape[1]
        assert w.shape[0] == k and n % LANES == 0
        side_in += [w, b.reshape(1, n)]
        side_in_specs += [row_block(rows, n), whole(n)]
        out_shapes.append(jax.ShapeDtypeStruct((1, n), F32))
        out_specs.append(whole(n))
        scratch.append(pltpu.VMEM((SUBLANES, n), F32))

    body = functools.partial(_conv_in_kernel, cast_widths=tuple(tuple(a.shape[1] for a in job)
                                                                for job in cast_jobs),
                             n_matvec=len(matvec_jobs))
    outs = pl.pallas_call(
        body,
        grid=(s // tm, nj),
        in_specs=[pl.BlockSpec(memory_space=pl.ANY), vec, vec, vec, *w_specs,
                  pl.BlockSpec((CONV_WIDTH, tn), lambda i, j: (0, j)), *side_in_specs],
        out_specs=out_specs,
        out_shape=out_shapes,
        scratch_shapes=scratch,
        compiler_params=_params("arbitrary", "arbitrary"),
        name="conv_in_proj",
    )(x, g, shift, scale, w_in, w_in, w_in, w_in, conv_w, *side_in)
    n_cast = len(cast_jobs)
    return outs[0], list(outs[1:1 + n_cast]), list(outs[1 + n_cast:])


MXU_COLS = 256


def _out_proj_kernel(y_ref, w_ref, x_ref, gate_ref, o_ref):
    for c0 in range(0, o_ref.shape[1], MXU_COLS):
        cols = slice(c0, c0 + MXU_COLS)
        acc = jnp.dot(y_ref[...], w_ref[:, cols], preferred_element_type=F32)
        o_ref[:, cols] = x_ref[:, cols] + gate_ref[:, cols] * acc


def _out_proj(y, w, x, gate, tm=1024, tn=1024):
    s, k = y.shape
    d = w.shape[1]
    return pl.pallas_call(
        _out_proj_kernel,
        grid=(s // tm, d // tn),
        in_specs=[pl.BlockSpec((tm, k), lambda i, j: (i, 0)),
                  pl.BlockSpec((k, tn), lambda i, j: (0, j)),
                  pl.BlockSpec((tm, tn), lambda i, j: (i, j)),
                  pl.BlockSpec((1, tn), lambda i, j: (0, j))],
        out_specs=pl.BlockSpec((tm, tn), lambda i, j: (i, j)),
        out_shape=jax.ShapeDtypeStruct((s, d), F32),
        compiler_params=_params("arbitrary", "arbitrary"),
        name="out_proj",
    )(y, w, x, gate)


def _out_proj_norm_kernel(y_ref, w_ref, x_ref, gate_ref, gf_ref, o_ref):
    tm, d = o_ref.shape
    ssq = jnp.zeros((tm, LANES), F32)
    for c0 in range(0, d, MXU_COLS):
        cols = slice(c0, c0 + MXU_COLS)
        acc = jnp.dot(y_ref[...], w_ref[:, cols], preferred_element_type=F32)
        x2 = x_ref[:, cols] + gate_ref[:, cols] * acc
        o_ref[:, cols] = x2
        for l0 in range(0, MXU_COLS, LANES):
            ssq = ssq + x2[:, l0:l0 + LANES] * x2[:, l0:l0 + LANES]
    rstd = lax.rsqrt(jnp.sum(ssq, axis=-1, keepdims=True) * (1.0 / d) + EPS)
    o_ref[...] = (o_ref[...] * rstd) * gf_ref[...]


def _out_proj_norm(y, w, x, gate, g_final, tm=256):
    s, k = y.shape
    d = w.shape[1]
    vec = pl.BlockSpec((1, d), lambda i: (0, 0))
    return pl.pallas_call(
        _out_proj_norm_kernel,
        grid=(s // tm,),
        in_specs=[pl.BlockSpec((tm, k), lambda i: (i, 0)),
                  pl.BlockSpec((k, d), lambda i: (0, 0), pipeline_mode=pl.Buffered(1)),
                  pl.BlockSpec((tm, d), lambda i: (i, 0)), vec, vec],
        out_specs=pl.BlockSpec((tm, d), lambda i: (i, 0)),
        out_shape=jax.ShapeDtypeStruct((s, d), F32),
        compiler_params=_params("arbitrary", vmem=60 * MIB),
        name="out_proj_final_norm",
    )(y, w, x, gate, g_final)


def _norm_proj_kernel(x_hbm, g_ref, sh_ref, sc_ref, w_ref, o_ref, x_buf, x_sem, h_scr, rstd_scr,
                      *, tail_tiles, scaled_tiles, scale, silu_tiles):
    i = pl.program_id(0)
    j = pl.program_id(1)

    @pl.when(j == 0)
    def _():
        _await_row_tile(x_hbm, x_buf, x_sem, i)
        _rms_stats(x_buf, rstd_scr)
        _modnorm_apply(x_buf, rstd_scr, g_ref, sh_ref, sc_ref, 1 if tail_tiles else 0, h_scr)
        if not tail_tiles:
            _prefetch_next_row_tile(x_hbm, x_buf, x_sem, i)

    if tail_tiles:
        @pl.when(j == tail_tiles)
        def _():
            _modnorm_apply(x_buf, rstd_scr, g_ref, sh_ref, sc_ref, 0, h_scr)
            _prefetch_next_row_tile(x_hbm, x_buf, x_sem, i)

    gated = (j >= tail_tiles + silu_tiles[0]) & (j < tail_tiles + silu_tiles[1])

    strips = [slice(c0, c0 + MXU_COLS) for c0 in range(0, o_ref.shape[1], MXU_COLS)]

    @pl.when(gated)
    def _():
        for cols in strips:
            acc = jnp.dot(h_scr[...], w_ref[:, cols], preferred_element_type=F32)
            o_ref[:, cols] = _silu(acc).astype(o_ref.dtype)

    @pl.when(jnp.logical_not(gated))
    def _():
        mult = None
        if scaled_tiles:
            scaled = (j >= tail_tiles) & (j < tail_tiles + scaled_tiles)
            mult = jnp.where(scaled, scale, 1.0)
        for cols in strips:
            acc = jnp.dot(h_scr[...], w_ref[:, cols], preferred_element_type=F32)
            o_ref[:, cols] = (acc if mult is None else acc * mult).astype(o_ref.dtype)


def _norm_proj(x, g, shift, scale, w, tail_cols=0, scaled_cols=0, col_scale=1.0,
               silu_cols=(0, 0), tm=1024, tn=1024):
    s, d = x.shape
    n = w.shape[1]
    n_mod = g.shape[0]
    assert scaled_cols % tn == 0 and tail_cols % tn == 0 and n_mod == (2 if tail_cols else 1)
    assert all(c % tn == 0 and scaled_cols <= c <= n - tail_cols for c in silu_cols)
    n_tiles, tail_tiles = n // tn, tail_cols // tn
    col_tile = lambda j: (j + n_tiles - tail_tiles) % n_tiles
    vec = pl.BlockSpec((n_mod, d), lambda i, j: (0, 0))
    body = functools.partial(_norm_proj_kernel, tail_tiles=tail_tiles,
                             scaled_tiles=scaled_cols // tn, scale=col_scale,
                             silu_tiles=tuple(c // tn for c in silu_cols))
    return pl.pallas_call(
        body,
        grid=(s // tm, n_tiles),
        in_specs=[pl.BlockSpec(memory_space=pl.ANY), vec, vec, vec,
                  pl.BlockSpec((d, tn), lambda i, j: (0, col_tile(j)))],
        out_specs=pl.BlockSpec((tm, tn), lambda i, j: (i, col_tile(j))),
        out_shape=jax.ShapeDtypeStruct((s, n), BF16),
        scratch_shapes=[pltpu.VMEM((tm, d), F32), pltpu.SemaphoreType.DMA(()),
                        pltpu.VMEM((tm, d), BF16), pltpu.VMEM((tm, LANES), F32)],
        compiler_params=_params("arbitrary", "arbitrary"),
        name="norm_proj",
    )(x, g, shift, scale, w)


def _t5_bucket(dist):
    max_exact = N_BUCKETS // 2
    d = jnp.maximum(dist, 0)
    d_f = jnp.maximum(d, 1).astype(F32)
    large = max_exact + (jnp.log(d_f / max_exact) / math.log(MAX_DISTANCE / max_exact)
                         * (N_BUCKETS - max_exact)).astype(jnp.int32)
    large = jnp.minimum(large, N_BUCKETS - 1)
    return jnp.where(d < max_exact, d, large)


HEAD_ORDER = tuple(range(0, GROUP, 2)) + tuple(range(1, GROUP, 2))


def _bias_table_kernel(rb_ref, bucket_ref, o_ref):
    g = pl.program_id(0)
    bucket = bucket_ref[...]
    b = lax.broadcasted_iota(jnp.int32, (2 * BLOCK, BLOCK), 0)
    a = lax.broadcasted_iota(jnp.int32, (2 * BLOCK, BLOCK), 1)
    dist = a + BLOCK - b
    band = (dist >= 0) & (dist < BLOCK)
    band_first = band & (b >= BLOCK)
    for c, hh in enumerate(HEAD_ORDER):
        by_dist = jnp.zeros(bucket.shape, F32)
        for k in range(N_BUCKETS):
            by_dist = jnp.where(bucket == k, rb_ref[k, g * GROUP + hh] * LOG2E, by_dist)
        rows = jnp.concatenate([by_dist] * (2 * BLOCK // SUBLANES), axis=0)
        tile = pltpu.roll(rows, 0, axis=1, stride=1, stride_axis=0)
        cols = slice(c * BLOCK, (c + 1) * BLOCK)
        o_ref[0, 0, :, cols] = jnp.where(band_first, tile, -jnp.inf)
        o_ref[1, 0, :, cols] = jnp.where(band, tile, -jnp.inf)


def _bias_table(rel_bias):
    n_kv = rel_bias.shape[1] // GROUP
    dist_bucket = jnp.broadcast_to(_t5_bucket(jnp.arange(BLOCK)), (SUBLANES, BLOCK))
    return pl.pallas_call(
        _bias_table_kernel,
        grid=(n_kv,),
        in_specs=[pl.BlockSpec(memory_space=pltpu.SMEM),
                  pl.BlockSpec((SUBLANES, BLOCK), lambda g: (0, 0))],
        out_specs=pl.BlockSpec((2, 1, 2 * BLOCK, GROUP * BLOCK), lambda g: (0, g, 0, 0)),
        out_shape=jax.ShapeDtypeStruct((2, n_kv, 2 * BLOCK, GROUP * BLOCK), F32),
        compiler_params=_params("arbitrary"),
        name="bias_table",
    )(rel_bias, dist_bucket.astype(jnp.int32))


ATTN_BLOCKS = 2


def _attn_kernel(sink_ref, qz_ref, kvp_ref, kvc_ref, bias_ref, o_ref):
    n_kv = bias_ref.shape[1]
    attn_dim, kv_dim = o_ref.shape[1], n_kv * HEAD_DIM
    for blk in range(ATTN_BLOCKS):
        rows = slice(blk * BLOCK, (blk + 1) * BLOCK)
        kv_prev = kvp_ref if blk == 0 else kvc_ref.at[(blk - 1) * BLOCK:blk * BLOCK, :]
        kv_cur = kvc_ref.at[rows, :]
        entry = jnp.where(pl.program_id(0) == 0, 0, 1) if blk == 0 else 1
        _attn_block(sink_ref, qz_ref.at[rows, :attn_dim], qz_ref.at[rows, attn_dim:],
                    kv_prev.at[:, :kv_dim], kv_cur.at[:, :kv_dim],
                    kv_prev.at[:, kv_dim:], kv_cur.at[:, kv_dim:],
                    bias_ref.at[entry], o_ref.at[rows, :])


def _attn_block(sink_ref, q_ref, z_ref, kp_ref, kc_ref, vp_ref, vc_ref, bias_ref, o_ref):
    n_kv = bias_ref.shape[0]
    ones_rows = jnp.ones((BF16_ROWS, 2 * BLOCK), BF16)
    low_half = lax.broadcasted_iota(jnp.int32, (2 * BLOCK, LANES), 1) < HEAD_DIM
    nt_dot = functools.partial(lax.dot_general, dimension_numbers=(((1,), (1,)), ((), ())),
                               preferred_element_type=F32)
    tiles = GROUP // 2
    for g in range(n_kv):
        heads = [g * GROUP + hh for hh in HEAD_ORDER]
        kv_cols = slice((g // 2) * LANES, (g // 2 + 1) * LANES)
        k2 = jnp.concatenate([kp_ref[:, kv_cols], kc_ref[:, kv_cols]], axis=0)
        v2_t = jnp.concatenate([vp_ref[:, kv_cols].T, vc_ref[:, kv_cols].T], axis=1)
        v2_t = jnp.concatenate([v2_t, ones_rows], axis=0)
        k_other = pltpu.roll(k2, HEAD_DIM, axis=1)
        k_low, k_high = (k2, k_other) if g % 2 == 0 else (k_other, k2)
        k_low = jnp.where(low_half, k_low, jnp.zeros_like(k_low))
        k_high = jnp.where(low_half, jnp.zeros_like(k_high), k_high)
        group_cols = slice(g * tiles * LANES, (g + 1) * tiles * LANES)
        q_slab = q_ref[:, group_cols]
        q_rows = jnp.concatenate([q_slab[:, t * LANES:(t + 1) * LANES] for t in range(tiles)],
                                 axis=0)
        s_t = jnp.concatenate([nt_dot(k_low, q_rows), nt_dot(k_high, q_rows)], axis=1)
        s_t = s_t + bias_ref[g]
        sink = jnp.concatenate([jnp.full((1, BLOCK), sink_ref[h] * LOG2E, F32) for h in heads],
                               axis=1)
        m = jnp.maximum(jnp.max(s_t, axis=0, keepdims=True), sink)
        p = jnp.exp2(s_t - m).astype(BF16)
        o2_t = jnp.dot(v2_t, p, preferred_element_type=F32)
        den = o2_t[2 * HEAD_DIM:2 * HEAD_DIM + 1] + jnp.exp2(sink - m)
        o_t = (o2_t[:HEAD_DIM] if g % 2 == 0 else o2_t[HEAD_DIM:2 * HEAD_DIM]) / den
        o_t = o_t.astype(o_ref.dtype)
        out_tiles = []
        for t in range(tiles):
            both = jnp.concatenate([o_t[:, t * BLOCK:(t + 1) * BLOCK],
                                    o_t[:, (tiles + t) * BLOCK:(tiles + t + 1) * BLOCK]], axis=0)
            out_tiles.append(both.T)
        o_ref[:, group_cols] = jnp.concatenate(out_tiles, axis=1) * z_ref[:, group_cols]


def _attention(qz, qz_blk, kv, kv_blk, bias, sinks):
    s = qz.shape[0]
    n_kv = bias.shape[1]
    kv_dim = n_kv * HEAD_DIM
    attn_dim = kv_dim * GROUP
    assert n_kv % 2 == 0 and GROUP % 2 == 0 and 2 * HEAD_DIM == LANES and BLOCK == LANES
    rows = ATTN_BLOCKS * BLOCK
    assert s % rows == 0
    before = lambda i: jnp.maximum(i * ATTN_BLOCKS - 1, 0)
    return pl.pallas_call(
        _attn_kernel,
        grid=(s // rows,),
        in_specs=[pl.BlockSpec(memory_space=pltpu.SMEM),
                  pl.BlockSpec((rows, 2 * attn_dim), lambda i: (i, qz_blk)),
                  pl.BlockSpec((BLOCK, 2 * kv_dim), lambda i: (before(i), kv_blk)),
                  pl.BlockSpec((rows, 2 * kv_dim), lambda i: (i, kv_blk)),
                  pl.BlockSpec(bias.shape, lambda i: (0, 0, 0, 0),
                               pipeline_mode=pl.Buffered(1))],
        out_specs=pl.BlockSpec((rows, attn_dim), lambda i: (i, 0)),
        out_shape=jax.ShapeDtypeStruct((s, attn_dim), BF16),
        compiler_params=_params("arbitrary"),
        name="swa_attention",
    )(sinks, qz, kv, kv, bias)


def _trunk(x, c, g_a, w_mod_a, b_mod_a, w_a_in, conv_a, w_a_out, g_kv, w_mod_kv, b_mod_kv,
           w_kv, rel_bias, g_b, w_mod_b, b_mod_b, w_b_in, sinks_b, w_b_out, g_final):
    s, d = x.shape
    n_a, n_b = w_a_in.shape[0], w_b_in.shape[0]
    assert n_b >= 1 and s % BLOCK == 0
    row = lambda v: v.reshape(1, -1)
    cb = jnp.broadcast_to(c.reshape(d, 1), (d, LANES))

    side_operands = (w_a_out[-1], w_b_in[0], w_kv, w_b_out[0], w_mod_kv, w_mod_b[0]) if n_a else ()
    stream = n_a > 0 and all(_side_rows(s, conv_a.shape[2], a.shape[0]) for a in side_operands)
    w_a_out_bf = w_cat = w_b_out_bf = mod_kv = mod_b = None
    for i in range(n_a):
        shift, scale, gate = jnp.split(_adaln(cb, w_mod_a[i], b_mod_a[i]), 3, axis=-1)
        args = (x, row(g_a[i]), shift, scale, w_a_in[i].astype(BF16), conv_a[i])
        if stream and i == n_a - 1:
            y, (w_a_out_bf, w_cat, w_b_out_bf), (mod_kv, mod_b) = _conv_in(
                *args, cast_jobs=((w_a_out[i],), (w_b_in[0], w_kv), (w_b_out[0],)),
                matvec_jobs=((w_mod_kv, b_mod_kv), (w_mod_b[0], b_mod_b[0])), cb=cb)
        else:
            y, _, _ = _conv_in(*args)
            w_a_out_bf = w_a_out[i].astype(BF16)
        x = _out_proj(y, w_a_out_bf, x, gate)

    if mod_kv is None:
        mod_kv = _adaln(cb, w_mod_kv, b_mod_kv)
    shift_kv, scale_kv = jnp.split(mod_kv, 2, axis=-1)
    bias = _bias_table(rel_bias)
    attn_dim = w_b_in.shape[2] // 2
    kv_dim = w_kv.shape[1] // 2
    kv = None
    for i in range(n_b):
        if i > 0 or mod_b is None:
            mod_b = _adaln(cb, w_mod_b[i], b_mod_b[i])
        shift, scale, gate = jnp.split(mod_b, 3, axis=-1)
        q_scale = dict(scaled_cols=attn_dim, col_scale=LOG2E * HEAD_DIM ** -0.5,
                       silu_cols=(attn_dim, 2 * attn_dim))
        if i == 0:
            if w_cat is None:
                w_cat = jnp.concatenate([w_b_in[i], w_kv], axis=1).astype(BF16)
            qz = _norm_proj(x, jnp.stack([g_b[i], g_kv]), jnp.concatenate([shift, shift_kv]),
                            jnp.concatenate([scale, scale_kv]), w_cat,
                            tail_cols=2 * kv_dim, **q_scale)
            kv, kv_blk = qz, attn_dim // kv_dim
        else:
            qz = _norm_proj(x, row(g_b[i]), shift, scale, w_b_in[i].astype(BF16), **q_scale)
        a = _attention(qz, 0, kv, kv_blk, bias, sinks_b[i])
        w_out = w_b_out_bf if i == 0 and w_b_out_bf is not None else w_b_out[i].astype(BF16)
        if i + 1 < n_b:
            x = _out_proj(a, w_out, x, gate)
        else:
            x = _out_proj_norm(a, w_out, x, gate, row(g_final))
    return x


def kernel(x, c, g_a, w_mod_a, b_mod_a, w_a_in, conv_a, w_a_out, g_kv, w_mod_kv, b_mod_kv, w_kv,
           rel_bias, g_b, w_mod_b, b_mod_b, w_b_in, sinks_b, w_b_out, g_final):
    outs = [_trunk(x[b], c[b], g_a, w_mod_a, b_mod_a, w_a_in, conv_a, w_a_out, g_kv, w_mod_kv,
                   b_mod_kv, w_kv, rel_bias, g_b, w_mod_b, b_mod_b, w_b_in, sinks_b, w_b_out,
                   g_final) for b in range(x.shape[0])]
    return jnp.stack(outs, axis=0)
```

```python
import functools
import math

import jax
import jax.numpy as jnp
from jax import lax
from jax.experimental import pallas as pl
from jax.experimental.pallas import tpu as pltpu

HEAD_DIM = 64
GROUP = 8
BLOCK = 128
N_BUCKETS = 32
MAX_DISTANCE = 128
CONV_WIDTH = 3
EPS = 1e-6
LOG2E = math.log2(math.e)

LANES = 128
SUBLANES = 8
MIB = 1024 * 1024
VMEM_LIMIT = 56 * MIB

F32 = jnp.float32
BF16 = jnp.bfloat16
STATS_ROWS = 512
NORM_ROWS = 256


def _params(*semantics, vmem=VMEM_LIMIT):
    return pltpu.CompilerParams(dimension_semantics=semantics, vmem_limit_bytes=vmem)


def _silu(v):
    return v * jax.nn.sigmoid(v)


def _adaln_kernel(cb_ref, w_ref, b_ref, o_ref):
    ca = _silu(cb_ref[...])
    for g in range(o_ref.shape[1] // LANES):
        cols = slice(g * LANES, (g + 1) * LANES)
        o_ref[:, cols] = jnp.sum(w_ref[:, cols] * ca, axis=0, keepdims=True) + b_ref[:, cols]


def _adaln(cb, w, b, tn=1024):
    k, n = w.shape
    return pl.pallas_call(
        _adaln_kernel,
        grid=(n // tn,),
        in_specs=[
            pl.BlockSpec((k, LANES), lambda j: (0, 0)),
            pl.BlockSpec((k, tn), lambda j: (0, j)),
            pl.BlockSpec((1, tn), lambda j: (0, j)),
        ],
        out_specs=pl.BlockSpec((1, tn), lambda j: (0, j)),
        out_shape=jax.ShapeDtypeStruct((1, n), F32),
        compiler_params=_params("arbitrary"),
        name="adaln_matvec",
    )(cb, w, b.reshape(1, n))


def _col_groups(d):
    return [slice(k * LANES, (k + 1) * LANES) for k in range(d // LANES)]


def _rms_stats(x_ref, rstd_scr):
    tm, d = x_ref.shape

    def body(r, carry):
        base = pl.multiple_of(r * STATS_ROWS, STATS_ROWS)
        parts = []
        for s in range(STATS_ROWS // SUBLANES):
            xr = x_ref[pl.ds(base + s * SUBLANES, SUBLANES), :]
            sq = xr * xr
            part = sq[:, :LANES]
            for cols in _col_groups(d)[1:]:
                part = part + sq[:, cols]
            parts.append(part)
        acc = jnp.concatenate(parts, axis=0)
        ms = jnp.sum(acc, axis=-1, keepdims=True) * (1.0 / d)
        rstd_scr[pl.ds(base, STATS_ROWS), :] = jnp.broadcast_to(lax.rsqrt(ms + EPS),
                                                               (STATS_ROWS, LANES))
        return carry

    lax.fori_loop(0, tm // STATS_ROWS, body, 0)


def _modnorm_apply(x_ref, rstd_scr, g_ref, sh_ref, sc_ref, mod, h_scr):
    tm, d = x_ref.shape
    vec = slice(mod, mod + 1)

    def body(r, carry):
        base = pl.multiple_of(r * NORM_ROWS, NORM_ROWS)
        gains = [g_ref[vec, cols] * (1.0 + sc_ref[vec, cols]) for cols in _col_groups(d)]
        for s in range(NORM_ROWS // BF16_ROWS):
            rows = pl.ds(base + s * BF16_ROWS, BF16_ROWS)
            rstd = rstd_scr[rows, :]
            for cols, gain in zip(_col_groups(d), gains):
                h_scr[rows, cols] = ((x_ref[rows, cols] * rstd) * gain
                                     + sh_ref[vec, cols]).astype(h_scr.dtype)
        return carry

    lax.fori_loop(0, tm // NORM_ROWS, body, 0)


def _row_tile_copy(x_hbm, x_buf, sem, i):
    tm = x_buf.shape[0]
    return pltpu.make_async_copy(x_hbm.at[pl.ds(i * tm, tm), :], x_buf, sem)


def _await_row_tile(x_hbm, x_buf, sem, i):
    @pl.when(i == 0)
    def _():
        _row_tile_copy(x_hbm, x_buf, sem, 0).start()

    _row_tile_copy(x_hbm, x_buf, sem, i).wait()


def _prefetch_next_row_tile(x_hbm, x_buf, sem, i):
    @pl.when(i + 1 < pl.num_programs(0))
    def _():
        _row_tile_copy(x_hbm, x_buf, sem, i + 1).start()


SIDE_COLS = 2048
ROW_SPLIT = 2


def _cast_cols(in_ref, o_ref, c0, o0):
    n = min(SIDE_COLS, in_ref.shape[1] - c0)
    o_ref[:, o0:o0 + n] = in_ref[:, c0:c0 + n].astype(o_ref.dtype)


def _matvec_cols(cb_ref, w_ref, acc_ref, c0):
    ca = _silu(cb_ref[...])
    for k in range(c0 // LANES, min(c0 + SIDE_COLS, w_ref.shape[1]) // LANES):
        cols = slice(k * LANES, (k + 1) * LANES)
        prod = w_ref[:, cols] * ca
        part = prod[0:SUBLANES]
        for q in range(1, prod.shape[0] // SUBLANES):
            part = part + prod[q * SUBLANES:(q + 1) * SUBLANES]
        acc_ref[:, cols] += part


def _conv_in_kernel(*refs, cast_widths, n_matvec):
    refs = list(refs)
    take = lambda n: [refs.pop(0) for _ in range(n)]
    x_hbm, g_ref, sh_ref, sc_ref, wb_ref, wc_ref, wu_ref, wz_ref, cw_ref = take(9)
    cast_in = [take(len(widths)) for widths in cast_widths]
    cb_ref = take(1)[0] if n_matvec else None
    mv_in = [take(2) for _ in range(n_matvec)]
    y_ref = take(1)[0]
    cast_out = take(len(cast_widths))
    mv_out = take(n_matvec)
    x_buf, x_sem, h_scr, rstd_scr, halo_scr = take(5)
    mv_acc = take(n_matvec)

    i = pl.program_id(0)
    j = pl.program_id(1)
    tm = y_ref.shape[0]
    step = i * pl.num_programs(1) + j
    last_step = pl.num_programs(0) * pl.num_programs(1) - 1

    @pl.when(step == 0)
    def _():
        for acc_ref in mv_acc:
            acc_ref[...] = jnp.zeros(acc_ref.shape, F32)

    @pl.when(j == 0)
    def _():
        _await_row_tile(x_hbm, x_buf, x_sem, i)
        _rms_stats(x_buf, rstd_scr)
        _modnorm_apply(x_buf, rstd_scr, g_ref, sh_ref, sc_ref, 0, h_scr)
        _prefetch_next_row_tile(x_hbm, x_buf, x_sem, i)

    @pl.when(i == 0)
    def _():
        halo_scr[j] = jnp.zeros(halo_scr.shape[1:], F32)

    side_jobs = []
    for ins, out in zip(cast_in, cast_out):
        col = 0
        for r in ins:
            for c0 in range(0, r.shape[1], SIDE_COLS):
                side_jobs.append(functools.partial(_cast_cols, r, out, c0, col + c0))
            col += r.shape[1]
    for (w_ref, _), acc_ref in zip(mv_in, mv_acc):
        for c0 in range(0, w_ref.shape[1], SIDE_COLS):
            side_jobs.append(functools.partial(_matvec_cols, cb_ref, w_ref, acc_ref, c0))
    proj = [[None] * ROW_SPLIT for _ in range(4)]
    n_gaps = 4 * ROW_SPLIT - 1
    rows_per = tm // ROW_SPLIT
    for n, (half, k) in enumerate((half, k) for half in range(ROW_SPLIT) for k in range(4)):
        w_ref = (wc_ref, wu_ref, wb_ref, wz_ref)[k]
        rows = slice(half * rows_per, (half + 1) * rows_per)
        proj[k][half] = jnp.dot(h_scr[rows, :], w_ref[...], preferred_element_type=F32)
        if n < n_gaps:
            for job in side_jobs[n::n_gaps]:
                job()
    c_proj, u_proj, b_proj, z_proj = [jnp.concatenate(p, axis=0) for p in proj]

    cu = c_proj * u_proj
    w0 = cw_ref[0:1, :]
    w1 = cw_ref[1:2, :]
    w2 = cw_ref[2:3, :]
    conv = w0 * pltpu.roll(cu, 2, axis=0) + w1 * pltpu.roll(cu, 1, axis=0) + w2 * cu
    gate = b_proj * _silu(z_proj)
    y_ref[...] = (gate * conv).astype(y_ref.dtype)

    prev = halo_scr[j]
    top = cu[0:SUBLANES]
    row = lax.broadcasted_iota(jnp.int32, top.shape, 0)
    back1 = jnp.where(row < 1, pltpu.roll(prev, 1, axis=0), pltpu.roll(top, 1, axis=0))
    back2 = jnp.where(row < 2, pltpu.roll(prev, 2, axis=0), pltpu.roll(top, 2, axis=0))
    conv_top = w0 * back2 + w1 * back1 + w2 * top
    y_ref[0:SUBLANES, :] = (gate[0:SUBLANES] * conv_top).astype(y_ref.dtype)
    halo_scr[j] = cu[tm - SUBLANES:tm]

    if n_matvec:
        @pl.when(step == last_step)
        def _():
            for (_, b_ref), o_ref, acc_ref in zip(mv_in, mv_out, mv_acc):
                o_ref[...] = jnp.sum(acc_ref[...], axis=0, keepdims=True) + b_ref[...]


CONV_TM, CONV_TN = 1024, 256
BF16_ROWS = 16


def _side_rows(s, c, k):
    steps = (s // CONV_TM) * (c // CONV_TN)
    rows = k // steps
    return rows if rows * steps == k and rows % BF16_ROWS == 0 else 0


def _conv_in(x, g, shift, scale, w_in, conv_w, cast_jobs=(), matvec_jobs=(), cb=None):
    tm, tn = CONV_TM, CONV_TN
    s, d = x.shape
    c = conv_w.shape[1]
    nj = c // tn
    vec = pl.BlockSpec((1, d), lambda i, j: (0, 0))
    w_specs = [pl.BlockSpec((d, tn), lambda i, j, k=k: (0, k * nj + j)) for k in range(4)]
    row_block = lambda rows, n: pl.BlockSpec((rows, n), lambda i, j: (i * nj + j, 0))
    whole = lambda n: pl.BlockSpec((1, n), lambda i, j: (0, 0))

    side_in, side_in_specs = [], []
    out_shapes = [jax.ShapeDtypeStruct((s, c), BF16)]
    out_specs = [pl.BlockSpec((tm, tn), lambda i, j: (i, j))]
    scratch = [pltpu.VMEM((tm, d), F32), pltpu.SemaphoreType.DMA(()),
               pltpu.VMEM((tm, d), BF16), pltpu.VMEM((tm, LANES), F32),
               pltpu.VMEM((nj, SUBLANES, tn), F32)]
    for job in cast_jobs:
        k = job[0].shape[0]
        rows = _side_rows(s, c, k)
        assert rows and all(a.shape[0] == k and a.shape[1] % LANES == 0 for a in job)
        side_in += list(job)
        side_in_specs += [row_block(rows, a.shape[1]) for a in job]
        n = sum(a.shape[1] for a in job)
        out_shapes.append(jax.ShapeDtypeStruct((k, n), BF16))
        out_specs.append(row_block(rows, n))
    if matvec_jobs:
        k = cb.shape[0]
        rows = _side_rows(s, c, k)
        assert rows
        side_in.append(cb)
        side_in_specs.append(row_block(rows, LANES))
    for w, b in matvec_jobs:
        n = w.shape[1]
        assert w.shape[0] == k and n % LANES == 0
        side_in += [w, b.reshape(1, n)]
        side_in_specs += [row_block(rows, n), whole(n)]
        out_shapes.append(jax.ShapeDtypeStruct((1, n), F32))
        out_specs.append(whole(n))
        scratch.append(pltpu.VMEM((SUBLANES, n), F32))

    body = functools.partial(_conv_in_kernel, cast_widths=tuple(tuple(a.shape[1] for a in job)
                                                                for job in cast_jobs),
                             n_matvec=len(matvec_jobs))
    outs = pl.pallas_call(
        body,
        grid=(s // tm, nj),
        in_specs=[pl.BlockSpec(memory_space=pl.ANY), vec, vec, vec, *w_specs,
                  pl.BlockSpec((CONV_WIDTH, tn), lambda i, j: (0, j)), *side_in_specs],
        out_specs=out_specs,
        out_shape=out_shapes,
        scratch_shapes=scratch,
        compiler_params=_params("arbitrary", "arbitrary"),
        name="conv_in_proj",
    )(x, g, shift, scale, w_in, w_in, w_in, w_in, conv_w, *side_in)
    n_cast = len(cast_jobs)
    return outs[0], list(outs[1:1 + n_cast]), list(outs[1 + n_cast:])


MXU_COLS = 256


def _out_proj_kernel(y_ref, w_ref, x_ref, gate_ref, o_ref):
    for c0 in range(0, o_ref.shape[1], MXU_COLS):
        cols = slice(c0, c0 + MXU_COLS)
        acc = jnp.dot(y_ref[...], w_ref[:, cols], preferred_element_type=F32)
        o_ref[:, cols] = x_ref[:, cols] + gate_ref[:, cols] * acc


def _out_proj(y, w, x, gate, tm=1024, tn=1024):
    s, k = y.shape
    d = w.shape[1]
    return pl.pallas_call(
        _out_proj_kernel,
        grid=(s // tm, d // tn),
        in_specs=[pl.BlockSpec((tm, k), lambda i, j: (i, 0)),
                  pl.BlockSpec((k, tn), lambda i, j: (0, j)),
                  pl.BlockSpec((tm, tn), lambda i, j: (i, j)),
                  pl.BlockSpec((1, tn), lambda i, j: (0, j))],
        out_specs=pl.BlockSpec((tm, tn), lambda i, j: (i, j)),
        out_shape=jax.ShapeDtypeStruct((s, d), F32),
        compiler_params=_params("arbitrary", "arbitrary"),
        name="out_proj",
    )(y, w, x, gate)


def _out_proj_norm_kernel(y_ref, w_ref, x_ref, gate_ref, gf_ref, o_ref):
    tm, d = o_ref.shape
    ssq = jnp.zeros((tm, LANES), F32)
    for c0 in range(0, d, MXU_COLS):
        cols = slice(c0, c0 + MXU_COLS)
        acc = jnp.dot(y_ref[...], w_ref[:, cols], preferred_element_type=F32)
        x2 = x_ref[:, cols] + gate_ref[:, cols] * acc
        o_ref[:, cols] = x2
        for l0 in range(0, MXU_COLS, LANES):
            ssq = ssq + x2[:, l0:l0 + LANES] * x2[:, l0:l0 + LANES]
    rstd = lax.rsqrt(jnp.sum(ssq, axis=-1, keepdims=True) * (1.0 / d) + EPS)
    o_ref[...] = (o_ref[...] * rstd) * gf_ref[...]


def _out_proj_norm(y, w, x, gate, g_final, tm=256):
    s, k = y.shape
    d = w.shape[1]
    vec = pl.BlockSpec((1, d), lambda i: (0, 0))
    return pl.pallas_call(
        _out_proj_norm_kernel,
        grid=(s // tm,),
        in_specs=[pl.BlockSpec((tm, k), lambda i: (i, 0)),
                  pl.BlockSpec((k, d), lambda i: (0, 0), pipeline_mode=pl.Buffered(1)),
                  pl.BlockSpec((tm, d), lambda i: (i, 0)), vec, vec],
        out_specs=pl.BlockSpec((tm, d), lambda i: (i, 0)),
        out_shape=jax.ShapeDtypeStruct((s, d), F32),
        compiler_params=_params("arbitrary", vmem=60 * MIB),
        name="out_proj_final_norm",
    )(y, w, x, gate, g_final)


def _norm_proj_kernel(x_hbm, g_ref, sh_ref, sc_ref, w_ref, o_ref, x_buf, x_sem, h_scr, rstd_scr,
                      *, tail_tiles, scaled_tiles, scale, silu_tiles):
    i = pl.program_id(0)
    j = pl.program_id(1)

    @pl.when(j == 0)
    def _():
        _await_row_tile(x_hbm, x_buf, x_sem, i)
        _rms_stats(x_buf, rstd_scr)
        _modnorm_apply(x_buf, rstd_scr, g_ref, sh_ref, sc_ref, 1 if tail_tiles else 0, h_scr)
        if not tail_tiles:
            _prefetch_next_row_tile(x_hbm, x_buf, x_sem, i)

    if tail_tiles:
        @pl.when(j == tail_tiles)
        def _():
            _modnorm_apply(x_buf, rstd_scr, g_ref, sh_ref, sc_ref, 0, h_scr)
            _prefetch_next_row_tile(x_hbm, x_buf, x_sem, i)

    gated = (j >= tail_tiles + silu_tiles[0]) & (j < tail_tiles + silu_tiles[1])

    @pl.when(gated)
    def _():
        acc = jnp.dot(h_scr[...], w_ref[...], preferred_element_type=F32)
        o_ref[...] = _silu(acc).astype(o_ref.dtype)

    @pl.when(jnp.logical_not(gated))
    def _():
        acc = jnp.dot(h_scr[...], w_ref[...], preferred_element_type=F32)
        if scaled_tiles:
            scaled = (j >= tail_tiles) & (j < tail_tiles + scaled_tiles)
            acc = acc * jnp.where(scaled, scale, 1.0)
        o_ref[...] = acc.astype(o_ref.dtype)


def _norm_proj(x, g, shift, scale, w, tail_cols=0, scaled_cols=0, col_scale=1.0,
               silu_cols=(0, 0), tm=1024, tn=1024):
    s, d = x.shape
    n = w.shape[1]
    n_mod = g.shape[0]
    assert scaled_cols % tn == 0 and tail_cols % tn == 0 and n_mod == (2 if tail_cols else 1)
    assert all(c % tn == 0 and scaled_cols <= c <= n - tail_cols for c in silu_cols)
    n_tiles, tail_tiles = n // tn, tail_cols // tn
    col_tile = lambda j: (j + n_tiles - tail_tiles) % n_tiles
    vec = pl.BlockSpec((n_mod, d), lambda i, j: (0, 0))
    body = functools.partial(_norm_proj_kernel, tail_tiles=tail_tiles,
                             scaled_tiles=scaled_cols // tn, scale=col_scale,
                             silu_tiles=tuple(c // tn for c in silu_cols))
    return pl.pallas_call(
        body,
        grid=(s // tm, n_tiles),
        in_specs=[pl.BlockSpec(memory_space=pl.ANY), vec, vec, vec,
                  pl.BlockSpec((d, tn), lambda i, j: (0, col_tile(j)))],
        out_specs=pl.BlockSpec((tm, tn), lambda i, j: (i, col_tile(j))),
        out_shape=jax.ShapeDtypeStruct((s, n), BF16),
        scratch_shapes=[pltpu.VMEM((tm, d), F32), pltpu.SemaphoreType.DMA(()),
                        pltpu.VMEM((tm, d), BF16), pltpu.VMEM((tm, LANES), F32)],
        compiler_params=_params("arbitrary", "arbitrary"),
        name="norm_proj",
    )(x, g, shift, scale, w)


def _t5_bucket(dist):
    max_exact = N_BUCKETS // 2
    d = jnp.maximum(dist, 0)
    d_f = jnp.maximum(d, 1).astype(F32)
    large = max_exact + (jnp.log(d_f / max_exact) / math.log(MAX_DISTANCE / max_exact)
                         * (N_BUCKETS - max_exact)).astype(jnp.int32)
    large = jnp.minimum(large, N_BUCKETS - 1)
    return jnp.where(d < max_exact, d, large)


HEAD_ORDER = tuple(range(0, GROUP, 2)) + tuple(range(1, GROUP, 2))


def _bias_table_kernel(rb_ref, bucket_ref, o_ref):
    g = pl.program_id(0)
    bucket = bucket_ref[...]
    b = lax.broadcasted_iota(jnp.int32, (2 * BLOCK, BLOCK), 0)
    a = lax.broadcasted_iota(jnp.int32, (2 * BLOCK, BLOCK), 1)
    dist = a + BLOCK - b
    band = (dist >= 0) & (dist < BLOCK)
    band_first = band & (b >= BLOCK)
    for c, hh in enumerate(HEAD_ORDER):
        by_dist = jnp.zeros(bucket.shape, F32)
        for k in range(N_BUCKETS):
            by_dist = jnp.where(bucket == k, rb_ref[k, g * GROUP + hh] * LOG2E, by_dist)
        rows = jnp.concatenate([by_dist] * (2 * BLOCK // SUBLANES), axis=0)
        tile = pltpu.roll(rows, 0, axis=1, stride=1, stride_axis=0)
        cols = slice(c * BLOCK, (c + 1) * BLOCK)
        o_ref[0, 0, :, cols] = jnp.where(band_first, tile, -jnp.inf)
        o_ref[1, 0, :, cols] = jnp.where(band, tile, -jnp.inf)


def _bias_table(rel_bias):
    n_kv = rel_bias.shape[1] // GROUP
    dist_bucket = jnp.broadcast_to(_t5_bucket(jnp.arange(BLOCK)), (SUBLANES, BLOCK))
    return pl.pallas_call(
        _bias_table_kernel,
        grid=(n_kv,),
        in_specs=[pl.BlockSpec(memory_space=pltpu.SMEM),
                  pl.BlockSpec((SUBLANES, BLOCK), lambda g: (0, 0))],
        out_specs=pl.BlockSpec((2, 1, 2 * BLOCK, GROUP * BLOCK), lambda g: (0, g, 0, 0)),
        out_shape=jax.ShapeDtypeStruct((2, n_kv, 2 * BLOCK, GROUP * BLOCK), F32),
        compiler_params=_params("arbitrary"),
        name="bias_table",
    )(rel_bias, dist_bucket.astype(jnp.int32))


ATTN_BLOCKS = 4


def _attn_kernel(sink_ref, qz_ref, kvp_ref, kvc_ref, bias_ref, o_ref):
    n_kv = bias_ref.shape[1]
    attn_dim, kv_dim = o_ref.shape[1], n_kv * HEAD_DIM
    for blk in range(ATTN_BLOCKS):
        rows = slice(blk * BLOCK, (blk + 1) * BLOCK)
        kv_prev = kvp_ref if blk == 0 else kvc_ref.at[(blk - 1) * BLOCK:blk * BLOCK, :]
        kv_cur = kvc_ref.at[rows, :]
        entry = jnp.where(pl.program_id(0) == 0, 0, 1) if blk == 0 else 1
        _attn_block(sink_ref, qz_ref.at[rows, :attn_dim], qz_ref.at[rows, attn_dim:],
                    kv_prev.at[:, :kv_dim], kv_cur.at[:, :kv_dim],
                    kv_prev.at[:, kv_dim:], kv_cur.at[:, kv_dim:],
                    bias_ref.at[entry], o_ref.at[rows, :])


def _attn_block(sink_ref, q_ref, z_ref, kp_ref, kc_ref, vp_ref, vc_ref, bias_ref, o_ref):
    n_kv = bias_ref.shape[0]
    ones_rows = jnp.ones((BF16_ROWS, 2 * BLOCK), BF16)
    low_half = lax.broadcasted_iota(jnp.int32, (2 * BLOCK, LANES), 1) < HEAD_DIM
    nt_dot = functools.partial(lax.dot_general, dimension_numbers=(((1,), (1,)), ((), ())),
                               preferred_element_type=F32)
    tiles = GROUP // 2
    for g in range(n_kv):
        heads = [g * GROUP + hh for hh in HEAD_ORDER]
        kv_cols = slice((g // 2) * LANES, (g // 2 + 1) * LANES)
        k2 = jnp.concatenate([kp_ref[:, kv_cols], kc_ref[:, kv_cols]], axis=0)
        v2_t = jnp.concatenate([vp_ref[:, kv_cols].T, vc_ref[:, kv_cols].T], axis=1)
        v_t = v2_t[:HEAD_DIM] if g % 2 == 0 else v2_t[HEAD_DIM:]
        v_t = jnp.concatenate([v_t, ones_rows], axis=0)
        k_other = pltpu.roll(k2, HEAD_DIM, axis=1)
        k_low, k_high = (k2, k_other) if g % 2 == 0 else (k_other, k2)
        k_low = jnp.where(low_half, k_low, jnp.zeros_like(k_low))
        k_high = jnp.where(low_half, jnp.zeros_like(k_high), k_high)
        group_cols = slice(g * tiles * LANES, (g + 1) * tiles * LANES)
        q_slab = q_ref[:, group_cols]
        q_rows = jnp.concatenate([q_slab[:, t * LANES:(t + 1) * LANES] for t in range(tiles)],
                                 axis=0)
        s_t = jnp.concatenate([nt_dot(k_low, q_rows), nt_dot(k_high, q_rows)], axis=1)
        s_t = s_t + bias_ref[g]
        sink = jnp.concatenate([jnp.full((1, BLOCK), sink_ref[h] * LOG2E, F32) for h in heads],
                               axis=1)
        m = jnp.maximum(jnp.max(s_t, axis=0, keepdims=True), sink)
        p = jnp.exp2(s_t - m).astype(BF16)
        o_aug = jnp.dot(v_t, p, preferred_element_type=F32)
        den = o_aug[HEAD_DIM:HEAD_DIM + 1] + jnp.exp2(sink - m)
        o_t = o_aug[:HEAD_DIM] / den
        o_t = o_t.astype(o_ref.dtype)
        out_tiles = []
        for t in range(tiles):
            both = jnp.concatenate([o_t[:, t * BLOCK:(t + 1) * BLOCK],
                                    o_t[:, (tiles + t) * BLOCK:(tiles + t + 1) * BLOCK]], axis=0)
            out_tiles.append(both.T)
        o_ref[:, group_cols] = jnp.concatenate(out_tiles, axis=1) * z_ref[:, group_cols]


def _attention(qz, qz_blk, kv, kv_blk, bias, sinks):
    s = qz.shape[0]
    n_kv = bias.shape[1]
    kv_dim = n_kv * HEAD_DIM
    attn_dim = kv_dim * GROUP
    assert n_kv % 2 == 0 and GROUP % 2 == 0 and 2 * HEAD_DIM == LANES and BLOCK == LANES
    rows = ATTN_BLOCKS * BLOCK
    assert s % rows == 0
    before = lambda i: jnp.maximum(i * ATTN_BLOCKS - 1, 0)
    return pl.pallas_call(
        _attn_kernel,
        grid=(s // rows,),
        in_specs=[pl.BlockSpec(memory_space=pltpu.SMEM),
                  pl.BlockSpec((rows, 2 * attn_dim), lambda i: (i, qz_blk)),
                  pl.BlockSpec((BLOCK, 2 * kv_dim), lambda i: (before(i), kv_blk)),
                  pl.BlockSpec((rows, 2 * kv_dim), lambda i: (i, kv_blk)),
                  pl.BlockSpec(bias.shape, lambda i: (0, 0, 0, 0),
                               pipeline_mode=pl.Buffered(1))],
        out_specs=pl.BlockSpec((rows, attn_dim), lambda i: (i, 0)),
        out_shape=jax.ShapeDtypeStruct((s, attn_dim), BF16),
        compiler_params=_params("arbitrary"),
        name="swa_attention",
    )(sinks, qz, kv, kv, bias)


def _trunk(x, c, g_a, w_mod_a, b_mod_a, w_a_in, conv_a, w_a_out, g_kv, w_mod_kv, b_mod_kv,
           w_kv, rel_bias, g_b, w_mod_b, b_mod_b, w_b_in, sinks_b, w_b_out, g_final):
    s, d = x.shape
    n_a, n_b = w_a_in.shape[0], w_b_in.shape[0]
    assert n_b >= 1 and s % BLOCK == 0
    row = lambda v: v.reshape(1, -1)
    cb = jnp.broadcast_to(c.reshape(d, 1), (d, LANES))

    side_operands = (w_a_out[-1], w_b_in[0], w_kv, w_b_out[0], w_mod_kv, w_mod_b[0]) if n_a else ()
    stream = n_a > 0 and all(_side_rows(s, conv_a.shape[2], a.shape[0]) for a in side_operands)
    w_a_out_bf = w_cat = w_b_out_bf = mod_kv = mod_b = None
    for i in range(n_a):
        shift, scale, gate = jnp.split(_adaln(cb, w_mod_a[i], b_mod_a[i]), 3, axis=-1)
        args = (x, row(g_a[i]), shift, scale, w_a_in[i].astype(BF16), conv_a[i])
        if stream and i == n_a - 1:
            y, (w_a_out_bf, w_cat, w_b_out_bf), (mod_kv, mod_b) = _conv_in(
                *args, cast_jobs=((w_a_out[i],), (w_b_in[0], w_kv), (w_b_out[0],)),
                matvec_jobs=((w_mod_kv, b_mod_kv), (w_mod_b[0], b_mod_b[0])), cb=cb)
        else:
            y, _, _ = _conv_in(*args)
            w_a_out_bf = w_a_out[i].astype(BF16)
        x = _out_proj(y, w_a_out_bf, x, gate)

    if mod_kv is None:
        mod_kv = _adaln(cb, w_mod_kv, b_mod_kv)
    shift_kv, scale_kv = jnp.split(mod_kv, 2, axis=-1)
    bias = _bias_table(rel_bias)
    attn_dim = w_b_in.shape[2] // 2
    kv_dim = w_kv.shape[1] // 2
    kv = None
    for i in range(n_b):
        if i > 0 or mod_b is None:
            mod_b = _adaln(cb, w_mod_b[i], b_mod_b[i])
        shift, scale, gate = jnp.split(mod_b, 3, axis=-1)
        q_scale = dict(scaled_cols=attn_dim, col_scale=LOG2E * HEAD_DIM ** -0.5,
                       silu_cols=(attn_dim, 2 * attn_dim))
        if i == 0:
            if w_cat is None:
                w_cat = jnp.concatenate([w_b_in[i], w_kv], axis=1).astype(BF16)
            qz = _norm_proj(x, jnp.stack([g_b[i], g_kv]), jnp.concatenate([shift, shift_kv]),
                            jnp.concatenate([scale, scale_kv]), w_cat,
                            tail_cols=2 * kv_dim, **q_scale)
            kv, kv_blk = qz, attn_dim // kv_dim
        else:
            qz = _norm_proj(x, row(g_b[i]), shift, scale, w_b_in[i].astype(BF16), **q_scale)
        a = _attention(qz, 0, kv, kv_blk, bias, sinks_b[i])
        w_out = w_b_out_bf if i == 0 and w_b_out_bf is not None else w_b_out[i].astype(BF16)
        if i + 1 < n_b:
            x = _out_proj(a, w_out, x, gate)
        else:
            x = _out_proj_norm(a, w_out, x, gate, row(g_final))
    return x


def kernel(x, c, g_a, w_mod_a, b_mod_a, w_a_in, conv_a, w_a_out, g_kv, w_mod_kv, b_mod_kv, w_kv,
           rel_bias, g_b, w_mod_b, b_mod_b, w_b_in, sinks_b, w_b_out, g_final):
    outs = [_trunk(x[b], c[b], g_a, w_mod_a, b_mod_a, w_a_in, conv_a, w_a_out, g_kv, w_mod_kv,
                   b_mod_kv, w_kv, rel_bias, g_b, w_mod_b, b_mod_b, w_b_in, sinks_b, w_b_out,
                   g_final) for b in range(x.shape[0])]
    return jnp.stack(outs, axis=0)
```

```python
import functools
import math

import jax
import jax.numpy as jnp
from jax import lax
from jax.experimental import pallas as pl
from jax.experimental.pallas import tpu as pltpu

HEAD_DIM = 64
GROUP = 8
BLOCK = 128
N_BUCKETS = 32
MAX_DISTANCE = 128
CONV_WIDTH = 3
EPS = 1e-6
LOG2E = math.log2(math.e)

LANES = 128
SUBLANES = 8
MIB = 1024 * 1024
VMEM_LIMIT = 56 * MIB

F32 = jnp.float32
BF16 = jnp.bfloat16
STATS_ROWS = 512
NORM_ROWS = 256


def _params(*semantics, vmem=VMEM_LIMIT):
    return pltpu.CompilerParams(dimension_semantics=semantics, vmem_limit_bytes=vmem)


def _silu(v):
    return v * jax.nn.sigmoid(v)


def _adaln_kernel(cb_ref, w_ref, b_ref, o_ref):
    ca = _silu(cb_ref[...])
    for g in range(o_ref.shape[1] // LANES):
        cols = slice(g * LANES, (g + 1) * LANES)
        o_ref[:, cols] = jnp.sum(w_ref[:, cols] * ca, axis=0, keepdims=True) + b_ref[:, cols]


def _adaln(cb, w, b, tn=1024):
    k, n = w.shape
    return pl.pallas_call(
        _adaln_kernel,
        grid=(n // tn,),
        in_specs=[
            pl.BlockSpec((k, LANES), lambda j: (0, 0)),
            pl.BlockSpec((k, tn), lambda j: (0, j)),
            pl.BlockSpec((1, tn), lambda j: (0, j)),
        ],
        out_specs=pl.BlockSpec((1, tn), lambda j: (0, j)),
        out_shape=jax.ShapeDtypeStruct((1, n), F32),
        compiler_params=_params("arbitrary"),
        name="adaln_matvec",
    )(cb, w, b.reshape(1, n))


def _col_groups(d):
    return [slice(k * LANES, (k + 1) * LANES) for k in range(d // LANES)]


def _rms_stats(x_ref, rstd_scr):
    tm, d = x_ref.shape

    def body(r, carry):
        base = pl.multiple_of(r * STATS_ROWS, STATS_ROWS)
        parts = []
        for s in range(STATS_ROWS // SUBLANES):
            xr = x_ref[pl.ds(base + s * SUBLANES, SUBLANES), :]
            sq = xr * xr
            part = sq[:, :LANES]
            for cols in _col_groups(d)[1:]:
                part = part + sq[:, cols]
            parts.append(part)
        acc = jnp.concatenate(parts, axis=0)
        ms = jnp.sum(acc, axis=-1, keepdims=True) * (1.0 / d)
        rstd_scr[pl.ds(base, STATS_ROWS), :] = jnp.broadcast_to(lax.rsqrt(ms + EPS),
                                                               (STATS_ROWS, LANES))
        return carry

    lax.fori_loop(0, tm // STATS_ROWS, body, 0)


def _modnorm_apply(x_ref, rstd_scr, g_ref, sh_ref, sc_ref, mod, h_scr):
    tm, d = x_ref.shape
    vec = slice(mod, mod + 1)

    def body(r, carry):
        base = pl.multiple_of(r * NORM_ROWS, NORM_ROWS)
        gains = [g_ref[vec, cols] * (1.0 + sc_ref[vec, cols]) for cols in _col_groups(d)]
        for s in range(NORM_ROWS // BF16_ROWS):
            rows = pl.ds(base + s * BF16_ROWS, BF16_ROWS)
            rstd = rstd_scr[rows, :]
            for cols, gain in zip(_col_groups(d), gains):
                h_scr[rows, cols] = ((x_ref[rows, cols] * rstd) * gain
                                     + sh_ref[vec, cols]).astype(h_scr.dtype)
        return carry

    lax.fori_loop(0, tm // NORM_ROWS, body, 0)


def _row_tile_copy(x_hbm, x_buf, sem, i):
    tm = x_buf.shape[0]
    return pltpu.make_async_copy(x_hbm.at[pl.ds(i * tm, tm), :], x_buf, sem)


def _await_row_tile(x_hbm, x_buf, sem, i):
    @pl.when(i == 0)
    def _():
        _row_tile_copy(x_hbm, x_buf, sem, 0).start()

    _row_tile_copy(x_hbm, x_buf, sem, i).wait()


def _prefetch_next_row_tile(x_hbm, x_buf, sem, i):
    @pl.when(i + 1 < pl.num_programs(0))
    def _():
        _row_tile_copy(x_hbm, x_buf, sem, i + 1).start()


SIDE_COLS = 2048
ROW_SPLIT = 2


def _cast_cols(in_ref, o_ref, c0, o0):
    n = min(SIDE_COLS, in_ref.shape[1] - c0)
    o_ref[:, o0:o0 + n] = in_ref[:, c0:c0 + n].astype(o_ref.dtype)


def _matvec_cols(cb_ref, w_ref, acc_ref, c0):
    ca = _silu(cb_ref[...])
    for k in range(c0 // LANES, min(c0 + SIDE_COLS, w_ref.shape[1]) // LANES):
        cols = slice(k * LANES, (k + 1) * LANES)
        prod = w_ref[:, cols] * ca
        part = prod[0:SUBLANES]
        for q in range(1, prod.shape[0] // SUBLANES):
            part = part + prod[q * SUBLANES:(q + 1) * SUBLANES]
        acc_ref[:, cols] += part


def _conv_in_kernel(*refs, cast_widths, n_matvec):
    refs = list(refs)
    take = lambda n: [refs.pop(0) for _ in range(n)]
    x_hbm, g_ref, sh_ref, sc_ref, wb_ref, wc_ref, wu_ref, wz_ref, cw_ref = take(9)
    cast_in = [take(len(widths)) for widths in cast_widths]
    cb_ref = take(1)[0] if n_matvec else None
    mv_in = [take(2) for _ in range(n_matvec)]
    y_ref = take(1)[0]
    cast_out = take(len(cast_widths))
    mv_out = take(n_matvec)
    x_buf, x_sem, h_scr, rstd_scr, halo_scr = take(5)
    mv_acc = take(n_matvec)

    i = pl.program_id(0)
    j = pl.program_id(1)
    tm = y_ref.shape[0]
    step = i * pl.num_programs(1) + j
    last_step = pl.num_programs(0) * pl.num_programs(1) - 1

    @pl.when(step == 0)
    def _():
        for acc_ref in mv_acc:
            acc_ref[...] = jnp.zeros(acc_ref.shape, F32)

    @pl.when(j == 0)
    def _():
        _await_row_tile(x_hbm, x_buf, x_sem, i)
        _rms_stats(x_buf, rstd_scr)
        _modnorm_apply(x_buf, rstd_scr, g_ref, sh_ref, sc_ref, 0, h_scr)
        _prefetch_next_row_tile(x_hbm, x_buf, x_sem, i)

    @pl.when(i == 0)
    def _():
        halo_scr[j] = jnp.zeros(halo_scr.shape[1:], F32)

    side_jobs = []
    for ins, out in zip(cast_in, cast_out):
        col = 0
        for r in ins:
            for c0 in range(0, r.shape[1], SIDE_COLS):
                side_jobs.append(functools.partial(_cast_cols, r, out, c0, col + c0))
            col += r.shape[1]
    for (w_ref, _), acc_ref in zip(mv_in, mv_acc):
        for c0 in range(0, w_ref.shape[1], SIDE_COLS):
            side_jobs.append(functools.partial(_matvec_cols, cb_ref, w_ref, acc_ref, c0))
    proj = [[None] * ROW_SPLIT for _ in range(4)]
    n_gaps = 4 * ROW_SPLIT - 1
    rows_per = tm // ROW_SPLIT
    for n, (half, k) in enumerate((half, k) for half in range(ROW_SPLIT) for k in range(4)):
        w_ref = (wc_ref, wu_ref, wb_ref, wz_ref)[k]
        rows = slice(half * rows_per, (half + 1) * rows_per)
        proj[k][half] = jnp.dot(h_scr[rows, :], w_ref[...], preferred_element_type=F32)
        if n < n_gaps:
            for job in side_jobs[n::n_gaps]:
                job()
    c_proj, u_proj, b_proj, z_proj = [jnp.concatenate(p, axis=0) for p in proj]

    cu = c_proj * u_proj
    w0 = cw_ref[0:1, :]
    w1 = cw_ref[1:2, :]
    w2 = cw_ref[2:3, :]
    conv = w0 * pltpu.roll(cu, 2, axis=0) + w1 * pltpu.roll(cu, 1, axis=0) + w2 * cu
    gate = b_proj * _silu(z_proj)
    y_ref[...] = (gate * conv).astype(y_ref.dtype)

    prev = halo_scr[j]
    top = cu[0:SUBLANES]
    row = lax.broadcasted_iota(jnp.int32, top.shape, 0)
    back1 = jnp.where(row < 1, pltpu.roll(prev, 1, axis=0), pltpu.roll(top, 1, axis=0))
    back2 = jnp.where(row < 2, pltpu.roll(prev, 2, axis=0), pltpu.roll(top, 2, axis=0))
    conv_top = w0 * back2 + w1 * back1 + w2 * top
    y_ref[0:SUBLANES, :] = (gate[0:SUBLANES] * conv_top).astype(y_ref.dtype)
    halo_scr[j] = cu[tm - SUBLANES:tm]

    if n_matvec:
        @pl.when(step == last_step)
        def _():
            for (_, b_ref), o_ref, acc_ref in zip(mv_in, mv_out, mv_acc):
                o_ref[...] = jnp.sum(acc_ref[...], axis=0, keepdims=True) + b_ref[...]


CONV_TM, CONV_TN = 1024, 256
BF16_ROWS = 16


def _side_rows(s, c, k):
    steps = (s // CONV_TM) * (c // CONV_TN)
    rows = k // steps
    return rows if rows * steps == k and rows % BF16_ROWS == 0 else 0


def _conv_in(x, g, shift, scale, w_in, conv_w, cast_jobs=(), matvec_jobs=(), cb=None):
    tm, tn = CONV_TM, CONV_TN
    s, d = x.shape
    c = conv_w.shape[1]
    nj = c // tn
    vec = pl.BlockSpec((1, d), lambda i, j: (0, 0))
    w_specs = [pl.BlockSpec((d, tn), lambda i, j, k=k: (0, k * nj + j)) for k in range(4)]
    row_block = lambda rows, n: pl.BlockSpec((rows, n), lambda i, j: (i * nj + j, 0))
    whole = lambda n: pl.BlockSpec((1, n), lambda i, j: (0, 0))

    side_in, side_in_specs = [], []
    out_shapes = [jax.ShapeDtypeStruct((s, c), BF16)]
    out_specs = [pl.BlockSpec((tm, tn), lambda i, j: (i, j))]
    scratch = [pltpu.VMEM((tm, d), F32), pltpu.SemaphoreType.DMA(()),
               pltpu.VMEM((tm, d), BF16), pltpu.VMEM((tm, LANES), F32),
               pltpu.VMEM((nj, SUBLANES, tn), F32)]
    for job in cast_jobs:
        k = job[0].shape[0]
        rows = _side_rows(s, c, k)
        assert rows and all(a.shape[0] == k and a.shape[1] % LANES == 0 for a in job)
        side_in += list(job)
        side_in_specs += [row_block(rows, a.shape[1]) for a in job]
        n = sum(a.shape[1] for a in job)
        out_shapes.append(jax.ShapeDtypeStruct((k, n), BF16))
        out_specs.append(row_block(rows, n))
    if matvec_jobs:
        k = cb.shape[0]
        rows = _side_rows(s, c, k)
        assert rows
        side_in.append(cb)
        side_in_specs.append(row_block(rows, LANES))
    for w, b in matvec_jobs:
        n = w.shape[1]
        assert w.shape[0] == k and n % LANES == 0
        side_in += [w, b.reshape(1, n)]
        side_in_specs += [row_block(rows, n), whole(n)]
        out_shapes.append(jax.ShapeDtypeStruct((1, n), F32))
        out_specs.append(whole(n))
        scratch.append(pltpu.VMEM((SUBLANES, n), F32))

    body = functools.partial(_conv_in_kernel, cast_widths=tuple(tuple(a.shape[1] for a in job)
                                                                for job in cast_jobs),
                             n_matvec=len(matvec_jobs))
    outs = pl.pallas_call(
        body,
        grid=(s // tm, nj),
        in_specs=[pl.BlockSpec(memory_space=pl.ANY), vec, vec, vec, *w_specs,
                  pl.BlockSpec((CONV_WIDTH, tn), lambda i, j: (0, j)), *side_in_specs],
        out_specs=out_specs,
        out_shape=out_shapes,
        scratch_shapes=scratch,
        compiler_params=_params("arbitrary", "arbitrary"),
        name="conv_in_proj",
    )(x, g, shift, scale, w_in, w_in, w_in, w_in, conv_w, *side_in)
    n_cast = len(cast_jobs)
    return outs[0], list(outs[1:1 + n_cast]), list(outs[1 + n_cast:])


MXU_COLS = 256


def _out_proj_kernel(y_ref, w_ref, x_ref, gate_ref, o_ref):
    for c0 in range(0, o_ref.shape[1], MXU_COLS):
        cols = slice(c0, c0 + MXU_COLS)
        acc = jnp.dot(y_ref[...], w_ref[:, cols], preferred_element_type=F32)
        o_ref[:, cols] = x_ref[:, cols] + gate_ref[:, cols] * acc


def _out_proj(y, w, x, gate, tm=1024, tn=1024):
    s, k = y.shape
    d = w.shape[1]
    return pl.pallas_call(
        _out_proj_kernel,
        grid=(s // tm, d // tn),
        in_specs=[pl.BlockSpec((tm, k), lambda i, j: (i, 0)),
                  pl.BlockSpec((k, tn), lambda i, j: (0, j)),
                  pl.BlockSpec((tm, tn), lambda i, j: (i, j)),
                  pl.BlockSpec((1, tn), lambda i, j: (0, j))],
        out_specs=pl.BlockSpec((tm, tn), lambda i, j: (i, j)),
        out_shape=jax.ShapeDtypeStruct((s, d), F32),
        compiler_params=_params("arbitrary", "arbitrary"),
        name="out_proj",
    )(y, w, x, gate)


def _out_proj_norm_kernel(y_ref, w_ref, x_ref, gate_ref, gf_ref, o_ref):
    tm, d = o_ref.shape
    ssq = jnp.zeros((tm, LANES), F32)
    for c0 in range(0, d, MXU_COLS):
        cols = slice(c0, c0 + MXU_COLS)
        acc = jnp.dot(y_ref[...], w_ref[:, cols], preferred_element_type=F32)
        x2 = x_ref[:, cols] + gate_ref[:, cols] * acc
        o_ref[:, cols] = x2
        for l0 in range(0, MXU_COLS, LANES):
            ssq = ssq + x2[:, l0:l0 + LANES] * x2[:, l0:l0 + LANES]
    rstd = lax.rsqrt(jnp.sum(ssq, axis=-1, keepdims=True) * (1.0 / d) + EPS)
    o_ref[...] = (o_ref[...] * rstd) * gf_ref[...]


def _out_proj_norm(y, w, x, gate, g_final, tm=256):
    s, k = y.shape
    d = w.shape[1]
    vec = pl.BlockSpec((1, d), lambda i: (0, 0))
    return pl.pallas_call(
        _out_proj_norm_kernel,
        grid=(s // tm,),
        in_specs=[pl.BlockSpec((tm, k), lambda i: (i, 0)),
                  pl.BlockSpec((k, d), lambda i: (0, 0), pipeline_mode=pl.Buffered(1)),
                  pl.BlockSpec((tm, d), lambda i: (i, 0)), vec, vec],
        out_specs=pl.BlockSpec((tm, d), lambda i: (i, 0)),
        out_shape=jax.ShapeDtypeStruct((s, d), F32),
        compiler_params=_params("arbitrary", vmem=60 * MIB),
        name="out_proj_final_norm",
    )(y, w, x, gate, g_final)


def _norm_proj_kernel(x_hbm, g_ref, sh_ref, sc_ref, w_ref, o_ref, x_buf, x_sem, h_scr, rstd_scr,
                      *, tail_tiles, scaled_tiles, scale, silu_tiles):
    i = pl.program_id(0)
    j = pl.program_id(1)

    @pl.when(j == 0)
    def _():
        _await_row_tile(x_hbm, x_buf, x_sem, i)
        _rms_stats(x_buf, rstd_scr)
        _modnorm_apply(x_buf, rstd_scr, g_ref, sh_ref, sc_ref, 1 if tail_tiles else 0, h_scr)
        if not tail_tiles:
            _prefetch_next_row_tile(x_hbm, x_buf, x_sem, i)

    if tail_tiles:
        @pl.when(j == tail_tiles)
        def _():
            _modnorm_apply(x_buf, rstd_scr, g_ref, sh_ref, sc_ref, 0, h_scr)
            _prefetch_next_row_tile(x_hbm, x_buf, x_sem, i)

    gated = (j >= tail_tiles + silu_tiles[0]) & (j < tail_tiles + silu_tiles[1])

    @pl.when(gated)
    def _():
        acc = jnp.dot(h_scr[...], w_ref[...], preferred_element_type=F32)
        o_ref[...] = _silu(acc).astype(o_ref.dtype)

    @pl.when(jnp.logical_not(gated))
    def _():
        acc = jnp.dot(h_scr[...], w_ref[...], preferred_element_type=F32)
        if scaled_tiles:
            scaled = (j >= tail_tiles) & (j < tail_tiles + scaled_tiles)
            acc = acc * jnp.where(scaled, scale, 1.0)
        o_ref[...] = acc.astype(o_ref.dtype)


def _norm_proj(x, g, shift, scale, w, tail_cols=0, scaled_cols=0, col_scale=1.0,
               silu_cols=(0, 0), tm=1024, tn=1024):
    s, d = x.shape
    n = w.shape[1]
    n_mod = g.shape[0]
    assert scaled_cols % tn == 0 and tail_cols % tn == 0 and n_mod == (2 if tail_cols else 1)
    assert all(c % tn == 0 and scaled_cols <= c <= n - tail_cols for c in silu_cols)
    n_tiles, tail_tiles = n // tn, tail_cols // tn
    col_tile = lambda j: (j + n_tiles - tail_tiles) % n_tiles
    vec = pl.BlockSpec((n_mod, d), lambda i, j: (0, 0))
    body = functools.partial(_norm_proj_kernel, tail_tiles=tail_tiles,
                             scaled_tiles=scaled_cols // tn, scale=col_scale,
                             silu_tiles=tuple(c // tn for c in silu_cols))
    return pl.pallas_call(
        body,
        grid=(s // tm, n_tiles),
        in_specs=[pl.BlockSpec(memory_space=pl.ANY), vec, vec, vec,
                  pl.BlockSpec((d, tn), lambda i, j: (0, col_tile(j)))],
        out_specs=pl.BlockSpec((tm, tn), lambda i, j: (i, col_tile(j))),
        out_shape=jax.ShapeDtypeStruct((s, n), BF16),
        scratch_shapes=[pltpu.VMEM((tm, d), F32), pltpu.SemaphoreType.DMA(()),
                        pltpu.VMEM((tm, d), BF16), pltpu.VMEM((tm, LANES), F32)],
        compiler_params=_params("arbitrary", "arbitrary"),
        name="norm_proj",
    )(x, g, shift, scale, w)


def _t5_bucket(dist):
    max_exact = N_BUCKETS // 2
    d = jnp.maximum(dist, 0)
    d_f = jnp.maximum(d, 1).astype(F32)
    large = max_exact + (jnp.log(d_f / max_exact) / math.log(MAX_DISTANCE / max_exact)
                         * (N_BUCKETS - max_exact)).astype(jnp.int32)
    large = jnp.minimum(large, N_BUCKETS - 1)
    return jnp.where(d < max_exact, d, large)


HALF = BLOCK // 2
HALF_LANES = GROUP * HALF
KEY_WINDOW = 3 * HALF


def _regroup_halves(a, b, low_lanes):
    return (jnp.where(low_lanes, a, pltpu.roll(b, HALF, axis=1)),
            jnp.where(low_lanes, pltpu.roll(a, HALF, axis=1), b))


def _bias_table_kernel(rb_ref, bucket_ref, o_ref):
    g = pl.program_id(0)
    bucket = bucket_ref[...]
    b = lax.broadcasted_iota(jnp.int32, (2 * BLOCK, BLOCK), 0)
    a = lax.broadcasted_iota(jnp.int32, (2 * BLOCK, BLOCK), 1)
    dist = a + BLOCK - b
    band = (dist >= 0) & (dist < BLOCK)
    masks = (band & (b >= BLOCK), band)
    low_lanes = a < HALF

    def head_tile(hh):
        by_dist = jnp.zeros(bucket.shape, F32)
        for k in range(N_BUCKETS):
            by_dist = jnp.where(bucket == k, rb_ref[k, g * GROUP + hh] * LOG2E, by_dist)
        rows = jnp.concatenate([by_dist] * (2 * BLOCK // SUBLANES), axis=0)
        return pltpu.roll(rows, 0, axis=1, stride=1, stride_axis=0)

    for t in range(GROUP // 2):
        even, odd = head_tile(2 * t), head_tile(2 * t + 1)
        for e, mask in enumerate(masks):
            halves = _regroup_halves(jnp.where(mask, even, -jnp.inf),
                                     jnp.where(mask, odd, -jnp.inf), low_lanes)
            for half, tile in enumerate(halves):
                lane0 = half * HALF_LANES + t * LANES
                o_ref[e, 0, :, lane0:lane0 + LANES] = tile


def _bias_table(rel_bias):
    n_kv = rel_bias.shape[1] // GROUP
    dist_bucket = jnp.broadcast_to(_t5_bucket(jnp.arange(BLOCK)), (SUBLANES, BLOCK))
    return pl.pallas_call(
        _bias_table_kernel,
        grid=(n_kv,),
        in_specs=[pl.BlockSpec(memory_space=pltpu.SMEM),
                  pl.BlockSpec((SUBLANES, BLOCK), lambda g: (0, 0))],
        out_specs=pl.BlockSpec((2, 1, 2 * BLOCK, GROUP * BLOCK), lambda g: (0, g, 0, 0)),
        out_shape=jax.ShapeDtypeStruct((2, n_kv, 2 * BLOCK, GROUP * BLOCK), F32),
        compiler_params=_params("arbitrary"),
        name="bias_table",
    )(rel_bias, dist_bucket.astype(jnp.int32))


ATTN_BLOCKS = 4


def _attn_kernel(sink_ref, qz_ref, kvp_ref, kvc_ref, bias_ref, o_ref):
    n_kv = bias_ref.shape[1]
    attn_dim, kv_dim = o_ref.shape[1], n_kv * HEAD_DIM
    for blk in range(ATTN_BLOCKS):
        rows = slice(blk * BLOCK, (blk + 1) * BLOCK)
        kv_prev = kvp_ref if blk == 0 else kvc_ref.at[(blk - 1) * BLOCK:blk * BLOCK, :]
        kv_cur = kvc_ref.at[rows, :]
        entry = jnp.where(pl.program_id(0) == 0, 0, 1) if blk == 0 else 1
        _attn_block(sink_ref, qz_ref.at[rows, :attn_dim], qz_ref.at[rows, attn_dim:],
                    kv_prev.at[:, :kv_dim], kv_cur.at[:, :kv_dim],
                    kv_prev.at[:, kv_dim:], kv_cur.at[:, kv_dim:],
                    bias_ref.at[entry], o_ref.at[rows, :])


def _attn_block(sink_ref, q_ref, z_ref, kp_ref, kc_ref, vp_ref, vc_ref, bias_ref, o_ref):
    n_kv = bias_ref.shape[0]
    tiles = GROUP // 2
    ones_rows = jnp.ones((BF16_ROWS, 2 * BLOCK), BF16)
    zero_keys = jnp.zeros((2 * BLOCK - KEY_WINDOW, HALF_LANES), BF16)
    low_lanes = lambda rows: lax.broadcasted_iota(jnp.int32, (rows, LANES), 1) < HALF
    nt_dot = functools.partial(lax.dot_general, dimension_numbers=(((1,), (1,)), ((), ())),
                               preferred_element_type=F32)
    windows = (slice(0, KEY_WINDOW), slice(2 * BLOCK - KEY_WINDOW, 2 * BLOCK))
    for g in range(n_kv):
        kv_cols = slice((g // 2) * LANES, (g // 2 + 1) * LANES)
        k2 = jnp.concatenate([kp_ref[:, kv_cols], kc_ref[:, kv_cols]], axis=0)
        v2_t = jnp.concatenate([vp_ref[:, kv_cols].T, vc_ref[:, kv_cols].T], axis=1)
        v_t = v2_t[:HEAD_DIM] if g % 2 == 0 else v2_t[HEAD_DIM:]
        v_t = jnp.concatenate([v_t, ones_rows], axis=0)
        k_other = pltpu.roll(k2, HEAD_DIM, axis=1)
        k_both = (jnp.where(low_lanes(2 * BLOCK), k2, k_other) if g % 2 == 0
                  else jnp.where(low_lanes(2 * BLOCK), k_other, k2))
        group_cols = slice(g * tiles * LANES, (g + 1) * tiles * LANES)
        q_slab = q_ref[:, group_cols]
        sink = jnp.concatenate(
            [jnp.where(low_lanes(1), sink_ref[g * GROUP + 2 * t] * LOG2E,
                       sink_ref[g * GROUP + 2 * t + 1] * LOG2E) for t in range(tiles)], axis=1)
        logits = []
        for half, win in enumerate(windows):
            q_half = q_slab[half * HALF:(half + 1) * HALF]
            q_rows = []
            for t in range(tiles):
                q_t = q_half[:, t * LANES:(t + 1) * LANES]
                q_rows += [jnp.where(low_lanes(HALF), q_t, jnp.zeros_like(q_t)),
                           jnp.where(low_lanes(HALF), jnp.zeros_like(q_t), q_t)]
            q_rows = jnp.concatenate(q_rows, axis=0)
            logits.append(nt_dot(k_both[win], q_rows))
        probs, maxes = [], []
        for half, win in enumerate(windows):
            s_t = logits[half] + bias_ref[g, win, half * HALF_LANES:(half + 1) * HALF_LANES]
            m = jnp.maximum(jnp.max(s_t, axis=0, keepdims=True), sink)
            p = jnp.exp2(s_t - m).astype(BF16)
            probs.append(jnp.concatenate([p, zero_keys] if half == 0 else [zero_keys, p], axis=0))
            maxes.append(m)
        o_aug = jnp.dot(v_t, jnp.concatenate(probs, axis=1), preferred_element_type=F32)
        o_halves = []
        for half in range(2):
            o_h = o_aug[:, half * HALF_LANES:(half + 1) * HALF_LANES]
            den = o_h[HEAD_DIM:HEAD_DIM + 1] + jnp.exp2(sink - maxes[half])
            o_halves.append((o_h[:HEAD_DIM] / den).astype(o_ref.dtype))
        out_tiles = []
        for t in range(tiles):
            cols = slice(t * LANES, (t + 1) * LANES)
            even, odd = _regroup_halves(o_halves[0][:, cols], o_halves[1][:, cols],
                                        low_lanes(HEAD_DIM))
            out_tiles.append(jnp.concatenate([even, odd], axis=0).T)
        o_ref[:, group_cols] = jnp.concatenate(out_tiles, axis=1) * z_ref[:, group_cols]


def _attention(qz, qz_blk, kv, kv_blk, bias, sinks):
    s = qz.shape[0]
    n_kv = bias.shape[1]
    kv_dim = n_kv * HEAD_DIM
    attn_dim = kv_dim * GROUP
    assert n_kv % 2 == 0 and GROUP % 2 == 0 and 2 * HEAD_DIM == LANES and BLOCK == LANES
    rows = ATTN_BLOCKS * BLOCK
    assert s % rows == 0
    before = lambda i: jnp.maximum(i * ATTN_BLOCKS - 1, 0)
    return pl.pallas_call(
        _attn_kernel,
        grid=(s // rows,),
        in_specs=[pl.BlockSpec(memory_space=pltpu.SMEM),
                  pl.BlockSpec((rows, 2 * attn_dim), lambda i: (i, qz_blk)),
                  pl.BlockSpec((BLOCK, 2 * kv_dim), lambda i: (before(i), kv_blk)),
                  pl.BlockSpec((rows, 2 * kv_dim), lambda i: (i, kv_blk)),
                  pl.BlockSpec(bias.shape, lambda i: (0, 0, 0, 0),
                               pipeline_mode=pl.Buffered(1))],
        out_specs=pl.BlockSpec((rows, attn_dim), lambda i: (i, 0)),
        out_shape=jax.ShapeDtypeStruct((s, attn_dim), BF16),
        compiler_params=_params("arbitrary"),
        name="swa_attention",
    )(sinks, qz, kv, kv, bias)


def _trunk(x, c, g_a, w_mod_a, b_mod_a, w_a_in, conv_a, w_a_out, g_kv, w_mod_kv, b_mod_kv,
           w_kv, rel_bias, g_b, w_mod_b, b_mod_b, w_b_in, sinks_b, w_b_out, g_final):
    s, d = x.shape
    n_a, n_b = w_a_in.shape[0], w_b_in.shape[0]
    assert n_b >= 1 and s % BLOCK == 0
    row = lambda v: v.reshape(1, -1)
    cb = jnp.broadcast_to(c.reshape(d, 1), (d, LANES))

    side_operands = (w_a_out[-1], w_b_in[0], w_kv, w_b_out[0], w_mod_kv, w_mod_b[0]) if n_a else ()
    stream = n_a > 0 and all(_side_rows(s, conv_a.shape[2], a.shape[0]) for a in side_operands)
    w_a_out_bf = w_cat = w_b_out_bf = mod_kv = mod_b = None
    for i in range(n_a):
        shift, scale, gate = jnp.split(_adaln(cb, w_mod_a[i], b_mod_a[i]), 3, axis=-1)
        args = (x, row(g_a[i]), shift, scale, w_a_in[i].astype(BF16), conv_a[i])
        if stream and i == n_a - 1:
            y, (w_a_out_bf, w_cat, w_b_out_bf), (mod_kv, mod_b) = _conv_in(
                *args, cast_jobs=((w_a_out[i],), (w_b_in[0], w_kv), (w_b_out[0],)),
                matvec_jobs=((w_mod_kv, b_mod_kv), (w_mod_b[0], b_mod_b[0])), cb=cb)
        else:
            y, _, _ = _conv_in(*args)
            w_a_out_bf = w_a_out[i].astype(BF16)
        x = _out_proj(y, w_a_out_bf, x, gate)

    if mod_kv is None:
        mod_kv = _adaln(cb, w_mod_kv, b_mod_kv)
    shift_kv, scale_kv = jnp.split(mod_kv, 2, axis=-1)
    bias = _bias_table(rel_bias)
    attn_dim = w_b_in.shape[2] // 2
    kv_dim = w_kv.shape[1] // 2
    kv = None
    for i in range(n_b):
        if i > 0 or mod_b is None:
            mod_b = _adaln(cb, w_mod_b[i], b_mod_b[i])
        shift, scale, gate = jnp.split(mod_b, 3, axis=-1)
        q_scale = dict(scaled_cols=attn_dim, col_scale=LOG2E * HEAD_DIM ** -0.5,
                       silu_cols=(attn_dim, 2 * attn_dim))
        if i == 0:
            if w_cat is None:
                w_cat = jnp.concatenate([w_b_in[i], w_kv], axis=1).astype(BF16)
            qz = _norm_proj(x, jnp.stack([g_b[i], g_kv]), jnp.concatenate([shift, shift_kv]),
                            jnp.concatenate([scale, scale_kv]), w_cat,
                            tail_cols=2 * kv_dim, **q_scale)
            kv, kv_blk = qz, attn_dim // kv_dim
        else:
            qz = _norm_proj(x, row(g_b[i]), shift, scale, w_b_in[i].astype(BF16), **q_scale)
        a = _attention(qz, 0, kv, kv_blk, bias, sinks_b[i])
        w_out = w_b_out_bf if i == 0 and w_b_out_bf is not None else w_b_out[i].astype(BF16)
        if i + 1 < n_b:
            x = _out_proj(a, w_out, x, gate)
        else:
            x = _out_proj_norm(a, w_out, x, gate, row(g_final))
    return x


def kernel(x, c, g_a, w_mod_a, b_mod_a, w_a_in, conv_a, w_a_out, g_kv, w_mod_kv, b_mod_kv, w_kv,
           rel_bias, g_b, w_mod_b, b_mod_b, w_b_in, sinks_b, w_b_out, g_final):
    outs = [_trunk(x[b], c[b], g_a, w_mod_a, b_mod_a, w_a_in, conv_a, w_a_out, g_kv, w_mod_kv,
                   b_mod_kv, w_kv, rel_bias, g_b, w_mod_b, b_mod_b, w_b_in, sinks_b, w_b_out,
                   g_final) for b in range(x.shape[0])]
    return jnp.stack(outs, axis=0)
```

```python
import functools
import math

import jax
import jax.numpy as jnp
from jax import lax
from jax.experimental import pallas as pl
from jax.experimental.pallas import tpu as pltpu

HEAD_DIM = 64
GROUP = 8
BLOCK = 128
N_BUCKETS = 32
MAX_DISTANCE = 128
CONV_WIDTH = 3
EPS = 1e-6
LOG2E = math.log2(math.e)

LANES = 128
SUBLANES = 8
MIB = 1024 * 1024
VMEM_LIMIT = 56 * MIB

F32 = jnp.float32
BF16 = jnp.bfloat16
STATS_ROWS = 512
NORM_ROWS = 256


def _params(*semantics, vmem=VMEM_LIMIT):
    return pltpu.CompilerParams(dimension_semantics=semantics, vmem_limit_bytes=vmem)


def _silu(v):
    return v * jax.nn.sigmoid(v)


def _adaln_kernel(cb_ref, w_ref, b_ref, o_ref):
    ca = _silu(cb_ref[...])
    for g in range(o_ref.shape[1] // LANES):
        cols = slice(g * LANES, (g + 1) * LANES)
        o_ref[:, cols] = jnp.sum(w_ref[:, cols] * ca, axis=0, keepdims=True) + b_ref[:, cols]


def _adaln(cb, w, b, tn=1024):
    k, n = w.shape
    return pl.pallas_call(
        _adaln_kernel,
        grid=(n // tn,),
        in_specs=[
            pl.BlockSpec((k, LANES), lambda j: (0, 0)),
            pl.BlockSpec((k, tn), lambda j: (0, j)),
            pl.BlockSpec((1, tn), lambda j: (0, j)),
        ],
        out_specs=pl.BlockSpec((1, tn), lambda j: (0, j)),
        out_shape=jax.ShapeDtypeStruct((1, n), F32),
        compiler_params=_params("arbitrary"),
        name="adaln_matvec",
    )(cb, w, b.reshape(1, n))


def _col_groups(d):
    return [slice(k * LANES, (k + 1) * LANES) for k in range(d // LANES)]


def _rms_stats(x_ref, rstd_scr):
    tm, d = x_ref.shape

    def body(r, carry):
        base = pl.multiple_of(r * STATS_ROWS, STATS_ROWS)
        parts = []
        for s in range(STATS_ROWS // SUBLANES):
            xr = x_ref[pl.ds(base + s * SUBLANES, SUBLANES), :]
            sq = xr * xr
            part = sq[:, :LANES]
            for cols in _col_groups(d)[1:]:
                part = part + sq[:, cols]
            parts.append(part)
        acc = jnp.concatenate(parts, axis=0)
        ms = jnp.sum(acc, axis=-1, keepdims=True) * (1.0 / d)
        rstd_scr[pl.ds(base, STATS_ROWS), :] = jnp.broadcast_to(lax.rsqrt(ms + EPS),
                                                               (STATS_ROWS, LANES))
        return carry

    lax.fori_loop(0, tm // STATS_ROWS, body, 0)


def _modnorm_apply(x_ref, rstd_scr, g_ref, sh_ref, sc_ref, mod, h_scr):
    tm, d = x_ref.shape
    vec = slice(mod, mod + 1)

    def body(r, carry):
        base = pl.multiple_of(r * NORM_ROWS, NORM_ROWS)
        gains = [g_ref[vec, cols] * (1.0 + sc_ref[vec, cols]) for cols in _col_groups(d)]
        for s in range(NORM_ROWS // BF16_ROWS):
            rows = pl.ds(base + s * BF16_ROWS, BF16_ROWS)
            rstd = rstd_scr[rows, :]
            for cols, gain in zip(_col_groups(d), gains):
                h_scr[rows, cols] = ((x_ref[rows, cols] * rstd) * gain
                                     + sh_ref[vec, cols]).astype(h_scr.dtype)
        return carry

    lax.fori_loop(0, tm // NORM_ROWS, body, 0)


def _row_tile_copy(x_hbm, x_buf, sem, i):
    tm = x_buf.shape[0]
    return pltpu.make_async_copy(x_hbm.at[pl.ds(i * tm, tm), :], x_buf, sem)


def _await_row_tile(x_hbm, x_buf, sem, i):
    @pl.when(i == 0)
    def _():
        _row_tile_copy(x_hbm, x_buf, sem, 0).start()

    _row_tile_copy(x_hbm, x_buf, sem, i).wait()


def _prefetch_next_row_tile(x_hbm, x_buf, sem, i):
    @pl.when(i + 1 < pl.num_programs(0))
    def _():
        _row_tile_copy(x_hbm, x_buf, sem, i + 1).start()


SIDE_COLS = 2048
ROW_SPLIT = 2


def _cast_cols(in_ref, o_ref, c0, o0):
    n = min(SIDE_COLS, in_ref.shape[1] - c0)
    o_ref[:, o0:o0 + n] = in_ref[:, c0:c0 + n].astype(o_ref.dtype)


def _matvec_cols(cb_ref, w_ref, acc_ref, c0):
    ca = _silu(cb_ref[...])
    for k in range(c0 // LANES, min(c0 + SIDE_COLS, w_ref.shape[1]) // LANES):
        cols = slice(k * LANES, (k + 1) * LANES)
        prod = w_ref[:, cols] * ca
        part = prod[0:SUBLANES]
        for q in range(1, prod.shape[0] // SUBLANES):
            part = part + prod[q * SUBLANES:(q + 1) * SUBLANES]
        acc_ref[:, cols] += part


def _conv_in_kernel(*refs, cast_widths, n_matvec):
    refs = list(refs)
    take = lambda n: [refs.pop(0) for _ in range(n)]
    x_hbm, g_ref, sh_ref, sc_ref, wb_ref, wc_ref, wu_ref, wz_ref, cw_ref = take(9)
    cast_in = [take(len(widths)) for widths in cast_widths]
    cb_ref = take(1)[0] if n_matvec else None
    mv_in = [take(2) for _ in range(n_matvec)]
    y_ref = take(1)[0]
    cast_out = take(len(cast_widths))
    mv_out = take(n_matvec)
    x_buf, x_sem, h_scr, rstd_scr, halo_scr = take(5)
    mv_acc = take(n_matvec)

    i = pl.program_id(0)
    j = pl.program_id(1)
    tm = y_ref.shape[0]
    step = i * pl.num_programs(1) + j
    last_step = pl.num_programs(0) * pl.num_programs(1) - 1

    @pl.when(step == 0)
    def _():
        for acc_ref in mv_acc:
            acc_ref[...] = jnp.zeros(acc_ref.shape, F32)

    @pl.when(j == 0)
    def _():
        _await_row_tile(x_hbm, x_buf, x_sem, i)
        _rms_stats(x_buf, rstd_scr)
        _modnorm_apply(x_buf, rstd_scr, g_ref, sh_ref, sc_ref, 0, h_scr)
        _prefetch_next_row_tile(x_hbm, x_buf, x_sem, i)

    @pl.when(i == 0)
    def _():
        halo_scr[j] = jnp.zeros(halo_scr.shape[1:], F32)

    side_jobs = []
    for ins, out in zip(cast_in, cast_out):
        col = 0
        for r in ins:
            for c0 in range(0, r.shape[1], SIDE_COLS):
                side_jobs.append(functools.partial(_cast_cols, r, out, c0, col + c0))
            col += r.shape[1]
    for (w_ref, _), acc_ref in zip(mv_in, mv_acc):
        for c0 in range(0, w_ref.shape[1], SIDE_COLS):
            side_jobs.append(functools.partial(_matvec_cols, cb_ref, w_ref, acc_ref, c0))
    proj = [[None] * ROW_SPLIT for _ in range(4)]
    n_gaps = 4 * ROW_SPLIT - 1
    rows_per = tm // ROW_SPLIT
    for n, (half, k) in enumerate((half, k) for half in range(ROW_SPLIT) for k in range(4)):
        w_ref = (wc_ref, wu_ref, wb_ref, wz_ref)[k]
        rows = slice(half * rows_per, (half + 1) * rows_per)
        proj[k][half] = jnp.dot(h_scr[rows, :], w_ref[...], preferred_element_type=F32)
        if n < n_gaps:
            for job in side_jobs[n::n_gaps]:
                job()
    c_proj, u_proj, b_proj, z_proj = [jnp.concatenate(p, axis=0) for p in proj]

    cu = c_proj * u_proj
    w0 = cw_ref[0:1, :]
    w1 = cw_ref[1:2, :]
    w2 = cw_ref[2:3, :]
    conv = w0 * pltpu.roll(cu, 2, axis=0) + w1 * pltpu.roll(cu, 1, axis=0) + w2 * cu
    gate = b_proj * _silu(z_proj)
    y_ref[...] = (gate * conv).astype(y_ref.dtype)

    prev = halo_scr[j]
    top = cu[0:SUBLANES]
    row = lax.broadcasted_iota(jnp.int32, top.shape, 0)
    back1 = jnp.where(row < 1, pltpu.roll(prev, 1, axis=0), pltpu.roll(top, 1, axis=0))
    back2 = jnp.where(row < 2, pltpu.roll(prev, 2, axis=0), pltpu.roll(top, 2, axis=0))
    conv_top = w0 * back2 + w1 * back1 + w2 * top
    y_ref[0:SUBLANES, :] = (gate[0:SUBLANES] * conv_top).astype(y_ref.dtype)
    halo_scr[j] = cu[tm - SUBLANES:tm]

    if n_matvec:
        @pl.when(step == last_step)
        def _():
            for (_, b_ref), o_ref, acc_ref in zip(mv_in, mv_out, mv_acc):
                o_ref[...] = jnp.sum(acc_ref[...], axis=0, keepdims=True) + b_ref[...]


CONV_TM, CONV_TN = 1024, 256
BF16_ROWS = 16


def _side_rows(s, c, k):
    steps = (s // CONV_TM) * (c // CONV_TN)
    rows = k // steps
    return rows if rows * steps == k and rows % BF16_ROWS == 0 else 0


def _conv_in(x, g, shift, scale, w_in, conv_w, cast_jobs=(), matvec_jobs=(), cb=None):
    tm, tn = CONV_TM, CONV_TN
    s, d = x.shape
    c = conv_w.shape[1]
    nj = c // tn
    vec = pl.BlockSpec((1, d), lambda i, j: (0, 0))
    w_specs = [pl.BlockSpec((d, tn), lambda i, j, k=k: (0, k * nj + j)) for k in range(4)]
    row_block = lambda rows, n: pl.BlockSpec((rows, n), lambda i, j: (i * nj + j, 0))
    whole = lambda n: pl.BlockSpec((1, n), lambda i, j: (0, 0))

    side_in, side_in_specs = [], []
    out_shapes = [jax.ShapeDtypeStruct((s, c), BF16)]
    out_specs = [pl.BlockSpec((tm, tn), lambda i, j: (i, j))]
    scratch = [pltpu.VMEM((tm, d), F32), pltpu.SemaphoreType.DMA(()),
               pltpu.VMEM((tm, d), BF16), pltpu.VMEM((tm, LANES), F32),
               pltpu.VMEM((nj, SUBLANES, tn), F32)]
    for job in cast_jobs:
        k = job[0].shape[0]
        rows = _side_rows(s, c, k)
        assert rows and all(a.shape[0] == k and a.shape[1] % LANES == 0 for a in job)
        side_in += list(job)
        side_in_specs += [row_block(rows, a.shape[1]) for a in job]
        n = sum(a.shape[1] for a in job)
        out_shapes.append(jax.ShapeDtypeStruct((k, n), BF16))
        out_specs.append(row_block(rows, n))
    if matvec_jobs:
        k = cb.shape[0]
        rows = _side_rows(s, c, k)
        assert rows
        side_in.append(cb)
        side_in_specs.append(row_block(rows, LANES))
    for w, b in matvec_jobs:
        n = w.shape[1]
        assert w.shape[0] == k and n % LANES == 0
        side_in += [w, b.reshape(1, n)]
        side_in_specs += [row_block(rows, n), whole(n)]
        out_shapes.append(jax.ShapeDtypeStruct((1, n), F32))
        out_specs.append(whole(n))
        scratch.append(pltpu.VMEM((SUBLANES, n), F32))

    body = functools.partial(_conv_in_kernel, cast_widths=tuple(tuple(a.shape[1] for a in job)
                                                                for job in cast_jobs),
                             n_matvec=len(matvec_jobs))
    outs = pl.pallas_call(
        body,
        grid=(s // tm, nj),
        in_specs=[pl.BlockSpec(memory_space=pl.ANY), vec, vec, vec, *w_specs,
                  pl.BlockSpec((CONV_WIDTH, tn), lambda i, j: (0, j)), *side_in_specs],
        out_specs=out_specs,
        out_shape=out_shapes,
        scratch_shapes=scratch,
        compiler_params=_params("arbitrary", "arbitrary"),
        name="conv_in_proj",
    )(x, g, shift, scale, w_in, w_in, w_in, w_in, conv_w, *side_in)
    n_cast = len(cast_jobs)
    return outs[0], list(outs[1:1 + n_cast]), list(outs[1 + n_cast:])


MXU_COLS = 256


def _out_proj_kernel(y_ref, w_ref, x_ref, gate_ref, o_ref):
    for c0 in range(0, o_ref.shape[1], MXU_COLS):
        cols = slice(c0, c0 + MXU_COLS)
        acc = jnp.dot(y_ref[...], w_ref[:, cols], preferred_element_type=F32)
        o_ref[:, cols] = x_ref[:, cols] + gate_ref[:, cols] * acc


def _out_proj(y, w, x, gate, tm=1024, tn=1024):
    s, k = y.shape
    d = w.shape[1]
    return pl.pallas_call(
        _out_proj_kernel,
        grid=(s // tm, d // tn),
        in_specs=[pl.BlockSpec((tm, k), lambda i, j: (i, 0)),
                  pl.BlockSpec((k, tn), lambda i, j: (0, j)),
                  pl.BlockSpec((tm, tn), lambda i, j: (i, j)),
                  pl.BlockSpec((1, tn), lambda i, j: (0, j))],
        out_specs=pl.BlockSpec((tm, tn), lambda i, j: (i, j)),
        out_shape=jax.ShapeDtypeStruct((s, d), F32),
        compiler_params=_params("arbitrary", "arbitrary"),
        name="out_proj",
    )(y, w, x, gate)


def _out_proj_norm_kernel(y_ref, w_ref, x_ref, gate_ref, gf_ref, o_ref):
    tm, d = o_ref.shape
    ssq = jnp.zeros((tm, LANES), F32)
    for c0 in range(0, d, MXU_COLS):
        cols = slice(c0, c0 + MXU_COLS)
        acc = jnp.dot(y_ref[...], w_ref[:, cols], preferred_element_type=F32)
        x2 = x_ref[:, cols] + gate_ref[:, cols] * acc
        o_ref[:, cols] = x2
        for l0 in range(0, MXU_COLS, LANES):
            ssq = ssq + x2[:, l0:l0 + LANES] * x2[:, l0:l0 + LANES]
    rstd = lax.rsqrt(jnp.sum(ssq, axis=-1, keepdims=True) * (1.0 / d) + EPS)
    o_ref[...] = (o_ref[...] * rstd) * gf_ref[...]


def _out_proj_norm(y, w, x, gate, g_final, tm=256):
    s, k = y.shape
    d = w.shape[1]
    vec = pl.BlockSpec((1, d), lambda i: (0, 0))
    return pl.pallas_call(
        _out_proj_norm_kernel,
        grid=(s // tm,),
        in_specs=[pl.BlockSpec((tm, k), lambda i: (i, 0)),
                  pl.BlockSpec((k, d), lambda i: (0, 0), pipeline_mode=pl.Buffered(1)),
                  pl.BlockSpec((tm, d), lambda i: (i, 0)), vec, vec],
        out_specs=pl.BlockSpec((tm, d), lambda i: (i, 0)),
        out_shape=jax.ShapeDtypeStruct((s, d), F32),
        compiler_params=_params("arbitrary", vmem=60 * MIB),
        name="out_proj_final_norm",
    )(y, w, x, gate, g_final)


def _norm_proj_kernel(x_hbm, g_ref, sh_ref, sc_ref, w_ref, o_ref, x_buf, x_sem, h_scr, rstd_scr,
                      *, tail_tiles, scaled_tiles, scale, silu_tiles):
    i = pl.program_id(0)
    j = pl.program_id(1)

    @pl.when(j == 0)
    def _():
        _await_row_tile(x_hbm, x_buf, x_sem, i)
        _rms_stats(x_buf, rstd_scr)
        _modnorm_apply(x_buf, rstd_scr, g_ref, sh_ref, sc_ref, 1 if tail_tiles else 0, h_scr)
        if not tail_tiles:
            _prefetch_next_row_tile(x_hbm, x_buf, x_sem, i)

    if tail_tiles:
        @pl.when(j == tail_tiles)
        def _():
            _modnorm_apply(x_buf, rstd_scr, g_ref, sh_ref, sc_ref, 0, h_scr)
            _prefetch_next_row_tile(x_hbm, x_buf, x_sem, i)

    gated = (j >= tail_tiles + silu_tiles[0]) & (j < tail_tiles + silu_tiles[1])

    @pl.when(gated)
    def _():
        acc = jnp.dot(h_scr[...], w_ref[...], preferred_element_type=F32)
        o_ref[...] = _silu(acc).astype(o_ref.dtype)

    @pl.when(jnp.logical_not(gated))
    def _():
        acc = jnp.dot(h_scr[...], w_ref[...], preferred_element_type=F32)
        if scaled_tiles:
            scaled = (j >= tail_tiles) & (j < tail_tiles + scaled_tiles)
            acc = acc * jnp.where(scaled, scale, 1.0)
        o_ref[...] = acc.astype(o_ref.dtype)


def _norm_proj(x, g, shift, scale, w, tail_cols=0, scaled_cols=0, col_scale=1.0,
               silu_cols=(0, 0), tm=1024, tn=1024):
    s, d = x.shape
    n = w.shape[1]
    n_mod = g.shape[0]
    assert scaled_cols % tn == 0 and tail_cols % tn == 0 and n_mod == (2 if tail_cols else 1)
    assert all(c % tn == 0 and scaled_cols <= c <= n - tail_cols for c in silu_cols)
    n_tiles, tail_tiles = n // tn, tail_cols // tn
    col_tile = lambda j: (j + n_tiles - tail_tiles) % n_tiles
    vec = pl.BlockSpec((n_mod, d), lambda i, j: (0, 0))
    body = functools.partial(_norm_proj_kernel, tail_tiles=tail_tiles,
                             scaled_tiles=scaled_cols // tn, scale=col_scale,
                             silu_tiles=tuple(c // tn for c in silu_cols))
    return pl.pallas_call(
        body,
        grid=(s // tm, n_tiles),
        in_specs=[pl.BlockSpec(memory_space=pl.ANY), vec, vec, vec,
                  pl.BlockSpec((d, tn), lambda i, j: (0, col_tile(j)))],
        out_specs=pl.BlockSpec((tm, tn), lambda i, j: (i, col_tile(j))),
        out_shape=jax.ShapeDtypeStruct((s, n), BF16),
        scratch_shapes=[pltpu.VMEM((tm, d), F32), pltpu.SemaphoreType.DMA(()),
                        pltpu.VMEM((tm, d), BF16), pltpu.VMEM((tm, LANES), F32)],
        compiler_params=_params("arbitrary", "arbitrary"),
        name="norm_proj",
    )(x, g, shift, scale, w)


def _t5_bucket(dist):
    max_exact = N_BUCKETS // 2
    d = jnp.maximum(dist, 0)
    d_f = jnp.maximum(d, 1).astype(F32)
    large = max_exact + (jnp.log(d_f / max_exact) / math.log(MAX_DISTANCE / max_exact)
                         * (N_BUCKETS - max_exact)).astype(jnp.int32)
    large = jnp.minimum(large, N_BUCKETS - 1)
    return jnp.where(d < max_exact, d, large)


HALF = BLOCK // 2
HALF_LANES = GROUP * HALF
KEY_WINDOW = 3 * HALF


def _regroup_halves(a, b, low_lanes):
    return (jnp.where(low_lanes, a, pltpu.roll(b, HALF, axis=1)),
            jnp.where(low_lanes, pltpu.roll(a, HALF, axis=1), b))


def _bias_table_kernel(rb_ref, bucket_ref, o_ref):
    g = pl.program_id(0)
    bucket = bucket_ref[...]
    b = lax.broadcasted_iota(jnp.int32, (2 * BLOCK, BLOCK), 0)
    a = lax.broadcasted_iota(jnp.int32, (2 * BLOCK, BLOCK), 1)
    dist = a + BLOCK - b
    band = (dist >= 0) & (dist < BLOCK)
    masks = (band & (b >= BLOCK), band)
    low_lanes = a < HALF

    def head_tile(hh):
        by_dist = jnp.zeros(bucket.shape, F32)
        for k in range(N_BUCKETS):
            by_dist = jnp.where(bucket == k, rb_ref[k, g * GROUP + hh] * LOG2E, by_dist)
        rows = jnp.concatenate([by_dist] * (2 * BLOCK // SUBLANES), axis=0)
        return pltpu.roll(rows, 0, axis=1, stride=1, stride_axis=0)

    for t in range(GROUP // 2):
        even, odd = head_tile(2 * t), head_tile(2 * t + 1)
        for e, mask in enumerate(masks):
            halves = _regroup_halves(jnp.where(mask, even, -jnp.inf),
                                     jnp.where(mask, odd, -jnp.inf), low_lanes)
            for half, tile in enumerate(halves):
                lane0 = half * HALF_LANES + t * LANES
                o_ref[e, 0, :, lane0:lane0 + LANES] = tile


def _bias_table(rel_bias):
    n_kv = rel_bias.shape[1] // GROUP
    dist_bucket = jnp.broadcast_to(_t5_bucket(jnp.arange(BLOCK)), (SUBLANES, BLOCK))
    return pl.pallas_call(
        _bias_table_kernel,
        grid=(n_kv,),
        in_specs=[pl.BlockSpec(memory_space=pltpu.SMEM),
                  pl.BlockSpec((SUBLANES, BLOCK), lambda g: (0, 0))],
        out_specs=pl.BlockSpec((2, 1, 2 * BLOCK, GROUP * BLOCK), lambda g: (0, g, 0, 0)),
        out_shape=jax.ShapeDtypeStruct((2, n_kv, 2 * BLOCK, GROUP * BLOCK), F32),
        compiler_params=_params("arbitrary"),
        name="bias_table",
    )(rel_bias, dist_bucket.astype(jnp.int32))


ATTN_BLOCKS = 4


def _attn_kernel(sink_ref, qz_ref, kvp_ref, kvc_ref, bias_ref, o_ref):
    n_kv = bias_ref.shape[1]
    attn_dim, kv_dim = o_ref.shape[1], n_kv * HEAD_DIM
    for blk in range(ATTN_BLOCKS):
        rows = slice(blk * BLOCK, (blk + 1) * BLOCK)
        kv_prev = kvp_ref if blk == 0 else kvc_ref.at[(blk - 1) * BLOCK:blk * BLOCK, :]
        kv_cur = kvc_ref.at[rows, :]
        entry = jnp.where(pl.program_id(0) == 0, 0, 1) if blk == 0 else 1
        _attn_block(sink_ref, qz_ref.at[rows, :attn_dim], qz_ref.at[rows, attn_dim:],
                    kv_prev.at[:, :kv_dim], kv_cur.at[:, :kv_dim],
                    kv_prev.at[:, kv_dim:], kv_cur.at[:, kv_dim:],
                    bias_ref.at[entry], o_ref.at[rows, :])


def _attn_block(sink_ref, q_ref, z_ref, kp_ref, kc_ref, vp_ref, vc_ref, bias_ref, o_ref):
    n_kv = bias_ref.shape[0]
    tiles = GROUP // 2
    ones_rows = jnp.ones((BF16_ROWS, 2 * BLOCK), BF16)
    zero_keys = jnp.zeros((2 * BLOCK - KEY_WINDOW, HALF_LANES), BF16)
    low_lanes = lambda rows: lax.broadcasted_iota(jnp.int32, (rows, LANES), 1) < HALF
    nt_dot = functools.partial(lax.dot_general, dimension_numbers=(((1,), (1,)), ((), ())),
                               preferred_element_type=F32)
    windows = (slice(0, KEY_WINDOW), slice(2 * BLOCK - KEY_WINDOW, 2 * BLOCK))
    for g in range(n_kv):
        kv_cols = slice((g // 2) * LANES, (g // 2 + 1) * LANES)
        k2 = jnp.concatenate([kp_ref[:, kv_cols], kc_ref[:, kv_cols]], axis=0)
        v2_t = jnp.concatenate([vp_ref[:, kv_cols].T, vc_ref[:, kv_cols].T], axis=1)
        v_t = v2_t[:HEAD_DIM] if g % 2 == 0 else v2_t[HEAD_DIM:]
        v_t = jnp.concatenate([v_t, ones_rows], axis=0)
        k_other = pltpu.roll(k2, HEAD_DIM, axis=1)
        k_both = (jnp.where(low_lanes(2 * BLOCK), k2, k_other) if g % 2 == 0
                  else jnp.where(low_lanes(2 * BLOCK), k_other, k2))
        group_cols = slice(g * tiles * LANES, (g + 1) * tiles * LANES)
        q_slab = q_ref[:, group_cols]
        sink = jnp.concatenate(
            [jnp.where(low_lanes(1), sink_ref[g * GROUP + 2 * t] * LOG2E,
                       sink_ref[g * GROUP + 2 * t + 1] * LOG2E) for t in range(tiles)], axis=1)
        logits = []
        for half, win in enumerate(windows):
            q_half = q_slab[half * HALF:(half + 1) * HALF]
            q_rows = []
            for t in range(tiles):
                q_t = q_half[:, t * LANES:(t + 1) * LANES]
                q_rows += [jnp.where(low_lanes(HALF), q_t, jnp.zeros_like(q_t)),
                           jnp.where(low_lanes(HALF), jnp.zeros_like(q_t), q_t)]
            q_rows = jnp.concatenate(q_rows, axis=0)
            logits.append(nt_dot(k_both[win], q_rows))
        probs, maxes = [], []
        for half, win in enumerate(windows):
            s_t = logits[half] + bias_ref[g, win, half * HALF_LANES:(half + 1) * HALF_LANES]
            m = jnp.maximum(jnp.max(s_t, axis=0, keepdims=True), sink)
            p = jnp.exp2(s_t - m).astype(BF16)
            probs.append(p)
            maxes.append(m)
        v_wins = (v_t[:, :KEY_WINDOW],
                  pltpu.roll(v_t, KEY_WINDOW, axis=1)[:, :KEY_WINDOW])
        o_halves = []
        for half in range(2):
            o_h = jnp.dot(v_wins[half], probs[half], preferred_element_type=F32)
            den = o_h[HEAD_DIM:HEAD_DIM + 1] + jnp.exp2(sink - maxes[half])
            o_halves.append((o_h[:HEAD_DIM] / den).astype(o_ref.dtype))
        out_tiles = []
        for t in range(tiles):
            cols = slice(t * LANES, (t + 1) * LANES)
            even, odd = _regroup_halves(o_halves[0][:, cols], o_halves[1][:, cols],
                                        low_lanes(HEAD_DIM))
            out_tiles.append(jnp.concatenate([even, odd], axis=0).T)
        o_ref[:, group_cols] = jnp.concatenate(out_tiles, axis=1) * z_ref[:, group_cols]


def _attention(qz, qz_blk, kv, kv_blk, bias, sinks):
    s = qz.shape[0]
    n_kv = bias.shape[1]
    kv_dim = n_kv * HEAD_DIM
    attn_dim = kv_dim * GROUP
    assert n_kv % 2 == 0 and GROUP % 2 == 0 and 2 * HEAD_DIM == LANES and BLOCK == LANES
    rows = ATTN_BLOCKS * BLOCK
    assert s % rows == 0
    before = lambda i: jnp.maximum(i * ATTN_BLOCKS - 1, 0)
    return pl.pallas_call(
        _attn_kernel,
        grid=(s // rows,),
        in_specs=[pl.BlockSpec(memory_space=pltpu.SMEM),
                  pl.BlockSpec((rows, 2 * attn_dim), lambda i: (i, qz_blk)),
                  pl.BlockSpec((BLOCK, 2 * kv_dim), lambda i: (before(i), kv_blk)),
                  pl.BlockSpec((rows, 2 * kv_dim), lambda i: (i, kv_blk)),
                  pl.BlockSpec(bias.shape, lambda i: (0, 0, 0, 0),
                               pipeline_mode=pl.Buffered(1))],
        out_specs=pl.BlockSpec((rows, attn_dim), lambda i: (i, 0)),
        out_shape=jax.ShapeDtypeStruct((s, attn_dim), BF16),
        compiler_params=_params("arbitrary"),
        name="swa_attention",
    )(sinks, qz, kv, kv, bias)


def _trunk(x, c, g_a, w_mod_a, b_mod_a, w_a_in, conv_a, w_a_out, g_kv, w_mod_kv, b_mod_kv,
           w_kv, rel_bias, g_b, w_mod_b, b_mod_b, w_b_in, sinks_b, w_b_out, g_final):
    s, d = x.shape
    n_a, n_b = w_a_in.shape[0], w_b_in.shape[0]
    assert n_b >= 1 and s % BLOCK == 0
    row = lambda v: v.reshape(1, -1)
    cb = jnp.broadcast_to(c.reshape(d, 1), (d, LANES))

    side_operands = (w_a_out[-1], w_b_in[0], w_kv, w_b_out[0], w_mod_kv, w_mod_b[0]) if n_a else ()
    stream = n_a > 0 and all(_side_rows(s, conv_a.shape[2], a.shape[0]) for a in side_operands)
    w_a_out_bf = w_cat = w_b_out_bf = mod_kv = mod_b = None
    for i in range(n_a):
        shift, scale, gate = jnp.split(_adaln(cb, w_mod_a[i], b_mod_a[i]), 3, axis=-1)
        args = (x, row(g_a[i]), shift, scale, w_a_in[i].astype(BF16), conv_a[i])
        if stream and i == n_a - 1:
            y, (w_a_out_bf, w_cat, w_b_out_bf), (mod_kv, mod_b) = _conv_in(
                *args, cast_jobs=((w_a_out[i],), (w_b_in[0], w_kv), (w_b_out[0],)),
                matvec_jobs=((w_mod_kv, b_mod_kv), (w_mod_b[0], b_mod_b[0])), cb=cb)
        else:
            y, _, _ = _conv_in(*args)
            w_a_out_bf = w_a_out[i].astype(BF16)
        x = _out_proj(y, w_a_out_bf, x, gate)

    if mod_kv is None:
        mod_kv = _adaln(cb, w_mod_kv, b_mod_kv)
    shift_kv, scale_kv = jnp.split(mod_kv, 2, axis=-1)
    bias = _bias_table(rel_bias)
    attn_dim = w_b_in.shape[2] // 2
    kv_dim = w_kv.shape[1] // 2
    kv = None
    for i in range(n_b):
        if i > 0 or mod_b is None:
            mod_b = _adaln(cb, w_mod_b[i], b_mod_b[i])
        shift, scale, gate = jnp.split(mod_b, 3, axis=-1)
        q_scale = dict(scaled_cols=attn_dim, col_scale=LOG2E * HEAD_DIM ** -0.5,
                       silu_cols=(attn_dim, 2 * attn_dim))
        if i == 0:
            if w_cat is None:
                w_cat = jnp.concatenate([w_b_in[i], w_kv], axis=1).astype(BF16)
            qz = _norm_proj(x, jnp.stack([g_b[i], g_kv]), jnp.concatenate([shift, shift_kv]),
                            jnp.concatenate([scale, scale_kv]), w_cat,
                            tail_cols=2 * kv_dim, **q_scale)
            kv, kv_blk = qz, attn_dim // kv_dim
        else:
            qz = _norm_proj(x, row(g_b[i]), shift, scale, w_b_in[i].astype(BF16), **q_scale)
        a = _attention(qz, 0, kv, kv_blk, bias, sinks_b[i])
        w_out = w_b_out_bf if i == 0 and w_b_out_bf is not None else w_b_out[i].astype(BF16)
        if i + 1 < n_b:
            x = _out_proj(a, w_out, x, gate)
        else:
            x = _out_proj_norm(a, w_out, x, gate, row(g_final))
    return x


def kernel(x, c, g_a, w_mod_a, b_mod_a, w_a_in, conv_a, w_a_out, g_kv, w_mod_kv, b_mod_kv, w_kv,
           rel_bias, g_b, w_mod_b, b_mod_b, w_b_in, sinks_b, w_b_out, g_final):
    outs = [_trunk(x[b], c[b], g_a, w_mod_a, b_mod_a, w_a_in, conv_a, w_a_out, g_kv, w_mod_kv,
                   b_mod_kv, w_kv, rel_bias, g_b, w_mod_b, b_mod_b, w_b_in, sinks_b, w_b_out,
                   g_final) for b in range(x.shape[0])]
    return jnp.stack(outs, axis=0)
```
